```python
import jax
import jax.numpy as jnp
from jax import lax
import numpy as np


D_MODEL = 1024
BATCH = 16
SEQ = 4096
DEPTH = 4

N_MIXERS = 3
N_A = (DEPTH + 2) // 3
N_B = (DEPTH + 1) // 3
N_C = DEPTH // 3
N_SUB = 3
D_FF = 2816
NORM_EPS = 1e-6

RWKV_N = 64
RWKV_H = D_MODEL // RWKV_N
LORA_DECAY = 64
LORA_ICLR = 64
LORA_VRES = 32
LORA_GATE = 128
RWKV_GN_EPS = 64e-5

RET_H = 4
RET_DK = D_MODEL // RET_H
RET_DV = 2 * RET_DK
RET_CHUNK = 128
ROPE_BASE = 10000.0
HEAD_NORM_EPS = 1e-5

LRU_WIDTH = 1280
LRU_H = 5
LRU_BW = LRU_WIDTH // LRU_H
CONV_W = 4
LRU_C = 8.0

kernel_name = 'hybrid_rwkv7_retnet_rglru_macaron_adaln'


def rms_norm(x, g):
    xf = x.astype(jnp.float32)
    return xf * lax.rsqrt(jnp.mean(xf * xf, axis=-1, keepdims=True) + NORM_EPS) * g


def head_norm(y, eps):
    yc = y - jnp.mean(y, axis=-1, keepdims=True)
    return yc * lax.rsqrt(jnp.mean(yc * yc, axis=-1, keepdims=True) + eps)


def swiglu(h, w_in, w_out):
    a, b = jnp.split(h @ w_in, 2, axis=-1)
    return (jax.nn.silu(a) * b) @ w_out


def token_shift(h):
    return jnp.pad(h, ((0, 0), (1, 0), (0, 0)))[:, :-1]


def sublayer_in(res, g_pre, shift, scale):
    return rms_norm(res, g_pre) * (1.0 + scale[:, None]) + shift[:, None]


def sublayer_out(res, y, g_post, gate, weight):
    return res + weight * gate[:, None] * rms_norm(y, g_post)


def wkv7_scan(r, w, k, v, a, b):
    bsz, _, nh, n = r.shape

    def step(S, inp):
        r_t, w_t, k_t, v_t, a_t, b_t = inp
        sa = jnp.einsum('bhvk,bhk->bhv', S, a_t)
        S = S * w_t[:, :, None, :] + sa[..., None] * b_t[:, :, None, :] + v_t[..., None] * k_t[:, :, None, :]
        return S, jnp.einsum('bhvk,bhk->bhv', S, r_t)

    xs = tuple(jnp.moveaxis(t.astype(jnp.float32), 1, 0) for t in (r, w, k, v, a, b))
    _, ys = lax.scan(step, jnp.zeros((bsz, nh, n, n), jnp.float32), xs)
    return jnp.moveaxis(ys, 0, 1)


def rwkv7_mix(h, v_first, mu, w_rkv, w0, w1, w2, a0, a1, a2, g1, g2, k_k, k_a, r_k, ln_w, ln_b, w_o, vres):
    bsz, t_len, d = h.shape
    dx = token_shift(h) - h
    xr, xw, xk, xv, xa, xg = (h + dx * mu[j] for j in range(6))
    r = xr @ w_rkv[0]
    k = xk @ w_rkv[1]
    v = xv @ w_rkv[2]
    decay = jnp.exp(-jnp.exp(-jax.nn.softplus(-(w0 + jnp.tanh(xw @ w1) @ w2)) - 0.5))
    a = jax.nn.sigmoid(a0 + (xa @ a1) @ a2)
    g = jax.nn.sigmoid(xg @ g1) @ g2
    if vres is None:
        v_first = v
    else:
        v0, v1, v2 = vres
        v = v + (v_first - v) * jax.nn.sigmoid(v0 + (xv @ v1) @ v2)
    heads = lambda t: t.reshape(bsz, t_len, RWKV_H, RWKV_N)
    kk = heads(k * k_k)
    kk = kk / jnp.maximum(jnp.sqrt(jnp.sum(kk * kk, axis=-1, keepdims=True)), 1e-12)
    k = heads(k * (1.0 + (a - 1.0) * k_a))
    r, v, a, decay = heads(r), heads(v), heads(a), heads(decay)
    y = wkv7_scan(r, decay, k, v, -kk, kk * a)
    y = head_norm(y, RWKV_GN_EPS).reshape(bsz, t_len, d) * ln_w + ln_b
    bonus = jnp.sum(r * k * r_k, axis=-1, keepdims=True) * v
    return ((y + bonus.reshape(bsz, t_len, d)) * g) @ w_o, v_first


def rotary(x, pos):
    half = x.shape[-1] // 2
    inv = 1.0 / (ROPE_BASE ** jnp.linspace(0.0, 1.0, half, dtype=jnp.float32))
    ang = pos.astype(jnp.float32)[..., None] * inv
    cos = jnp.cos(ang)[:, :, None, :]
    sin = jnp.sin(ang)[:, :, None, :]
    x1, x2 = x[..., :half], x[..., half:]
    return jnp.concatenate([x1 * cos - x2 * sin, x1 * sin + x2 * cos], axis=-1)


def retention_mix(h, pos, w_in, w_o):
    bsz, t_len, _ = h.shape
    n_chunks = t_len // RET_CHUNK
    qk = RET_H * RET_DK
    q, k, v, g = jnp.split(h @ w_in, [qk, 2 * qk, 2 * qk + RET_H * RET_DV], axis=-1)
    q = rotary(q.reshape(bsz, t_len, RET_H, RET_DK), pos).astype(jnp.float32)
    k = (rotary(k.reshape(bsz, t_len, RET_H, RET_DK), pos) * RET_DK ** -0.5).astype(jnp.float32)
    v = v.reshape(bsz, t_len, RET_H, RET_DV).astype(jnp.float32)
    log_gamma = jnp.log(1.0 - 2.0 ** (-5.0 - jnp.arange(RET_H, dtype=jnp.float32)))
    idx = jnp.arange(RET_CHUNK, dtype=jnp.float32)
    diff = idx[:, None] - idx[None, :]
    inner = jnp.where(diff >= 0, jnp.exp(log_gamma[:, None, None] * jnp.maximum(diff, 0.0)), 0.0)
    q_decay = jnp.exp(log_gamma[None, :] * (idx[:, None] + 1.0))[None, :, :, None]
    k_decay = jnp.exp(log_gamma[None, :] * (RET_CHUNK - 1.0 - idx[:, None]))[None, :, :, None]
    chunk_decay = jnp.exp(log_gamma * RET_CHUNK)[None, :, None, None]
    to_chunks = lambda t: jnp.moveaxis(t.reshape(bsz, n_chunks, RET_CHUNK, RET_H, -1), 1, 0)

    def step(R, inp):
        q_c, k_c, v_c = inp
        s = jnp.einsum('bihd,bjhd->bhij', q_c, k_c) * inner
        o = jnp.einsum('bhij,bjhe->bihe', s, v_c)
        o = o + jnp.einsum('bihd,bhde->bihe', q_c, R) * q_decay
        R = R * chunk_decay + jnp.einsum('bjhd,bjhe->bhde', k_c * k_decay, v_c)
        return R, o

    R0 = jnp.zeros((bsz, RET_H, RET_DK, RET_DV), jnp.float32)
    _, o = lax.scan(step, R0, (to_chunks(q), to_chunks(k), to_chunks(v)))
    o = jnp.moveaxis(o, 0, 1).reshape(bsz, t_len, RET_H, RET_DV)
    o = head_norm(o, HEAD_NORM_EPS).reshape(bsz, t_len, RET_H * RET_DV)
    return (jax.nn.silu(g) * o) @ w_o


def rglru_mix(h, w_in, conv_w, conv_b, gate_w, gate_b, lam, w_o):
    bsz, t_len, _ = h.shape
    gate_branch, xb = jnp.split(h @ w_in, 2, axis=-1)
    xc = lax.conv_general_dilated(
        xb.astype(jnp.float32), conv_w.astype(jnp.float32)[:, None, :],
        window_strides=(1,), padding=[(CONV_W - 1, 0)],
        dimension_numbers=('NWC', 'WIO', 'NWC'), feature_group_count=LRU_WIDTH) + conv_b
    xg = xc.reshape(bsz, t_len, LRU_H, LRU_BW)
    gates = jax.nn.sigmoid(jnp.einsum('bthi,ghij->gbthj', xg, gate_w) + gate_b[:, None, None])
    i_gate = gates[0].reshape(bsz, t_len, LRU_WIDTH)
    r_gate = gates[1].reshape(bsz, t_len, LRU_WIDTH)
    log_a = -LRU_C * r_gate * jax.nn.softplus(-lam)
    a = jnp.exp(log_a)
    u = jnp.sqrt(-jnp.expm1(2.0 * log_a)) * (i_gate * xc)

    def combine(left, right):
        a_l, u_l = left
        a_r, u_r = right
        return a_l * a_r, a_r * u_l + u_r

    _, hs = lax.associative_scan(combine, (a, u), axis=1)
    return (jax.nn.gelu(gate_branch) * hs) @ w_o


def setup_inputs(seed: int = 0) -> dict:
    key = jax.random.key(seed)
    ks = iter(jax.random.split(key, 40))
    nrm = lambda shape, fan_in, s=1.0: jax.random.normal(next(ks), shape, jnp.float32) * (s * fan_in ** -0.5)
    rnd = lambda shape, s: jax.random.normal(next(ks), shape, jnp.float32) * s
    unif = lambda shape, lo, hi: jax.random.uniform(next(ks), shape, jnp.float32, lo, hi)
    D, F = D_MODEL, D_FF
    x = jax.random.normal(next(ks), (BATCH, SEQ, D), jnp.float32)
    c = jax.random.normal(next(ks), (BATCH, D), jnp.float32)
    start = jax.random.randint(next(ks), (BATCH, 1), 0, 1024, dtype=jnp.int32)
    positions = (start + jnp.arange(SEQ, dtype=jnp.int32)[None, :]).astype(jnp.int32)
    ada_w = nrm((DEPTH, D, 3 * N_SUB * D), D, 0.5)
    ada_b = rnd((DEPTH, 3 * N_SUB * D), 0.01)
    norm_g = 1.0 + rnd((DEPTH, 2 * N_SUB, D), 0.05)
    ffn_w_in = nrm((DEPTH, 2, D, 2 * F), D)
    ffn_w_out = nrm((DEPTH, 2, F, D), F)
    rwkv_mu = unif((N_A, 6, D), 0.0, 1.0)
    rwkv_w_rkv = nrm((N_A, 3, D, D), D)
    rwkv_w0 = unif((N_A, D), -6.0, 0.0)
    rwkv_w1 = nrm((N_A, D, LORA_DECAY), D)
    rwkv_w2 = nrm((N_A, LORA_DECAY, D), LORA_DECAY, 0.1)
    rwkv_a0 = rnd((N_A, D), 0.1)
    rwkv_a1 = nrm((N_A, D, LORA_ICLR), D)
    rwkv_a2 = nrm((N_A, LORA_ICLR, D), LORA_ICLR, 0.1)
    rwkv_g1 = nrm((N_A, D, LORA_GATE), D)
    rwkv_g2 = nrm((N_A, LORA_GATE, D), LORA_GATE)
    rwkv_k_k = 0.85 + rnd((N_A, D), 0.05)
    rwkv_k_a = 1.0 + rnd((N_A, D), 0.05)
    rwkv_r_k = rnd((N_A, RWKV_H, RWKV_N), 0.1)
    rwkv_ln_w = 1.0 + rnd((N_A, D), 0.05)
    rwkv_ln_b = rnd((N_A, D), 0.01)
    rwkv_w_o = nrm((N_A, D, D), D)
    rwkv_v0 = 1.0 + rnd((N_A - 1, D), 0.1)
    rwkv_v1 = nrm((N_A - 1, D, LORA_VRES), D)
    rwkv_v2 = nrm((N_A - 1, LORA_VRES, D), LORA_VRES, 0.1)
    ret_w_in = nrm((N_B, D, 2 * RET_H * RET_DK + 2 * RET_H * RET_DV), D)
    ret_w_o = nrm((N_B, RET_H * RET_DV, D), RET_H * RET_DV)
    lru_w_in = nrm((N_C, D, 2 * LRU_WIDTH), D)
    lru_conv_w = nrm((N_C, CONV_W, LRU_WIDTH), CONV_W)
    lru_conv_b = rnd((N_C, LRU_WIDTH), 0.01)
    lru_gate_w = nrm((N_C, 2, LRU_H, LRU_BW, LRU_BW), LRU_BW)
    lru_gate_b = rnd((N_C, 2, LRU_H, LRU_BW), 0.01)
    rad = jnp.sqrt(unif((N_C, LRU_WIDTH), 0.81, 0.998))
    lru_lambda = jnp.log(rad) - jnp.log1p(-rad)
    lru_w_o = nrm((N_C, LRU_WIDTH, D), LRU_WIDTH)
    return {'x': x, 'c': c, 'positions': positions, 'ada_w': ada_w, 'ada_b': ada_b, 'norm_g': norm_g,
            'ffn_w_in': ffn_w_in, 'ffn_w_out': ffn_w_out,
            'rwkv_mu': rwkv_mu, 'rwkv_w_rkv': rwkv_w_rkv, 'rwkv_w0': rwkv_w0, 'rwkv_w1': rwkv_w1,
            'rwkv_w2': rwkv_w2, 'rwkv_a0': rwkv_a0, 'rwkv_a1': rwkv_a1, 'rwkv_a2': rwkv_a2,
            'rwkv_g1': rwkv_g1, 'rwkv_g2': rwkv_g2, 'rwkv_k_k': rwkv_k_k, 'rwkv_k_a': rwkv_k_a,
            'rwkv_r_k': rwkv_r_k, 'rwkv_ln_w': rwkv_ln_w, 'rwkv_ln_b': rwkv_ln_b, 'rwkv_w_o': rwkv_w_o,
            'rwkv_v0': rwkv_v0, 'rwkv_v1': rwkv_v1, 'rwkv_v2': rwkv_v2,
            'ret_w_in': ret_w_in, 'ret_w_o': ret_w_o,
            'lru_w_in': lru_w_in, 'lru_conv_w': lru_conv_w, 'lru_conv_b': lru_conv_b,
            'lru_gate_w': lru_gate_w, 'lru_gate_b': lru_gate_b, 'lru_lambda': lru_lambda, 'lru_w_o': lru_w_o}


def reference(x, c, positions, ada_w, ada_b, norm_g, ffn_w_in, ffn_w_out,
              rwkv_mu, rwkv_w_rkv, rwkv_w0, rwkv_w1, rwkv_w2, rwkv_a0, rwkv_a1, rwkv_a2,
              rwkv_g1, rwkv_g2, rwkv_k_k, rwkv_k_a, rwkv_r_k, rwkv_ln_w, rwkv_ln_b, rwkv_w_o,
              rwkv_v0, rwkv_v1, rwkv_v2,
              ret_w_in, ret_w_o,
              lru_w_in, lru_conv_w, lru_conv_b, lru_gate_w, lru_gate_b, lru_lambda, lru_w_o):
    res = x.astype(jnp.float32)
    cond = jax.nn.silu(c.astype(jnp.float32))
    v_first = None
    for i in range(DEPTH):
        mod = (cond @ ada_w[i] + ada_b[i]).reshape(cond.shape[0], 3 * N_SUB, -1)
        y = swiglu(sublayer_in(res, norm_g[i, 0], mod[:, 0], mod[:, 1]), ffn_w_in[i, 0], ffn_w_out[i, 0])
        res = sublayer_out(res, y, norm_g[i, 1], mod[:, 2], 0.5)
        hin = sublayer_in(res, norm_g[i, 2], mod[:, 3], mod[:, 4])
        kind, j = i % N_MIXERS, i // N_MIXERS
        if kind == 0:
            vres = None if j == 0 else (rwkv_v0[j - 1], rwkv_v1[j - 1], rwkv_v2[j - 1])
            y, v_first = rwkv7_mix(hin, v_first, rwkv_mu[j], rwkv_w_rkv[j], rwkv_w0[j], rwkv_w1[j], rwkv_w2[j],
                                   rwkv_a0[j], rwkv_a1[j], rwkv_a2[j], rwkv_g1[j], rwkv_g2[j],
                                   rwkv_k_k[j], rwkv_k_a[j], rwkv_r_k[j], rwkv_ln_w[j], rwkv_ln_b[j],
                                   rwkv_w_o[j], vres)
        elif kind == 1:
            y = retention_mix(hin, positions, ret_w_in[j], ret_w_o[j])
        else:
            y = rglru_mix(hin, lru_w_in[j], lru_conv_w[j], lru_conv_b[j], lru_gate_w[j], lru_gate_b[j],
                          lru_lambda[j], lru_w_o[j])
        res = sublayer_out(res, y, norm_g[i, 3], mod[:, 5], 1.0)
        y = swiglu(sublayer_in(res, norm_g[i, 4], mod[:, 6], mod[:, 7]), ffn_w_in[i, 1], ffn_w_out[i, 1])
        res = sublayer_out(res, y, norm_g[i, 5], mod[:, 8], 0.5)
    return res.astype(x.dtype)
```

```python
import functools
import math

import jax
import jax.numpy as jnp
from jax import lax
from jax.experimental import pallas as pl
from jax.experimental.pallas import tpu as pltpu

F32 = jnp.float32
BF16 = jnp.bfloat16

NORM_EPS = 1e-6
N_SUB = 3

RWKV_N = 64
RWKV_GROUP = 4
RWKV_GW = RWKV_GROUP * RWKV_N
RWKV_CHUNK = 64
RWKV_GN_EPS = 64e-5
LORA_PAD = 128

RET_H = 4
RET_DK = 256
RET_DV = 512
RET_CHUNK = 128
ROPE_BASE = 10000.0
HEAD_NORM_EPS = 1e-5

LRU_H = 5
LRU_BW = 256
CONV_W = 4
LRU_C = 8.0
SUBLANES = 8

VMEM_LIMIT = 56 * 1024 * 1024

NT_DIMS = (((1,), (1,)), ((), ()))
TN_DIMS = (((0,), (0,)), ((), ()))


def _mm(a, b):
    return jnp.dot(a.astype(BF16), b.astype(BF16), preferred_element_type=F32)


def _mm_nt(a, b):
    return lax.dot_general(a.astype(BF16), b.astype(BF16), NT_DIMS, preferred_element_type=F32)


def _mm_tn(a, b):
    return lax.dot_general(a.astype(BF16), b.astype(BF16), TN_DIMS, preferred_element_type=F32)


def _mm_split(x, w):
    hi = x.astype(BF16)
    lo = (x - hi.astype(F32)).astype(BF16)
    return (jnp.dot(hi, w, preferred_element_type=F32) + jnp.dot(lo, w, preferred_element_type=F32))


def _mm_split_lhs_exact(w, x):
    hi = x.astype(BF16)
    lo = (x - hi.astype(F32)).astype(BF16)
    return (jnp.dot(w, hi, preferred_element_type=F32) + jnp.dot(w, lo, preferred_element_type=F32))


def _rms(x, g):
    return x * lax.rsqrt(jnp.mean(x * x, axis=-1, keepdims=True) + NORM_EPS) * g


def _sublayer_in(res, mod, ng, s):
    return _rms(res, ng[2 * s:2 * s + 1]) * (1.0 + mod[3 * s + 1:3 * s + 2]) + mod[3 * s:3 * s + 1]


def _sublayer_out(res, y, mod, ng, s, weight):
    return res + weight * mod[3 * s + 2:3 * s + 3] * _rms(y, ng[2 * s + 1:2 * s + 2])


def _sigmoid(x):
    return jax.nn.sigmoid(x)


def _resident(shape):
    nd = len(shape)
    return pl.BlockSpec(shape, lambda *_: (0,) * nd, pipeline_mode=pl.Buffered(1))


def _params(n_axes):
    return pltpu.CompilerParams(dimension_semantics=("arbitrary",) * n_axes, vmem_limit_bytes=VMEM_LIMIT)


def _tile(t_len, want):
    tm = min(want, t_len)
    assert t_len % tm == 0, (t_len, tm)
    return tm


def _ada_kernel(c_ref, w_ref, b_ref, o_ref):
    c = c_ref[...]
    cond = c * _sigmoid(c)
    o_ref[0] = _mm(cond, w_ref[0]) + b_ref[0]


def _ada_mod(c, ada_w, ada_b):
    depth, d, n = ada_w.shape
    bsz = c.shape[0]
    tn = 1536
    assert n % tn == 0
    out = pl.pallas_call(
        _ada_kernel,
        grid=(depth, n // tn),
        in_specs=[pl.BlockSpec((bsz, d), lambda l, j: (0, 0)),
                  pl.BlockSpec((1, d, tn), lambda l, j: (l, 0, j)),
                  pl.BlockSpec((1, 1, tn), lambda l, j: (l, 0, j))],
        out_specs=pl.BlockSpec((1, bsz, tn), lambda l, j: (l, 0, j)),
        out_shape=jax.ShapeDtypeStruct((depth, bsz, n), F32),
        compiler_params=_params(2),
        name="ada_mod",
    )(c, ada_w, ada_b.reshape(depth, 1, n))
    return out.reshape(depth, bsz, 3 * N_SUB, d)


def _ffn_kernel(s, n_chunks, res_ref, mod_ref, ng_ref, wa_ref, wb_ref, wo_ref, out_ref):
    res = res_ref[0]
    mod = mod_ref[0]
    ng = ng_ref[...]
    h = _sublayer_in(res, mod, ng, s).astype(BF16)
    f = wa_ref.shape[1]
    tf = f // n_chunks
    y = jnp.zeros(res.shape, F32)
    for c in range(n_chunks):
        a = jnp.dot(h, wa_ref[:, c * tf:(c + 1) * tf], preferred_element_type=F32)
        b = jnp.dot(h, wb_ref[:, c * tf:(c + 1) * tf], preferred_element_type=F32)
        z = (a * _sigmoid(a) * b).astype(BF16)
        y = y + jnp.dot(z, wo_ref[c * tf:(c + 1) * tf, :], preferred_element_type=F32)
    out_ref[0] = _sublayer_out(res, y, mod, ng, s, 0.5)


def _ffn_sublayer(res, mod, ng, w_a, w_b, w_o, s):
    bsz, t_len, d = res.shape
    f = w_a.shape[1]
    tm = _tile(t_len, 512)
    return pl.pallas_call(
        functools.partial(_ffn_kernel, s, 2),
        grid=(bsz, t_len // tm),
        in_specs=[pl.BlockSpec((1, tm, d), lambda b, t: (b, t, 0)),
                  pl.BlockSpec((1, 3 * N_SUB, d), lambda b, t: (b, 0, 0)),
                  _resident(ng.shape), _resident((d, f)), _resident((d, f)), _resident((f, d))],
        out_specs=pl.BlockSpec((1, tm, d), lambda b, t: (b, t, 0)),
        out_shape=jax.ShapeDtypeStruct(res.shape, F32),
        compiler_params=_params(2),
        name="ffn_sublayer",
    )(res, mod, ng, w_a, w_b, w_o)


def _out_kernel(s, z_ref, res_ref, mod_ref, ng_ref, wo_ref, out_ref):
    y = jnp.dot(z_ref[0], wo_ref[...], preferred_element_type=F32)
    out_ref[0] = _sublayer_out(res_ref[0], y, mod_ref[0], ng_ref[...], s, 1.0)


def _out_sublayer(z, res, mod, ng, w_o, s):
    bsz, t_len, d = res.shape
    k = z.shape[-1]
    tm = _tile(t_len, 512)
    return pl.pallas_call(
        functools.partial(_out_kernel, s),
        grid=(bsz, t_len // tm),
        in_specs=[pl.BlockSpec((1, tm, k), lambda b, t: (b, t, 0)),
                  pl.BlockSpec((1, tm, d), lambda b, t: (b, t, 0)),
                  pl.BlockSpec((1, 3 * N_SUB, d), lambda b, t: (b, 0, 0)),
                  _resident(ng.shape), _resident((k, d))],
        out_specs=pl.BlockSpec((1, tm, d), lambda b, t: (b, t, 0)),
        out_shape=jax.ShapeDtypeStruct(res.shape, F32),
        compiler_params=_params(2),
        name="out_sublayer",
    )(z, res, mod, ng, w_o)


def _rwkv_pre_kernel(has_vres, *refs):
    if has_vres:
        (res_ref, mod_ref, ng_ref, mu_ref, vec_ref, wr_ref, wk_ref, wv_ref, w1_ref, w2_ref, a1_ref, a2_ref,
         g1_ref, g2_ref, v1_ref, v2_ref, v0_ref, vfirst_ref,
         r_ref, lw_ref, k_ref, v_ref, kk_ref, a_ref, g_ref, hext) = refs
    else:
        (res_ref, mod_ref, ng_ref, mu_ref, vec_ref, wr_ref, wk_ref, wv_ref, w1_ref, w2_ref, a1_ref, a2_ref,
         g1_ref, g2_ref,
         r_ref, lw_ref, k_ref, v_ref, kk_ref, a_ref, g_ref, hext) = refs
    tm = res_ref.shape[1]

    @pl.when(pl.program_id(1) == 0)
    def _():
        hext[0:SUBLANES, :] = jnp.zeros((SUBLANES, hext.shape[1]), F32)

    h = _sublayer_in(res_ref[0], mod_ref[0], ng_ref[...], 1)
    hext[SUBLANES:SUBLANES + tm, :] = h
    h_prev = hext[SUBLANES - 1:SUBLANES - 1 + tm, :]
    hext[0:SUBLANES, :] = hext[tm:tm + SUBLANES, :]
    dx = h_prev - h
    mu = mu_ref[...]
    xr, xw, xk, xv, xa, xg = ((h + dx * mu[j:j + 1]).astype(BF16) for j in range(6))
    vec = vec_ref[...]
    w0, a0, k_k, k_a = vec[0:1], vec[1:2], vec[2:3], vec[3:4]

    r = jnp.dot(xr, wr_ref[...], preferred_element_type=F32)
    k = jnp.dot(xk, wk_ref[...], preferred_element_type=F32)
    v = jnp.dot(xv, wv_ref[...], preferred_element_type=F32)
    zw = w0 + _mm(jnp.tanh(jnp.dot(xw, w1_ref[...], preferred_element_type=F32)), w2_ref[...])
    lw_ref[0] = -math.exp(-0.5) * _sigmoid(zw)
    a = _sigmoid(a0 + _mm(jnp.dot(xa, a1_ref[...], preferred_element_type=F32), a2_ref[...]))
    g_ref[0] = _mm(_sigmoid(jnp.dot(xg, g1_ref[...], preferred_element_type=F32)), g2_ref[...])
    if has_vres:
        mix = _sigmoid(v0_ref[...] + _mm(jnp.dot(xv, v1_ref[...], preferred_element_type=F32), v2_ref[...]))
        v = v + (vfirst_ref[0] - v) * mix
    r_ref[0] = r
    v_ref[0] = v
    a_ref[0] = a
    kk_ref[0] = k * k_k
    k_ref[0] = k * (1.0 + (a - 1.0) * k_a)


def _pad_cols(w, n):
    return jnp.pad(w, ((0, 0), (0, n - w.shape[1])))


def _pad_rows(w, n):
    return jnp.pad(w, ((0, n - w.shape[0]), (0, 0)))


def _rwkv_pre(res, mod, ng, p, v_first):
    bsz, t_len, d = res.shape
    tm = _tile(t_len, 256)
    has_vres = v_first is not None
    tok = pl.BlockSpec((1, tm, d), lambda b, t: (b, t, 0))
    lora_in = lambda w: _pad_cols(w, LORA_PAD).astype(BF16)
    lora_out = lambda w: _pad_rows(w, LORA_PAD).astype(BF16)
    vec = jnp.stack([p["w0"], p["a0"], p["k_k"], p["k_a"]])
    args = [res, mod, ng, p["mu"], vec,
            p["w_rkv"][0].astype(BF16), p["w_rkv"][1].astype(BF16), p["w_rkv"][2].astype(BF16),
            lora_in(p["w1"]), lora_out(p["w2"]), lora_in(p["a1"]), lora_out(p["a2"]),
            lora_in(p["g1"]), lora_out(p["g2"])]
    in_specs = [tok, pl.BlockSpec((1, 3 * N_SUB, d), lambda b, t: (b, 0, 0))]
    in_specs += [_resident(a.shape) for a in args[2:]]
    if has_vres:
        extra = [lora_in(p["v1"]), lora_out(p["v2"]), p["v0"].reshape(1, d)]
        args += extra + [v_first]
        in_specs += [_resident(a.shape) for a in extra] + [tok]
    outs = pl.pallas_call(
        functools.partial(_rwkv_pre_kernel, has_vres),
        grid=(bsz, t_len // tm),
        in_specs=in_specs,
        out_specs=[tok] * 7,
        out_shape=[jax.ShapeDtypeStruct(res.shape, F32)] * 7,
        scratch_shapes=[pltpu.VMEM((tm + SUBLANES, d), F32)],
        compiler_params=_params(2),
        name="rwkv_pre",
    )(*args)
    return outs


def _wkv_chunk(r, lw, k, v, kk, a, z_state, masks):
    bd_ones, tril64, bd_mask, strict, incl, eye = masks
    n2 = _mm_split(kk * kk, bd_ones)
    kkn = kk / jnp.maximum(jnp.sqrt(n2), 1e-12)
    av = -kkn
    bv = kkn * a
    cum = _mm_split_lhs_exact(tril64, lw)
    g_t = jnp.exp(cum)
    rt = r * g_t
    at = av * jnp.exp(cum - lw)
    g_inv = jnp.exp(-cum)
    bt = bv * g_inv
    kt = k * g_inv
    g_last = g_t[RWKV_CHUNK - 1:RWKV_CHUNK, :]

    def stack(x):
        return jnp.where(bd_mask, jnp.concatenate([x] * RWKV_GROUP, axis=0), 0.0).astype(BF16)

    sa, sr, sb, sk, sv = stack(at), stack(rt), stack(bt), stack(kt), stack(v)
    l_ab = jnp.where(strict, _mm_nt(sa, sb), 0.0)
    l_ak = jnp.where(strict, _mm_nt(sa, sk), 0.0)
    l_rb = jnp.where(incl, _mm_nt(sr, sb), 0.0)
    l_rk = jnp.where(incl, _mm_nt(sr, sk), 0.0)
    pw = l_ab
    inv = eye + l_ab
    for _ in range(5):
        pw = _mm(pw, pw)
        inv = inv + _mm(inv, pw)
    zb = z_state.astype(BF16)
    u = _mm(inv, _mm_nt(sa, zb) + _mm(l_ak, sv))
    y_st = _mm_nt(sr, zb) + _mm(l_rb, u) + _mm(l_rk, sv)
    z_new = (z_state + _mm_tn(u, sb) + _mm_tn(sv, sk)) * g_last
    c = RWKV_CHUNK
    y = y_st[0:c] + y_st[c:2 * c] + y_st[2 * c:3 * c] + y_st[3 * c:4 * c]
    return y, z_new


def _wkv_masks():
    gw = RWKV_GW
    row = lax.broadcasted_iota(jnp.int32, (gw, gw), 0)
    col = lax.broadcasted_iota(jnp.int32, (gw, gw), 1)
    same_head = (row // RWKV_N) == (col // RWKV_N)
    bd_ones = jnp.where(same_head, 1.0, 0.0).astype(BF16)
    t_row = row % RWKV_CHUNK
    t_col = col % RWKV_CHUNK
    strict = t_row > t_col
    incl = t_row >= t_col
    eye = jnp.where(row == col, 1.0, 0.0).astype(F32)
    r64 = lax.broadcasted_iota(jnp.int32, (RWKV_CHUNK, RWKV_CHUNK), 0)
    c64 = lax.broadcasted_iota(jnp.int32, (RWKV_CHUNK, RWKV_CHUNK), 1)
    tril64 = jnp.where(r64 >= c64, 1.0, 0.0).astype(BF16)
    return bd_ones, tril64, same_head, strict, incl, eye


def _wkv_kernel(n_chunks, r_ref, lw_ref, k_ref, v_ref, kk_ref, a_ref, g_ref, vec_ref, z_ref, state):
    @pl.when(pl.program_id(2) == 0)
    def _():
        state[...] = jnp.zeros(state.shape, F32)

    masks = _wkv_masks()
    bd_ones = masks[0]
    vec = vec_ref[0]
    r_k, ln_w, ln_b = vec[0:1], vec[1:2], vec[2:3]
    c = RWKV_CHUNK
    for i in range(n_chunks):
        sl = slice(i * c, (i + 1) * c)
        r, k, v = r_ref[0, sl, :], k_ref[0, sl, :], v_ref[0, sl, :]
        y, z_new = _wkv_chunk(r, lw_ref[0, sl, :], k, v, kk_ref[0, sl, :], a_ref[0, sl, :], state[...], masks)
        state[...] = z_new
        mean = _mm_split(y, bd_ones) * (1.0 / RWKV_N)
        yc = y - mean
        var = _mm_split(yc * yc, bd_ones) * (1.0 / RWKV_N)
        yn = yc * lax.rsqrt(var + RWKV_GN_EPS) * ln_w + ln_b
        bonus = _mm_split(r * k * r_k, bd_ones) * v
        z_ref[0, sl, :] = ((yn + bonus) * g_ref[0, sl, :]).astype(BF16)


def _wkv(r, lw, k, v, kk, a, g, p):
    bsz, t_len, d = r.shape
    n_groups = d // RWKV_GW
    tb = _tile(t_len, 256)
    vec = jnp.stack([p["r_k"].reshape(d), p["ln_w"], p["ln_b"]]).reshape(3, n_groups, RWKV_GW).transpose(1, 0, 2)
    tok = pl.BlockSpec((1, tb, RWKV_GW), lambda b, gi, t: (b, t, gi))
    return pl.pallas_call(
        functools.partial(_wkv_kernel, tb // RWKV_CHUNK),
        grid=(bsz, n_groups, t_len // tb),
        in_specs=[tok] * 7 + [pl.BlockSpec((1, 3, RWKV_GW), lambda b, gi, t: (gi, 0, 0))],
        out_specs=tok,
        out_shape=jax.ShapeDtypeStruct(r.shape, BF16),
        scratch_shapes=[pltpu.VMEM((RWKV_GW, RWKV_GW), F32)],
        compiler_params=_params(3),
        name="wkv7",
    )(r, lw, k, v, kk, a, g, vec)


def _ret_pre_kernel(res_ref, mod_ref, ng_ref, pos_ref, inv_ref, w_ref, q_ref, k_ref, v_ref, g_ref):
    h = _sublayer_in(res_ref[0], mod_ref[0], ng_ref[...], 1).astype(BF16)
    ang = pos_ref[0] * inv_ref[...]
    cos = jnp.cos(ang)
    sin = jnp.sin(ang)
    half = RET_DK // 2
    qk = RET_H * RET_DK
    for which, out_ref, scale in ((0, q_ref, 1.0), (1, k_ref, RET_DK ** -0.5)):
        for hd in range(RET_H):
            lo = which * qk + hd * RET_DK
            x = jnp.dot(h, w_ref[:, lo:lo + RET_DK], preferred_element_type=F32)
            x1, x2 = x[:, :half], x[:, half:]
            out_ref[0, :, hd * RET_DK:hd * RET_DK + half] = ((x1 * cos - x2 * sin) * scale).astype(BF16)
            out_ref[0, :, hd * RET_DK + half:(hd + 1) * RET_DK] = ((x1 * sin + x2 * cos) * scale).astype(BF16)
    nv = RET_H * RET_DV
    for hd in range(RET_H):
        lo = 2 * qk + hd * RET_DV
        v_ref[0, :, hd * RET_DV:(hd + 1) * RET_DV] = jnp.dot(
            h, w_ref[:, lo:lo + RET_DV], preferred_element_type=F32).astype(BF16)
        g_ref[0, :, hd * RET_DV:(hd + 1) * RET_DV] = jnp.dot(
            h, w_ref[:, lo + nv:lo + nv + RET_DV], preferred_element_type=F32)


def _ret_pre(res, mod, ng, pos_f, w_in):
    bsz, t_len, d = res.shape
    tm = _tile(t_len, 512)
    qk = RET_H * RET_DK
    nv = RET_H * RET_DV
    half = RET_DK // 2
    inv = (1.0 / (ROPE_BASE ** jnp.linspace(0.0, 1.0, half, dtype=F32))).reshape(1, half)
    tok = lambda n: pl.BlockSpec((1, tm, n), lambda b, t: (b, t, 0))
    return pl.pallas_call(
        _ret_pre_kernel,
        grid=(bsz, t_len // tm),
        in_specs=[tok(d), pl.BlockSpec((1, 3 * N_SUB, d), lambda b, t: (b, 0, 0)), _resident(ng.shape),
                  tok(1), _resident((1, half)), _resident(w_in.shape)],
        out_specs=[tok(qk), tok(qk), tok(nv), tok(nv)],
        out_shape=[jax.ShapeDtypeStruct((bsz, t_len, qk), BF16), jax.ShapeDtypeStruct((bsz, t_len, qk), BF16),
                   jax.ShapeDtypeStruct((bsz, t_len, nv), BF16), jax.ShapeDtypeStruct((bsz, t_len, nv), F32)],
        compiler_params=_params(2),
        name="ret_pre",
    )(res, mod, ng, pos_f, inv, w_in)


def _ret_mix_kernel(n_chunks, q_ref, k_ref, v_ref, g_ref, res_ref, mod_ref, ng_ref, wo_ref, out_ref, state, z_buf):
    @pl.when(pl.program_id(1) == 0)
    def _():
        state[...] = jnp.zeros(state.shape, F32)

    c = RET_CHUNK
    row = lax.broadcasted_iota(jnp.int32, (c, c), 0)
    col = lax.broadcasted_iota(jnp.int32, (c, c), 1)
    diff = (row - col).astype(F32)
    idx = lax.broadcasted_iota(jnp.int32, (c, 1), 0).astype(F32)
    for hd in range(RET_H):
        log_gamma = math.log(1.0 - 2.0 ** (-5.0 - hd))
        inner = jnp.where(diff >= 0, jnp.exp(log_gamma * jnp.maximum(diff, 0.0)), 0.0)
        q_decay = jnp.exp(log_gamma * (idx + 1.0))
        k_decay = jnp.exp(log_gamma * (c - 1.0 - idx))
        chunk_decay = math.exp(log_gamma * c)
        for i in range(n_chunks):
            rows = slice(i * c, (i + 1) * c)
            q_c = q_ref[0, rows, hd * RET_DK:(hd + 1) * RET_DK]
            k_c = k_ref[0, rows, hd * RET_DK:(hd + 1) * RET_DK]
            v_c = v_ref[0, rows, hd * RET_DV:(hd + 1) * RET_DV]
            r_state = state[hd]
            s = _mm_nt(q_c, k_c) * inner
            o = _mm(s, v_c) + _mm(q_c, r_state) * q_decay
            state[hd] = r_state * chunk_decay + _mm_tn(k_c.astype(F32) * k_decay, v_c)
            oc = o - jnp.mean(o, axis=-1, keepdims=True)
            on = oc * lax.rsqrt(jnp.mean(oc * oc, axis=-1, keepdims=True) + HEAD_NORM_EPS)
            gt = g_ref[0, rows, hd * RET_DV:(hd + 1) * RET_DV]
            z_buf[rows, hd * RET_DV:(hd + 1) * RET_DV] = (gt * _sigmoid(gt) * on).astype(BF16)
    y = jnp.dot(z_buf[...], wo_ref[...], preferred_element_type=F32)
    out_ref[0] = _sublayer_out(res_ref[0], y, mod_ref[0], ng_ref[...], 1, 1.0)


def _ret_mix(q, k, v, g, res, mod, ng, w_o):
    bsz, t_len, d = res.shape
    tm = _tile(t_len, 512)
    qk = RET_H * RET_DK
    nv = RET_H * RET_DV
    tok = lambda n: pl.BlockSpec((1, tm, n), lambda b, t: (b, t, 0))
    return pl.pallas_call(
        functools.partial(_ret_mix_kernel, tm // RET_CHUNK),
        grid=(bsz, t_len // tm),
        in_specs=[tok(qk), tok(qk), tok(nv), tok(nv), tok(d),
                  pl.BlockSpec((1, 3 * N_SUB, d), lambda b, t: (b, 0, 0)), _resident(ng.shape), _resident(w_o.shape)],
        out_specs=tok(d),
        out_shape=jax.ShapeDtypeStruct(res.shape, F32),
        scratch_shapes=[pltpu.VMEM((RET_H, RET_DK, RET_DV), F32), pltpu.VMEM((tm, nv), BF16)],
        compiler_params=_params(2),
        name="ret_mix",
    )(q, k, v, g, res, mod, ng, w_o)


def _lru_kernel(res_ref, mod_ref, ng_ref, win_ref, cw_ref, vec_ref, gw_ref, wo_ref, out_ref,
                xext, a_buf, u_buf, h_carry):
    tm = res_ref.shape[1]
    width = a_buf.shape[1]

    @pl.when(pl.program_id(1) == 0)
    def _():
        xext[0:SUBLANES, :] = jnp.zeros((SUBLANES, width), F32)
        h_carry[...] = jnp.zeros(h_carry.shape, F32)

    res = res_ref[0]
    mod = mod_ref[0]
    ng = ng_ref[...]
    h = _sublayer_in(res, mod, ng, 1).astype(BF16)
    gate_branch = jnp.dot(h, win_ref[:, :width], preferred_element_type=F32)
    xb = jnp.dot(h, win_ref[:, width:], preferred_element_type=F32)
    xext[SUBLANES:SUBLANES + tm, :] = xb
    cw = cw_ref[...]
    vec = vec_ref[...]
    conv_b, gate_bi, gate_br, lam = vec[0:1], vec[1:2], vec[2:3], vec[3:4]
    xc = conv_b + cw[CONV_W - 1:CONV_W] * xb
    for j in range(CONV_W - 1):
        lo = SUBLANES - (CONV_W - 1) + j
        xc = xc + cw[j:j + 1] * xext[lo:lo + tm, :]
    xext[0:SUBLANES, :] = xext[tm:tm + SUBLANES, :]

    xcb = xc.astype(BF16)
    gates = []
    for gi in range(2):
        parts = [jnp.dot(xcb[:, hd * LRU_BW:(hd + 1) * LRU_BW], gw_ref[gi * LRU_H + hd], preferred_element_type=F32)
                 for hd in range(LRU_H)]
        gates.append(jnp.concatenate(parts, axis=1))
    i_gate = _sigmoid(gates[0] + gate_bi)
    r_gate = _sigmoid(gates[1] + gate_br)
    neg_lam = -lam
    softplus = jnp.maximum(neg_lam, 0.0) + jnp.log1p(jnp.exp(-jnp.abs(neg_lam)))
    log_a = -LRU_C * r_gate * softplus
    a = jnp.exp(log_a)
    a_buf[...] = a
    u_buf[...] = jnp.sqrt(-jnp.tanh(log_a) * (a * a + 1.0)) * (i_gate * xc)

    row = lax.broadcasted_iota(jnp.int32, (SUBLANES, width), 0)

    def group(i, carry):
        rows = pl.ds(pl.multiple_of(i * SUBLANES, SUBLANES), SUBLANES)
        ag = a_buf[rows, :]
        ug = u_buf[rows, :]
        for d in (1, 2, 4):
            keep = row >= d
            u_prev = jnp.where(keep, pltpu.roll(ug, d, 0), 0.0)
            a_prev = jnp.where(keep, pltpu.roll(ag, d, 0), 1.0)
            ug = ug + ag * u_prev
            ag = ag * a_prev
        hg = ug + ag * carry
        u_buf[rows, :] = hg
        return jnp.broadcast_to(hg[SUBLANES - 1:SUBLANES, :], (SUBLANES, width))

    h_carry[...] = lax.fori_loop(0, tm // SUBLANES, group, h_carry[...])

    gb = gate_branch
    gelu = 0.5 * gb * (1.0 + jnp.tanh(math.sqrt(2.0 / math.pi) * (gb + 0.044715 * (gb * gb * gb))))
    z = (gelu * u_buf[...]).astype(BF16)
    y = jnp.dot(z, wo_ref[...], preferred_element_type=F32)
    out_ref[0] = _sublayer_out(res, y, mod, ng, 1, 1.0)


def _lru_sublayer(res, mod, ng, w_in, conv_w, vec, gate_w, w_o):
    bsz, t_len, d = res.shape
    width = w_o.shape[0]
    tm = _tile(t_len, 256)
    tok = pl.BlockSpec((1, tm, d), lambda b, t: (b, t, 0))
    return pl.pallas_call(
        _lru_kernel,
        grid=(bsz, t_len // tm),
        in_specs=[tok, pl.BlockSpec((1, 3 * N_SUB, d), lambda b, t: (b, 0, 0)), _resident(ng.shape),
                  _resident(w_in.shape), _resident(conv_w.shape), _resident(vec.shape),
                  _resident(gate_w.shape), _resident(w_o.shape)],
        out_specs=tok,
        out_shape=jax.ShapeDtypeStruct(res.shape, F32),
        scratch_shapes=[pltpu.VMEM((tm + SUBLANES, width), F32), pltpu.VMEM((tm, width), F32),
                        pltpu.VMEM((tm, width), F32), pltpu.VMEM((SUBLANES, width), F32)],
        compiler_params=_params(2),
        name="lru_sublayer",
    )(res, mod, ng, w_in, conv_w, vec, gate_w, w_o)


def kernel(x, c, positions, ada_w, ada_b, norm_g, ffn_w_in, ffn_w_out, rwkv_mu, rwkv_w_rkv, rwkv_w0, rwkv_w1, rwkv_w2, rwkv_a0, rwkv_a1, rwkv_a2, rwkv_g1, rwkv_g2, rwkv_k_k, rwkv_k_a, rwkv_r_k, rwkv_ln_w, rwkv_ln_b, rwkv_w_o, rwkv_v0, rwkv_v1, rwkv_v2, ret_w_in, ret_w_o, lru_w_in, lru_conv_w, lru_conv_b, lru_gate_w, lru_gate_b, lru_lambda, lru_w_o):
    depth = ada_w.shape[0]
    d_ff = ffn_w_out.shape[2]
    res = x.astype(F32)
    mod = _ada_mod(c.astype(F32), ada_w, ada_b)
    pos_f = positions.astype(F32)[..., None]
    v_first = None
    for i in range(depth):
        ng = norm_g[i]
        ffn = lambda res, m, s: _ffn_sublayer(
            res, mod[i], ng, ffn_w_in[i, m, :, :d_ff].astype(BF16), ffn_w_in[i, m, :, d_ff:].astype(BF16),
            ffn_w_out[i, m].astype(BF16), s)
        res = ffn(res, 0, 0)
        kind, j = i % 3, i // 3
        if kind == 0:
            p = dict(mu=rwkv_mu[j], w_rkv=rwkv_w_rkv[j], w0=rwkv_w0[j], w1=rwkv_w1[j], w2=rwkv_w2[j],
                     a0=rwkv_a0[j], a1=rwkv_a1[j], a2=rwkv_a2[j], g1=rwkv_g1[j], g2=rwkv_g2[j],
                     k_k=rwkv_k_k[j], k_a=rwkv_k_a[j], r_k=rwkv_r_k[j], ln_w=rwkv_ln_w[j], ln_b=rwkv_ln_b[j])
            if j > 0:
                p.update(v0=rwkv_v0[j - 1], v1=rwkv_v1[j - 1], v2=rwkv_v2[j - 1])
            r, lw, k, v, kk, a, g = _rwkv_pre(res, mod[i], ng, p, v_first if j > 0 else None)
            if j == 0:
                v_first = v
            z = _wkv(r, lw, k, v, kk, a, g, p)
            res = _out_sublayer(z, res, mod[i], ng, rwkv_w_o[j].astype(BF16), 1)
        elif kind == 1:
            q, k, v, g = _ret_pre(res, mod[i], ng, pos_f, ret_w_in[j].astype(BF16))
            res = _ret_mix(q, k, v, g, res, mod[i], ng, ret_w_o[j].astype(BF16))
        else:
            width = lru_w_o.shape[1]
            vec = jnp.stack([lru_conv_b[j], lru_gate_b[j, 0].reshape(width), lru_gate_b[j, 1].reshape(width),
                             lru_lambda[j]])
            gate_w = lru_gate_w[j].reshape(2 * LRU_H, LRU_BW, LRU_BW).astype(BF16)
            res = _lru_sublayer(res, mod[i], ng, lru_w_in[j].astype(BF16), lru_conv_w[j], vec, gate_w,
                                lru_w_o[j].astype(BF16))
        res = ffn(res, 1, 2)
    return res.astype(x.dtype)
```

```python
import functools
import math

import jax
import jax.numpy as jnp
from jax import lax
from jax.experimental import pallas as pl
from jax.experimental.pallas import tpu as pltpu

F32 = jnp.float32
BF16 = jnp.bfloat16

NORM_EPS = 1e-6
N_SUB = 3

RWKV_N = 64
RWKV_GROUP = 4
RWKV_GW = RWKV_GROUP * RWKV_N
RWKV_CHUNK = 64
RWKV_GN_EPS = 64e-5
LORA_PAD = 128

RET_H = 4
RET_DK = 256
RET_DV = 512
RET_CHUNK = 128
ROPE_BASE = 10000.0
HEAD_NORM_EPS = 1e-5

LRU_H = 5
LRU_BW = 256
CONV_W = 4
LRU_C = 8.0
SUBLANES = 8

VMEM_LIMIT = 56 * 1024 * 1024

NT_DIMS = (((1,), (1,)), ((), ()))
TN_DIMS = (((0,), (0,)), ((), ()))


def _mm(a, b):
    return jnp.dot(a.astype(BF16), b.astype(BF16), preferred_element_type=F32)


def _mm_nt(a, b):
    return lax.dot_general(a.astype(BF16), b.astype(BF16), NT_DIMS, preferred_element_type=F32)


def _mm_tn(a, b):
    return lax.dot_general(a.astype(BF16), b.astype(BF16), TN_DIMS, preferred_element_type=F32)


def _mm_split_lhs_exact(w, x):
    hi = x.astype(BF16)
    lo = (x - hi.astype(F32)).astype(BF16)
    return (jnp.dot(w, hi, preferred_element_type=F32) + jnp.dot(w, lo, preferred_element_type=F32))


def _rms(x, g):
    return x * lax.rsqrt(jnp.mean(x * x, axis=-1, keepdims=True) + NORM_EPS) * g


def _sublayer_in(res, mod, ng, s):
    return _rms(res, ng[2 * s:2 * s + 1]) * (1.0 + mod[3 * s + 1:3 * s + 2]) + mod[3 * s:3 * s + 1]


def _sublayer_out(res, y, mod, ng, s, weight):
    return res + weight * mod[3 * s + 2:3 * s + 3] * _rms(y, ng[2 * s + 1:2 * s + 2])


def _sigmoid(x):
    return jax.nn.sigmoid(x)


def _resident(shape):
    nd = len(shape)
    return pl.BlockSpec(shape, lambda *_: (0,) * nd, pipeline_mode=pl.Buffered(1))


def _params(n_axes):
    return pltpu.CompilerParams(dimension_semantics=("arbitrary",) * n_axes, vmem_limit_bytes=VMEM_LIMIT)


def _tile(t_len, want):
    tm = min(want, t_len)
    assert t_len % tm == 0, (t_len, tm)
    return tm


def _ada_kernel(c_ref, w_ref, b_ref, o_ref):
    c = c_ref[...]
    cond = c * _sigmoid(c)
    o_ref[0] = _mm(cond, w_ref[0]) + b_ref[0]


def _ada_mod(c, ada_w, ada_b):
    depth, d, n = ada_w.shape
    bsz = c.shape[0]
    tn = 1536
    assert n % tn == 0
    out = pl.pallas_call(
        _ada_kernel,
        grid=(depth, n // tn),
        in_specs=[pl.BlockSpec((bsz, d), lambda l, j: (0, 0)),
                  pl.BlockSpec((1, d, tn), lambda l, j: (l, 0, j)),
                  pl.BlockSpec((1, 1, tn), lambda l, j: (l, 0, j))],
        out_specs=pl.BlockSpec((1, bsz, tn), lambda l, j: (l, 0, j)),
        out_shape=jax.ShapeDtypeStruct((depth, bsz, n), F32),
        compiler_params=_params(2),
        name="ada_mod",
    )(c, ada_w, ada_b.reshape(depth, 1, n))
    return out.reshape(depth, bsz, 3 * N_SUB, d)


def _ffn_kernel(s, n_chunks, res_ref, mod_ref, ng_ref, wa_ref, wb_ref, wo_ref, out_ref):
    res = res_ref[0]
    mod = mod_ref[0]
    ng = ng_ref[...]
    h = _sublayer_in(res, mod, ng, s).astype(BF16)
    f = wa_ref.shape[1]
    tf = f // n_chunks
    y = jnp.zeros(res.shape, F32)
    for c in range(n_chunks):
        a = jnp.dot(h, wa_ref[:, c * tf:(c + 1) * tf], preferred_element_type=F32)
        b = jnp.dot(h, wb_ref[:, c * tf:(c + 1) * tf], preferred_element_type=F32)
        z = (a * _sigmoid(a) * b).astype(BF16)
        y = y + jnp.dot(z, wo_ref[c * tf:(c + 1) * tf, :], preferred_element_type=F32)
    out_ref[0] = _sublayer_out(res, y, mod, ng, s, 0.5)


def _ffn_sublayer(res, mod, ng, w_a, w_b, w_o, s):
    bsz, t_len, d = res.shape
    f = w_a.shape[1]
    tm = _tile(t_len, 512)
    return pl.pallas_call(
        functools.partial(_ffn_kernel, s, 2),
        grid=(bsz, t_len // tm),
        in_specs=[pl.BlockSpec((1, tm, d), lambda b, t: (b, t, 0)),
                  pl.BlockSpec((1, 3 * N_SUB, d), lambda b, t: (b, 0, 0)),
                  _resident(ng.shape), _resident((d, f)), _resident((d, f)), _resident((f, d))],
        out_specs=pl.BlockSpec((1, tm, d), lambda b, t: (b, t, 0)),
        out_shape=jax.ShapeDtypeStruct(res.shape, F32),
        compiler_params=_params(2),
        name="ffn_sublayer",
    )(res, mod, ng, w_a, w_b, w_o)


def _out_kernel(s, z_ref, res_ref, mod_ref, ng_ref, wo_ref, out_ref):
    y = jnp.dot(z_ref[0], wo_ref[...], preferred_element_type=F32)
    out_ref[0] = _sublayer_out(res_ref[0], y, mod_ref[0], ng_ref[...], s, 1.0)


def _out_sublayer(z, res, mod, ng, w_o, s):
    bsz, t_len, d = res.shape
    k = z.shape[-1]
    tm = _tile(t_len, 512)
    return pl.pallas_call(
        functools.partial(_out_kernel, s),
        grid=(bsz, t_len // tm),
        in_specs=[pl.BlockSpec((1, tm, k), lambda b, t: (b, t, 0)),
                  pl.BlockSpec((1, tm, d), lambda b, t: (b, t, 0)),
                  pl.BlockSpec((1, 3 * N_SUB, d), lambda b, t: (b, 0, 0)),
                  _resident(ng.shape), _resident((k, d))],
        out_specs=pl.BlockSpec((1, tm, d), lambda b, t: (b, t, 0)),
        out_shape=jax.ShapeDtypeStruct(res.shape, F32),
        compiler_params=_params(2),
        name="out_sublayer",
    )(z, res, mod, ng, w_o)


def _rwkv_pre_kernel(has_vres, *refs):
    if has_vres:
        (res_ref, mod_ref, ng_ref, mu_ref, vec_ref, wr_ref, wk_ref, wv_ref, w1_ref, w2_ref, a1_ref, a2_ref,
         g1_ref, g2_ref, v1_ref, v2_ref, v0_ref, vfirst_ref,
         r_ref, lw_ref, k_ref, v_ref, kk_ref, a_ref, g_ref, hext) = refs
    else:
        (res_ref, mod_ref, ng_ref, mu_ref, vec_ref, wr_ref, wk_ref, wv_ref, w1_ref, w2_ref, a1_ref, a2_ref,
         g1_ref, g2_ref,
         r_ref, lw_ref, k_ref, v_ref, kk_ref, a_ref, g_ref, hext) = refs
    tm = res_ref.shape[1]

    @pl.when(pl.program_id(1) == 0)
    def _():
        hext[0:SUBLANES, :] = jnp.zeros((SUBLANES, hext.shape[1]), F32)

    h = _sublayer_in(res_ref[0], mod_ref[0], ng_ref[...], 1)
    hext[SUBLANES:SUBLANES + tm, :] = h
    h_prev = hext[SUBLANES - 1:SUBLANES - 1 + tm, :]
    hext[0:SUBLANES, :] = hext[tm:tm + SUBLANES, :]
    dx = h_prev - h
    mu = mu_ref[...]
    xr, xw, xk, xv, xa, xg = ((h + dx * mu[j:j + 1]).astype(BF16) for j in range(6))
    vec = vec_ref[...]
    w0, a0, k_k, k_a = vec[0:1], vec[1:2], vec[2:3], vec[3:4]

    r = jnp.dot(xr, wr_ref[...], preferred_element_type=F32)
    k = jnp.dot(xk, wk_ref[...], preferred_element_type=F32)
    v = jnp.dot(xv, wv_ref[...], preferred_element_type=F32)
    zw = w0 + _mm(jnp.tanh(jnp.dot(xw, w1_ref[...], preferred_element_type=F32)), w2_ref[...])
    lw_ref[0] = -math.exp(-0.5) * _sigmoid(zw)
    a = _sigmoid(a0 + _mm(jnp.dot(xa, a1_ref[...], preferred_element_type=F32), a2_ref[...]))
    g_ref[0] = _mm(_sigmoid(jnp.dot(xg, g1_ref[...], preferred_element_type=F32)), g2_ref[...])
    if has_vres:
        mix = _sigmoid(v0_ref[...] + _mm(jnp.dot(xv, v1_ref[...], preferred_element_type=F32), v2_ref[...]))
        v = v + (vfirst_ref[0] - v) * mix
    r_ref[0] = r
    v_ref[0] = v
    a_ref[0] = a
    kk_ref[0] = k * k_k
    k_ref[0] = k * (1.0 + (a - 1.0) * k_a)


def _pad_cols(w, n):
    return jnp.pad(w, ((0, 0), (0, n - w.shape[1])))


def _pad_rows(w, n):
    return jnp.pad(w, ((0, n - w.shape[0]), (0, 0)))


def _rwkv_pre(res, mod, ng, p, v_first):
    bsz, t_len, d = res.shape
    tm = _tile(t_len, 256)
    has_vres = v_first is not None
    tok = pl.BlockSpec((1, tm, d), lambda b, t: (b, t, 0))
    lora_in = lambda w: _pad_cols(w, LORA_PAD).astype(BF16)
    lora_out = lambda w: _pad_rows(w, LORA_PAD).astype(BF16)
    vec = jnp.stack([p["w0"], p["a0"], p["k_k"], p["k_a"]])
    args = [res, mod, ng, p["mu"], vec,
            p["w_rkv"][0].astype(BF16), p["w_rkv"][1].astype(BF16), p["w_rkv"][2].astype(BF16),
            lora_in(p["w1"]), lora_out(p["w2"]), lora_in(p["a1"]), lora_out(p["a2"]),
            lora_in(p["g1"]), lora_out(p["g2"])]
    in_specs = [tok, pl.BlockSpec((1, 3 * N_SUB, d), lambda b, t: (b, 0, 0))]
    in_specs += [_resident(a.shape) for a in args[2:]]
    if has_vres:
        extra = [lora_in(p["v1"]), lora_out(p["v2"]), p["v0"].reshape(1, d)]
        args += extra + [v_first]
        in_specs += [_resident(a.shape) for a in extra] + [tok]
    outs = pl.pallas_call(
        functools.partial(_rwkv_pre_kernel, has_vres),
        grid=(bsz, t_len // tm),
        in_specs=in_specs,
        out_specs=[tok] * 7,
        out_shape=[jax.ShapeDtypeStruct(res.shape, F32)] * 7,
        scratch_shapes=[pltpu.VMEM((tm + SUBLANES, d), F32)],
        compiler_params=_params(2),
        name="rwkv_pre",
    )(*args)
    return outs


def _block_diag(x, bd_mask):
    xb = x.astype(BF16)
    return jnp.where(bd_mask, jnp.concatenate([xb] * RWKV_GROUP, axis=0), jnp.zeros((), BF16))


def _wkv_prepare(units, masks):
    bd_mask, strict, incl, eye = masks
    c = RWKV_CHUNK
    bd = lambda x: _block_diag(x, bd_mask)
    ar = [jnp.concatenate([at, rt], axis=0).astype(BF16) for rt, at, bt, kt, v in units]
    sv = [bd(v) for rt, at, bt, kt, v in units]
    lb = [_mm_nt(x, bd(u[2])) for x, u in zip(ar, units)]
    lk = [_mm_nt(x, bd(u[3])) for x, u in zip(ar, units)]
    l_ab = [jnp.where(strict, x[:c], 0.0) for x in lb]
    l_rb = [jnp.where(incl, x[c:], 0.0) for x in lb]
    l_ak = [jnp.where(strict, x[:c], 0.0) for x in lk]
    l_rk = [jnp.where(incl, x[c:], 0.0) for x in lk]
    pw = [_mm(x, bd(x)) for x in l_ab]
    inv = [eye + x for x in l_ab]
    for _ in range(4):
        both = [_mm(jnp.concatenate([p, t], axis=0), bd(p)) for p, t in zip(pw, inv)]
        pw = [x[:c] for x in both]
        inv = [t + x[c:] for t, x in zip(inv, both)]
    inv = [t + _mm(t, bd(p)) for t, p in zip(inv, pw)]
    ak_v = [_mm(x, s) for x, s in zip(l_ak, sv)]
    rk_v = [_mm(x, s) for x, s in zip(l_rk, sv)]
    bk = [jnp.concatenate([bt, kt], axis=0).astype(BF16) for rt, at, bt, kt, v in units]
    return [dict(ar=ar[i], inv=inv[i], l_rb=l_rb[i], ak_v=ak_v[i], rk_v=rk_v[i], bk=bk[i], v=units[i][4])
            for i in range(len(units))]


def _wkv_advance(prepared, z_states, g_last, masks):
    bd_mask = masks[0]
    c = RWKV_CHUNK
    bd = lambda x: _block_diag(x, bd_mask)
    ars = [_mm_nt(p["ar"], z) for p, z in zip(prepared, z_states)]
    u = [_mm(p["inv"], bd(x[:c] + p["ak_v"])) for p, x in zip(prepared, ars)]
    y = [x[c:] + _mm(p["l_rb"], bd(uu)) + p["rk_v"] for p, x, uu in zip(prepared, ars, u)]
    dz = [_mm_tn(jnp.concatenate([uu, p["v"]], axis=0), p["bk"]) for p, uu in zip(prepared, u)]
    z_new = [(z + jnp.where(bd_mask, d, 0.0)) * gl for z, d, gl in zip(z_states, dz, g_last)]
    return y, z_new


def _wkv_masks(tb):
    gw, c = RWKV_GW, RWKV_CHUNK
    row = lax.broadcasted_iota(jnp.int32, (gw, gw), 0)
    col = lax.broadcasted_iota(jnp.int32, (gw, gw), 1)
    bd_mask = (row // RWKV_N) == (col // RWKV_N)
    t = lax.broadcasted_iota(jnp.int32, (c, gw), 0)
    j = lax.broadcasted_iota(jnp.int32, (c, gw), 1) % c
    eye = jnp.where(t == j, 1.0, 0.0).astype(F32)
    rb = lax.broadcasted_iota(jnp.int32, (tb, tb), 0)
    cb = lax.broadcasted_iota(jnp.int32, (tb, tb), 1)
    tril_chunks = jnp.where((rb >= cb) & ((rb // c) == (cb // c)), 1.0, 0.0).astype(BF16)
    return (bd_mask, t > j, t >= j, eye), tril_chunks


def _wkv_kernel(n_chunks, r_ref, lw_ref, k_ref, v_ref, kk_ref, a_ref, g_ref, vec_ref, z_ref, state):
    @pl.when(pl.program_id(1) == 0)
    def _():
        state[...] = jnp.zeros(state.shape, F32)

    c = RWKV_CHUNK
    n_groups = state.shape[0]
    masks, tril_chunks = _wkv_masks(n_chunks * c)
    bd_ones = jnp.where(masks[0], 1.0, 0.0).astype(BF16)
    prep = []
    for gi in range(n_groups):
        ln = slice(gi * RWKV_GW, (gi + 1) * RWKV_GW)
        r, lw, k, v, kk, a = (ref[0, :, ln] for ref in (r_ref, lw_ref, k_ref, v_ref, kk_ref, a_ref))
        kkn = kk / jnp.maximum(jnp.sqrt(_mm(kk * kk, bd_ones)), 1e-12)
        cum = _mm_split_lhs_exact(tril_chunks, lw)
        g_t = jnp.exp(cum)
        g_inv = jnp.exp(-cum)
        prep.append((r * g_t, -kkn * jnp.exp(cum - lw), kkn * a * g_inv, k * g_inv, v, g_t))
    units = [tuple(x[i * c:(i + 1) * c] for x in prep[gi][:5]) for i in range(n_chunks) for gi in range(n_groups)]
    prepared = _wkv_prepare(units, masks)
    z_states = [state[gi] for gi in range(n_groups)]
    ys = [[] for _ in range(n_groups)]
    for i in range(n_chunks):
        g_last = [prep[gi][5][(i + 1) * c - 1:(i + 1) * c] for gi in range(n_groups)]
        y, z_states = _wkv_advance(prepared[i * n_groups:(i + 1) * n_groups], z_states, g_last, masks)
        for gi in range(n_groups):
            ys[gi].append(y[gi])
    for gi in range(n_groups):
        ln = slice(gi * RWKV_GW, (gi + 1) * RWKV_GW)
        state[gi] = z_states[gi]
        y = jnp.concatenate(ys[gi], axis=0)
        r, k, v = r_ref[0, :, ln], k_ref[0, :, ln], v_ref[0, :, ln]
        r_k, ln_w, ln_b = vec_ref[0:1, ln], vec_ref[1:2, ln], vec_ref[2:3, ln]
        yc = y - _mm(y, bd_ones) * (1.0 / RWKV_N)
        var = _mm(yc * yc, bd_ones) * (1.0 / RWKV_N)
        yn = yc * lax.rsqrt(var + RWKV_GN_EPS) * ln_w + ln_b
        bonus = _mm(r * k * r_k, bd_ones) * v
        z_ref[0, :, ln] = ((yn + bonus) * g_ref[0, :, ln]).astype(BF16)


def _wkv(r, lw, k, v, kk, a, g, p):
    bsz, t_len, d = r.shape
    n_groups = d // RWKV_GW
    tb = _tile(t_len, 128)
    vec = jnp.stack([p["r_k"].reshape(d), p["ln_w"], p["ln_b"]])
    tok = pl.BlockSpec((1, tb, d), lambda b, t: (b, t, 0))
    return pl.pallas_call(
        functools.partial(_wkv_kernel, tb // RWKV_CHUNK),
        grid=(bsz, t_len // tb),
        in_specs=[tok] * 7 + [_resident(vec.shape)],
        out_specs=tok,
        out_shape=jax.ShapeDtypeStruct(r.shape, BF16),
        scratch_shapes=[pltpu.VMEM((n_groups, RWKV_GW, RWKV_GW), F32)],
        compiler_params=_params(2),
        name="wkv7",
    )(r, lw, k, v, kk, a, g, vec)


def _ret_pre_kernel(res_ref, mod_ref, ng_ref, pos_ref, inv_ref, w_ref, q_ref, k_ref, v_ref, g_ref):
    h = _sublayer_in(res_ref[0], mod_ref[0], ng_ref[...], 1).astype(BF16)
    ang = pos_ref[0] * inv_ref[...]
    cos = jnp.cos(ang)
    sin = jnp.sin(ang)
    half = RET_DK // 2
    qk = RET_H * RET_DK
    for which, out_ref, scale in ((0, q_ref, 1.0), (1, k_ref, RET_DK ** -0.5)):
        for hd in range(RET_H):
            lo = which * qk + hd * RET_DK
            x = jnp.dot(h, w_ref[:, lo:lo + RET_DK], preferred_element_type=F32)
            x1, x2 = x[:, :half], x[:, half:]
            out_ref[0, :, hd * RET_DK:hd * RET_DK + half] = ((x1 * cos - x2 * sin) * scale).astype(BF16)
            out_ref[0, :, hd * RET_DK + half:(hd + 1) * RET_DK] = ((x1 * sin + x2 * cos) * scale).astype(BF16)
    nv = RET_H * RET_DV
    for hd in range(RET_H):
        lo = 2 * qk + hd * RET_DV
        v_ref[0, :, hd * RET_DV:(hd + 1) * RET_DV] = jnp.dot(
            h, w_ref[:, lo:lo + RET_DV], preferred_element_type=F32).astype(BF16)
        g_ref[0, :, hd * RET_DV:(hd + 1) * RET_DV] = jnp.dot(
            h, w_ref[:, lo + nv:lo + nv + RET_DV], preferred_element_type=F32)


def _ret_pre(res, mod, ng, pos_f, w_in):
    bsz, t_len, d = res.shape
    tm = _tile(t_len, 512)
    qk = RET_H * RET_DK
    nv = RET_H * RET_DV
    half = RET_DK // 2
    inv = (1.0 / (ROPE_BASE ** jnp.linspace(0.0, 1.0, half, dtype=F32))).reshape(1, half)
    tok = lambda n: pl.BlockSpec((1, tm, n), lambda b, t: (b, t, 0))
    return pl.pallas_call(
        _ret_pre_kernel,
        grid=(bsz, t_len // tm),
        in_specs=[tok(d), pl.BlockSpec((1, 3 * N_SUB, d), lambda b, t: (b, 0, 0)), _resident(ng.shape),
                  tok(1), _resident((1, half)), _resident(w_in.shape)],
        out_specs=[tok(qk), tok(qk), tok(nv), tok(nv)],
        out_shape=[jax.ShapeDtypeStruct((bsz, t_len, qk), BF16), jax.ShapeDtypeStruct((bsz, t_len, qk), BF16),
                   jax.ShapeDtypeStruct((bsz, t_len, nv), BF16), jax.ShapeDtypeStruct((bsz, t_len, nv), F32)],
        compiler_params=_params(2),
        name="ret_pre",
    )(res, mod, ng, pos_f, inv, w_in)


def _ret_mix_kernel(n_chunks, q_ref, k_ref, v_ref, g_ref, res_ref, mod_ref, ng_ref, wo_ref, out_ref, state, z_buf):
    @pl.when(pl.program_id(1) == 0)
    def _():
        state[...] = jnp.zeros(state.shape, F32)

    c = RET_CHUNK
    row = lax.broadcasted_iota(jnp.int32, (c, c), 0)
    col = lax.broadcasted_iota(jnp.int32, (c, c), 1)
    diff = (row - col).astype(F32)
    idx = lax.broadcasted_iota(jnp.int32, (c, 1), 0).astype(F32)
    for hd in range(RET_H):
        log_gamma = math.log(1.0 - 2.0 ** (-5.0 - hd))
        inner = jnp.where(diff >= 0, jnp.exp(log_gamma * jnp.maximum(diff, 0.0)), 0.0)
        q_decay = jnp.exp(log_gamma * (idx + 1.0))
        k_decay = jnp.exp(log_gamma * (c - 1.0 - idx))
        chunk_decay = math.exp(log_gamma * c)
        for i in range(n_chunks):
            rows = slice(i * c, (i + 1) * c)
            q_c = q_ref[0, rows, hd * RET_DK:(hd + 1) * RET_DK]
            k_c = k_ref[0, rows, hd * RET_DK:(hd + 1) * RET_DK]
            v_c = v_ref[0, rows, hd * RET_DV:(hd + 1) * RET_DV]
            r_state = state[hd]
            s = _mm_nt(q_c, k_c) * inner
            o = _mm(s, v_c) + _mm(q_c, r_state) * q_decay
            state[hd] = r_state * chunk_decay + _mm_tn(k_c.astype(F32) * k_decay, v_c)
            oc = o - jnp.mean(o, axis=-1, keepdims=True)
            on = oc * lax.rsqrt(jnp.mean(oc * oc, axis=-1, keepdims=True) + HEAD_NORM_EPS)
            gt = g_ref[0, rows, hd * RET_DV:(hd + 1) * RET_DV]
            z_buf[rows, hd * RET_DV:(hd + 1) * RET_DV] = (gt * _sigmoid(gt) * on).astype(BF16)
    y = jnp.dot(z_buf[...], wo_ref[...], preferred_element_type=F32)
    out_ref[0] = _sublayer_out(res_ref[0], y, mod_ref[0], ng_ref[...], 1, 1.0)


def _ret_mix(q, k, v, g, res, mod, ng, w_o):
    bsz, t_len, d = res.shape
    tm = _tile(t_len, 512)
    qk = RET_H * RET_DK
    nv = RET_H * RET_DV
    tok = lambda n: pl.BlockSpec((1, tm, n), lambda b, t: (b, t, 0))
    return pl.pallas_call(
        functools.partial(_ret_mix_kernel, tm // RET_CHUNK),
        grid=(bsz, t_len // tm),
        in_specs=[tok(qk), tok(qk), tok(nv), tok(nv), tok(d),
                  pl.BlockSpec((1, 3 * N_SUB, d), lambda b, t: (b, 0, 0)), _resident(ng.shape), _resident(w_o.shape)],
        out_specs=tok(d),
        out_shape=jax.ShapeDtypeStruct(res.shape, F32),
        scratch_shapes=[pltpu.VMEM((RET_H, RET_DK, RET_DV), F32), pltpu.VMEM((tm, nv), BF16)],
        compiler_params=_params(2),
        name="ret_mix",
    )(q, k, v, g, res, mod, ng, w_o)


def _lru_kernel(res_ref, mod_ref, ng_ref, win_ref, cw_ref, vec_ref, gw_ref, wo_ref, out_ref,
                xext, a_buf, u_buf, h_carry):
    tm = res_ref.shape[1]
    width = a_buf.shape[1]

    @pl.when(pl.program_id(1) == 0)
    def _():
        xext[0:SUBLANES, :] = jnp.zeros((SUBLANES, width), F32)
        h_carry[...] = jnp.zeros(h_carry.shape, F32)

    res = res_ref[0]
    mod = mod_ref[0]
    ng = ng_ref[...]
    h = _sublayer_in(res, mod, ng, 1).astype(BF16)
    gate_branch = jnp.dot(h, win_ref[:, :width], preferred_element_type=F32)
    xb = jnp.dot(h, win_ref[:, width:], preferred_element_type=F32)
    xext[SUBLANES:SUBLANES + tm, :] = xb
    cw = cw_ref[...]
    vec = vec_ref[...]
    conv_b, gate_bi, gate_br, lam = vec[0:1], vec[1:2], vec[2:3], vec[3:4]
    xc = conv_b + cw[CONV_W - 1:CONV_W] * xb
    for j in range(CONV_W - 1):
        lo = SUBLANES - (CONV_W - 1) + j
        xc = xc + cw[j:j + 1] * xext[lo:lo + tm, :]
    xext[0:SUBLANES, :] = xext[tm:tm + SUBLANES, :]

    xcb = xc.astype(BF16)
    gates = []
    for gi in range(2):
        parts = [jnp.dot(xcb[:, hd * LRU_BW:(hd + 1) * LRU_BW], gw_ref[gi * LRU_H + hd], preferred_element_type=F32)
                 for hd in range(LRU_H)]
        gates.append(jnp.concatenate(parts, axis=1))
    i_gate = _sigmoid(gates[0] + gate_bi)
    r_gate = _sigmoid(gates[1] + gate_br)
    neg_lam = -lam
    softplus = jnp.maximum(neg_lam, 0.0) + jnp.log1p(jnp.exp(-jnp.abs(neg_lam)))
    log_a = -LRU_C * r_gate * softplus
    a = jnp.exp(log_a)
    a_buf[...] = a
    u_buf[...] = jnp.sqrt(-jnp.tanh(log_a) * (a * a + 1.0)) * (i_gate * xc)

    row = lax.broadcasted_iota(jnp.int32, (SUBLANES, width), 0)

    def group(i, carry):
        rows = pl.ds(pl.multiple_of(i * SUBLANES, SUBLANES), SUBLANES)
        ag = a_buf[rows, :]
        ug = u_buf[rows, :]
        for d in (1, 2, 4):
            keep = row >= d
            u_prev = jnp.where(keep, pltpu.roll(ug, d, 0), 0.0)
            a_prev = jnp.where(keep, pltpu.roll(ag, d, 0), 1.0)
            ug = ug + ag * u_prev
            ag = ag * a_prev
        hg = ug + ag * carry
        u_buf[rows, :] = hg
        return jnp.broadcast_to(hg[SUBLANES - 1:SUBLANES, :], (SUBLANES, width))

    h_carry[...] = lax.fori_loop(0, tm // SUBLANES, group, h_carry[...])

    gb = gate_branch
    gelu = 0.5 * gb * (1.0 + jnp.tanh(math.sqrt(2.0 / math.pi) * (gb + 0.044715 * (gb * gb * gb))))
    z = (gelu * u_buf[...]).astype(BF16)
    y = jnp.dot(z, wo_ref[...], preferred_element_type=F32)
    out_ref[0] = _sublayer_out(res, y, mod, ng, 1, 1.0)


def _lru_sublayer(res, mod, ng, w_in, conv_w, vec, gate_w, w_o):
    bsz, t_len, d = res.shape
    width = w_o.shape[0]
    tm = _tile(t_len, 256)
    tok = pl.BlockSpec((1, tm, d), lambda b, t: (b, t, 0))
    return pl.pallas_call(
        _lru_kernel,
        grid=(bsz, t_len // tm),
        in_specs=[tok, pl.BlockSpec((1, 3 * N_SUB, d), lambda b, t: (b, 0, 0)), _resident(ng.shape),
                  _resident(w_in.shape), _resident(conv_w.shape), _resident(vec.shape),
                  _resident(gate_w.shape), _resident(w_o.shape)],
        out_specs=tok,
        out_shape=jax.ShapeDtypeStruct(res.shape, F32),
        scratch_shapes=[pltpu.VMEM((tm + SUBLANES, width), F32), pltpu.VMEM((tm, width), F32),
                        pltpu.VMEM((tm, width), F32), pltpu.VMEM((SUBLANES, width), F32)],
        compiler_params=_params(2),
        name="lru_sublayer",
    )(res, mod, ng, w_in, conv_w, vec, gate_w, w_o)


def kernel(x, c, positions, ada_w, ada_b, norm_g, ffn_w_in, ffn_w_out, rwkv_mu, rwkv_w_rkv, rwkv_w0, rwkv_w1, rwkv_w2, rwkv_a0, rwkv_a1, rwkv_a2, rwkv_g1, rwkv_g2, rwkv_k_k, rwkv_k_a, rwkv_r_k, rwkv_ln_w, rwkv_ln_b, rwkv_w_o, rwkv_v0, rwkv_v1, rwkv_v2, ret_w_in, ret_w_o, lru_w_in, lru_conv_w, lru_conv_b, lru_gate_w, lru_gate_b, lru_lambda, lru_w_o):
    depth = ada_w.shape[0]
    d_ff = ffn_w_out.shape[2]
    res = x.astype(F32)
    mod = _ada_mod(c.astype(F32), ada_w, ada_b)
    pos_f = positions.astype(F32)[..., None]
    v_first = None
    for i in range(depth):
        ng = norm_g[i]
        ffn = lambda res, m, s: _ffn_sublayer(
            res, mod[i], ng, ffn_w_in[i, m, :, :d_ff].astype(BF16), ffn_w_in[i, m, :, d_ff:].astype(BF16),
            ffn_w_out[i, m].astype(BF16), s)
        res = ffn(res, 0, 0)
        kind, j = i % 3, i // 3
        if kind == 0:
            p = dict(mu=rwkv_mu[j], w_rkv=rwkv_w_rkv[j], w0=rwkv_w0[j], w1=rwkv_w1[j], w2=rwkv_w2[j],
                     a0=rwkv_a0[j], a1=rwkv_a1[j], a2=rwkv_a2[j], g1=rwkv_g1[j], g2=rwkv_g2[j],
                     k_k=rwkv_k_k[j], k_a=rwkv_k_a[j], r_k=rwkv_r_k[j], ln_w=rwkv_ln_w[j], ln_b=rwkv_ln_b[j])
            if j > 0:
                p.update(v0=rwkv_v0[j - 1], v1=rwkv_v1[j - 1], v2=rwkv_v2[j - 1])
            r, lw, k, v, kk, a, g = _rwkv_pre(res, mod[i], ng, p, v_first if j > 0 else None)
            if j == 0:
                v_first = v
            z = _wkv(r, lw, k, v, kk, a, g, p)
            res = _out_sublayer(z, res, mod[i], ng, rwkv_w_o[j].astype(BF16), 1)
        elif kind == 1:
            q, k, v, g = _ret_pre(res, mod[i], ng, pos_f, ret_w_in[j].astype(BF16))
            res = _ret_mix(q, k, v, g, res, mod[i], ng, ret_w_o[j].astype(BF16))
        else:
            width = lru_w_o.shape[1]
            vec = jnp.stack([lru_conv_b[j], lru_gate_b[j, 0].reshape(width), lru_gate_b[j, 1].reshape(width),
                             lru_lambda[j]])
            gate_w = lru_gate_w[j].reshape(2 * LRU_H, LRU_BW, LRU_BW).astype(BF16)
            res = _lru_sublayer(res, mod[i], ng, lru_w_in[j].astype(BF16), lru_conv_w[j], vec, gate_w,
                                lru_w_o[j].astype(BF16))
        res = ffn(res, 1, 2)
    return res.astype(x.dtype)
```

```python
import functools
import math

import jax
import jax.numpy as jnp
from jax import lax
from jax.experimental import pallas as pl
from jax.experimental.pallas import tpu as pltpu

F32 = jnp.float32
BF16 = jnp.bfloat16

NORM_EPS = 1e-6
N_SUB = 3

RWKV_N = 64
RWKV_GROUP = 4
RWKV_GW = RWKV_GROUP * RWKV_N
RWKV_CHUNK = 64
RWKV_GN_EPS = 64e-5
LORA_PAD = 128

RET_H = 4
RET_DK = 256
RET_DV = 512
RET_CHUNK = 128
ROPE_BASE = 10000.0
HEAD_NORM_EPS = 1e-5

LRU_H = 5
LRU_BW = 256
CONV_W = 4
LRU_C = 8.0
SUBLANES = 8

VMEM_LIMIT = 56 * 1024 * 1024

NT_DIMS = (((1,), (1,)), ((), ()))
TN_DIMS = (((0,), (0,)), ((), ()))


def _mm(a, b):
    return jnp.dot(a.astype(BF16), b.astype(BF16), preferred_element_type=F32)


def _mm_nt(a, b):
    return lax.dot_general(a.astype(BF16), b.astype(BF16), NT_DIMS, preferred_element_type=F32)


def _mm_tn(a, b):
    return lax.dot_general(a.astype(BF16), b.astype(BF16), TN_DIMS, preferred_element_type=F32)


def _mm_split_lhs_exact(w, x):
    hi = x.astype(BF16)
    lo = (x - hi.astype(F32)).astype(BF16)
    return (jnp.dot(w, hi, preferred_element_type=F32) + jnp.dot(w, lo, preferred_element_type=F32))


def _rms(x, g):
    return x * lax.rsqrt(jnp.mean(x * x, axis=-1, keepdims=True) + NORM_EPS) * g


def _sublayer_in(res, mod, ng, s):
    return _rms(res, ng[2 * s:2 * s + 1]) * (1.0 + mod[3 * s + 1:3 * s + 2]) + mod[3 * s:3 * s + 1]


def _sublayer_out(res, y, mod, ng, s, weight):
    return res + weight * mod[3 * s + 2:3 * s + 3] * _rms(y, ng[2 * s + 1:2 * s + 2])


def _sigmoid(x):
    return jax.nn.sigmoid(x)


def _resident(shape):
    nd = len(shape)
    return pl.BlockSpec(shape, lambda *_: (0,) * nd, pipeline_mode=pl.Buffered(1))


def _params(n_axes):
    return pltpu.CompilerParams(dimension_semantics=("arbitrary",) * n_axes, vmem_limit_bytes=VMEM_LIMIT)


def _tile(t_len, want):
    tm = min(want, t_len)
    assert t_len % tm == 0, (t_len, tm)
    return tm


def _ada_kernel(c_ref, w_ref, b_ref, o_ref):
    c = c_ref[...]
    cond = c * _sigmoid(c)
    o_ref[0] = _mm(cond, w_ref[0]) + b_ref[0]


def _ada_mod(c, ada_w, ada_b):
    depth, d, n = ada_w.shape
    bsz = c.shape[0]
    tn = 1536
    assert n % tn == 0
    out = pl.pallas_call(
        _ada_kernel,
        grid=(depth, n // tn),
        in_specs=[pl.BlockSpec((bsz, d), lambda l, j: (0, 0)),
                  pl.BlockSpec((1, d, tn), lambda l, j: (l, 0, j)),
                  pl.BlockSpec((1, 1, tn), lambda l, j: (l, 0, j))],
        out_specs=pl.BlockSpec((1, bsz, tn), lambda l, j: (l, 0, j)),
        out_shape=jax.ShapeDtypeStruct((depth, bsz, n), F32),
        compiler_params=_params(2),
        name="ada_mod",
    )(c, ada_w, ada_b.reshape(depth, 1, n))
    return out.reshape(depth, bsz, 3 * N_SUB, d)


def _ffn_kernel(s, n_chunks, res_ref, mod_ref, ng_ref, wa_ref, wb_ref, wo_ref, out_ref):
    mod = mod_ref[0]
    ng = ng_ref[...]
    f = wa_ref.shape[1]
    tf = f // n_chunks
    tm = res_ref.shape[1]
    halves = [slice(0, tm // 2), slice(tm // 2, tm)]
    res = [res_ref[0, rows, :] for rows in halves]
    h = [_sublayer_in(x, mod, ng, s).astype(BF16) for x in res]
    y = [jnp.zeros(x.shape, F32) for x in res]
    for c in range(n_chunks):
        for i in range(len(halves)):
            a = jnp.dot(h[i], wa_ref[:, c * tf:(c + 1) * tf], preferred_element_type=F32)
            b = jnp.dot(h[i], wb_ref[:, c * tf:(c + 1) * tf], preferred_element_type=F32)
            z = (a * _sigmoid(a) * b).astype(BF16)
            y[i] = y[i] + jnp.dot(z, wo_ref[c * tf:(c + 1) * tf, :], preferred_element_type=F32)
    for i, rows in enumerate(halves):
        out_ref[0, rows, :] = _sublayer_out(res[i], y[i], mod, ng, s, 0.5)


def _ffn_sublayer(res, mod, ng, w_a, w_b, w_o, s):
    bsz, t_len, d = res.shape
    f = w_a.shape[1]
    tm = _tile(t_len, 512)
    return pl.pallas_call(
        functools.partial(_ffn_kernel, s, 2),
        grid=(bsz, t_len // tm),
        in_specs=[pl.BlockSpec((1, tm, d), lambda b, t: (b, t, 0)),
                  pl.BlockSpec((1, 3 * N_SUB, d), lambda b, t: (b, 0, 0)),
                  _resident(ng.shape), _resident((d, f)), _resident((d, f)), _resident((f, d))],
        out_specs=pl.BlockSpec((1, tm, d), lambda b, t: (b, t, 0)),
        out_shape=jax.ShapeDtypeStruct(res.shape, F32),
        compiler_params=_params(2),
        name="ffn_sublayer",
    )(res, mod, ng, w_a, w_b, w_o)


def _rwkv_pre_kernel(has_vres, *refs):
    if has_vres:
        (res_ref, mod_ref, ng_ref, mu_ref, vec_ref, wr_ref, wk_ref, wv_ref, w1_ref, w2_ref, a1_ref, a2_ref,
         g1_ref, g2_ref, v1_ref, v2_ref, v0_ref, vfirst_ref,
         r_ref, lw_ref, k_ref, v_ref, kk_ref, a_ref, g_ref, hext) = refs
    else:
        (res_ref, mod_ref, ng_ref, mu_ref, vec_ref, wr_ref, wk_ref, wv_ref, w1_ref, w2_ref, a1_ref, a2_ref,
         g1_ref, g2_ref,
         r_ref, lw_ref, k_ref, v_ref, kk_ref, a_ref, g_ref, hext) = refs
    tm = res_ref.shape[1]

    @pl.when(pl.program_id(1) == 0)
    def _():
        hext[0:SUBLANES, :] = jnp.zeros((SUBLANES, hext.shape[1]), F32)

    h = _sublayer_in(res_ref[0], mod_ref[0], ng_ref[...], 1)
    hext[SUBLANES:SUBLANES + tm, :] = h
    h_prev = hext[SUBLANES - 1:SUBLANES - 1 + tm, :]
    hext[0:SUBLANES, :] = hext[tm:tm + SUBLANES, :]
    dx = h_prev - h
    mu = mu_ref[...]
    xr, xw, xk, xv, xa, xg = ((h + dx * mu[j:j + 1]).astype(BF16) for j in range(6))
    vec = vec_ref[...]
    w0, a0, k_k, k_a = vec[0:1], vec[1:2], vec[2:3], vec[3:4]

    r = jnp.dot(xr, wr_ref[...], preferred_element_type=F32)
    k = jnp.dot(xk, wk_ref[...], preferred_element_type=F32)
    v = jnp.dot(xv, wv_ref[...], preferred_element_type=F32)
    zw = w0 + _mm(jnp.tanh(jnp.dot(xw, w1_ref[...], preferred_element_type=F32)), w2_ref[...])
    lw_ref[0] = -math.exp(-0.5) * _sigmoid(zw)
    a = _sigmoid(a0 + _mm(jnp.dot(xa, a1_ref[...], preferred_element_type=F32), a2_ref[...]))
    g_ref[0] = _mm(_sigmoid(jnp.dot(xg, g1_ref[...], preferred_element_type=F32)), g2_ref[...]).astype(BF16)
    if has_vres:
        mix = _sigmoid(v0_ref[...] + _mm(jnp.dot(xv, v1_ref[...], preferred_element_type=F32), v2_ref[...]))
        v = v + (vfirst_ref[0].astype(F32) - v) * mix
    r_ref[0] = r.astype(BF16)
    v_ref[0] = v.astype(BF16)
    a_ref[0] = a.astype(BF16)
    kk_ref[0] = (k * k_k).astype(BF16)
    k_ref[0] = (k * (1.0 + (a - 1.0) * k_a)).astype(BF16)


def _pad_cols(w, n):
    return jnp.pad(w, ((0, 0), (0, n - w.shape[1])))


def _pad_rows(w, n):
    return jnp.pad(w, ((0, n - w.shape[0]), (0, 0)))


def _rwkv_pre(res, mod, ng, p, v_first):
    bsz, t_len, d = res.shape
    tm = _tile(t_len, 256)
    has_vres = v_first is not None
    tok = pl.BlockSpec((1, tm, d), lambda b, t: (b, t, 0))
    lora_in = lambda w: _pad_cols(w, LORA_PAD).astype(BF16)
    lora_out = lambda w: _pad_rows(w, LORA_PAD).astype(BF16)
    vec = jnp.stack([p["w0"], p["a0"], p["k_k"], p["k_a"]])
    args = [res, mod, ng, p["mu"], vec,
            p["w_rkv"][0].astype(BF16), p["w_rkv"][1].astype(BF16), p["w_rkv"][2].astype(BF16),
            lora_in(p["w1"]), lora_out(p["w2"]), lora_in(p["a1"]), lora_out(p["a2"]),
            lora_in(p["g1"]), lora_out(p["g2"])]
    in_specs = [tok, pl.BlockSpec((1, 3 * N_SUB, d), lambda b, t: (b, 0, 0))]
    in_specs += [_resident(a.shape) for a in args[2:]]
    if has_vres:
        extra = [lora_in(p["v1"]), lora_out(p["v2"]), p["v0"].reshape(1, d)]
        args += extra + [v_first]
        in_specs += [_resident(a.shape) for a in extra] + [tok]
    outs = pl.pallas_call(
        functools.partial(_rwkv_pre_kernel, has_vres),
        grid=(bsz, t_len // tm),
        in_specs=in_specs,
        out_specs=[tok] * 7,
        out_shape=[jax.ShapeDtypeStruct(res.shape, F32 if i == 1 else BF16) for i in range(7)],
        scratch_shapes=[pltpu.VMEM((tm + SUBLANES, d), F32)],
        compiler_params=_params(2),
        name="rwkv_pre",
    )(*args)
    return outs


def _block_diag(x, bd_mask):
    xb = x.astype(BF16)
    return jnp.where(bd_mask, jnp.concatenate([xb] * RWKV_GROUP, axis=0), jnp.zeros((), BF16))


class _Staged:
    def __init__(self, gen):
        self.gen, self.done, self.value = gen, False, None

    def step(self):
        if not self.done:
            try:
                next(self.gen)
            except StopIteration as stop:
                self.done, self.value = True, stop.value

    def finish(self):
        while not self.done:
            self.step()
        return self.value


def _wkv_prepare(raw, masks, consts):
    bd_mask, strict, incl, eye = masks
    tril = consts
    c = RWKV_CHUNK
    bd = lambda x: _block_diag(x, bd_mask)
    cum = [_mm_split_lhs_exact(tril, lw) for r, lw, k, v, kkn, a in raw]
    yield
    units = []
    for (r, lw, k, v, kkn, a), cum_i in zip(raw, cum):
        g_t = jnp.exp(cum_i)
        g_inv = jnp.exp(-cum_i)
        units.append((r * g_t, -kkn * jnp.exp(cum_i - lw), kkn * a * g_inv, k * g_inv, v, g_t[c - 1:c]))
    ar = [jnp.concatenate([at, rt], axis=0).astype(BF16) for rt, at, bt, kt, v, gl in units]
    sv = [bd(u[4]) for u in units]
    lb = [_mm_nt(x, bd(u[2])) for x, u in zip(ar, units)]
    lk = [_mm_nt(x, bd(u[3])) for x, u in zip(ar, units)]
    yield
    l_ab = [jnp.where(strict, x[:c], 0.0) for x in lb]
    l_rb = [jnp.where(incl, x[c:], 0.0) for x in lb]
    l_ak = [jnp.where(strict, x[:c], 0.0) for x in lk]
    l_rk = [jnp.where(incl, x[c:], 0.0) for x in lk]
    pw = [_mm(x, bd(x)) for x in l_ab]
    inv = [eye + x for x in l_ab]
    yield
    for _ in range(4):
        both = [_mm(jnp.concatenate([p, t], axis=0), bd(p)) for p, t in zip(pw, inv)]
        pw = [x[:c] for x in both]
        inv = [t + x[c:] for t, x in zip(inv, both)]
        yield
    inv = [t + _mm(t, bd(p)) for t, p in zip(inv, pw)]
    kv = [_mm(jnp.concatenate([x, y], axis=0), s) for x, y, s in zip(l_ak, l_rk, sv)]
    ak_v = [x[:c] for x in kv]
    rk_v = [x[c:] for x in kv]
    bk = [jnp.concatenate([bt, kt], axis=0).astype(BF16) for rt, at, bt, kt, v, gl in units]
    return [dict(ar=ar[i], inv=inv[i], l_rb=l_rb[i], ak_v=ak_v[i], rk_v=rk_v[i], bk=bk[i], v=units[i][4],
                 g_last=units[i][5]) for i in range(len(units))]


def _wkv_advance(get_prepared, n_chunks, z_states, masks):
    bd_mask = masks[0]
    c = RWKV_CHUNK
    bd = lambda x: _block_diag(x, bd_mask)
    ys = []
    for i in range(n_chunks):
        prepared = get_prepared(i)
        ars = [_mm_nt(p["ar"], z) for p, z in zip(prepared, z_states)]
        yield
        u = [_mm(p["inv"], bd(x[:c] + p["ak_v"])) for p, x in zip(prepared, ars)]
        yield
        y = [x[c:] + _mm(p["l_rb"], bd(uu)) + p["rk_v"] for p, x, uu in zip(prepared, ars, u)]
        dz = [_mm_tn(jnp.concatenate([uu, p["v"]], axis=0), p["bk"]) for p, uu in zip(prepared, u)]
        z_states = [(z + jnp.where(bd_mask, d, 0.0)) * p["g_last"] for z, d, p in zip(z_states, dz, prepared)]
        ys.append(y)
        yield
    return ys, z_states


def _wkv_masks():
    gw, c = RWKV_GW, RWKV_CHUNK
    row = lax.broadcasted_iota(jnp.int32, (gw, gw), 0)
    col = lax.broadcasted_iota(jnp.int32, (gw, gw), 1)
    bd_mask = (row // RWKV_N) == (col // RWKV_N)
    t = lax.broadcasted_iota(jnp.int32, (c, gw), 0)
    j = lax.broadcasted_iota(jnp.int32, (c, gw), 1) % c
    eye = jnp.where(t == j, 1.0, 0.0).astype(F32)
    rc = lax.broadcasted_iota(jnp.int32, (c, c), 0)
    cc = lax.broadcasted_iota(jnp.int32, (c, c), 1)
    tril = jnp.where(rc >= cc, 1.0, 0.0).astype(BF16)
    bd_ones = jnp.where(bd_mask, 1.0, 0.0).astype(BF16)
    return (bd_mask, t > j, t >= j, eye), tril, bd_ones


def _wkv_kernel(n_chunks, r_ref, lw_ref, k_ref, v_ref, kk_ref, a_ref, g_ref, vec_ref, res_ref, mod_ref, ng_ref, wo_ref,
                out_ref, state, z_buf):
    @pl.when(pl.program_id(1) == 0)
    def _():
        state[...] = jnp.zeros(state.shape, F32)

    c = RWKV_CHUNK
    n_groups = state.shape[0]
    masks, tril, bd_ones = _wkv_masks()
    lanes = [slice(gi * RWKV_GW, (gi + 1) * RWKV_GW) for gi in range(n_groups)]
    tb = n_chunks * c

    def head_sums(per_group):
        s = _mm(jnp.concatenate(per_group, axis=0), bd_ones)
        return [s[gi * tb:(gi + 1) * tb] for gi in range(n_groups)]

    load = lambda ref: [ref[0, :, ln].astype(F32) for ln in lanes]
    r, lw, k, v, kk, a = (load(ref) for ref in (r_ref, lw_ref, k_ref, v_ref, kk_ref, a_ref))
    kkn = [x / jnp.maximum(jnp.sqrt(n2), 1e-12) for x, n2 in zip(kk, head_sums([x * x for x in kk]))]
    bonus = head_sums([r[gi] * k[gi] * vec_ref[0:1, lanes[gi]] for gi in range(n_groups)])

    def raw_units(i):
        rows = slice(i * c, (i + 1) * c)
        return [tuple(x[gi][rows] for x in (r, lw, k, v, kkn, a)) for gi in range(n_groups)]

    lead = min(2, n_chunks)
    first = _Staged(_wkv_prepare([u for i in range(lead) for u in raw_units(i)], masks, tril)).finish()
    later = [_Staged(_wkv_prepare(raw_units(i), masks, tril)) for i in range(lead, n_chunks)]

    def get_prepared(i):
        if i < lead:
            return first[i * n_groups:(i + 1) * n_groups]
        return later[i - lead].finish()

    advance = _Staged(_wkv_advance(get_prepared, n_chunks, [state[gi] for gi in range(n_groups)], masks))
    while not advance.done:
        advance.step()
        for task in later:
            if not task.done:
                task.step()
                break
    ys, z_states = advance.value
    for gi in range(n_groups):
        state[gi] = z_states[gi]
    y = [jnp.concatenate([ys[i][gi] for i in range(n_chunks)], axis=0) for gi in range(n_groups)]
    yc = [x - m * (1.0 / RWKV_N) for x, m in zip(y, head_sums(y))]
    var = head_sums([x * x for x in yc])
    for gi, ln in enumerate(lanes):
        yn = yc[gi] * lax.rsqrt(var[gi] * (1.0 / RWKV_N) + RWKV_GN_EPS) * vec_ref[1:2, ln] + vec_ref[2:3, ln]
        z_buf[:, ln] = ((yn + bonus[gi] * v[gi]) * g_ref[0, :, ln].astype(F32)).astype(BF16)
    y_out = jnp.dot(z_buf[...], wo_ref[...], preferred_element_type=F32)
    out_ref[0] = _sublayer_out(res_ref[0], y_out, mod_ref[0], ng_ref[...], 1, 1.0)


def _wkv_sublayer(r, lw, k, v, kk, a, g, p, res, mod, ng, w_o):
    bsz, t_len, d = r.shape
    n_groups = d // RWKV_GW
    tb = _tile(t_len, 256)
    vec = jnp.stack([p["r_k"].reshape(d), p["ln_w"], p["ln_b"]])
    tok = pl.BlockSpec((1, tb, d), lambda b, t: (b, t, 0))
    return pl.pallas_call(
        functools.partial(_wkv_kernel, tb // RWKV_CHUNK),
        grid=(bsz, t_len // tb),
        in_specs=[tok] * 7 + [_resident(vec.shape), tok, pl.BlockSpec((1, 3 * N_SUB, d), lambda b, t: (b, 0, 0)),
                             _resident(ng.shape), _resident(w_o.shape)],
        out_specs=tok,
        out_shape=jax.ShapeDtypeStruct(res.shape, F32),
        scratch_shapes=[pltpu.VMEM((n_groups, RWKV_GW, RWKV_GW), F32), pltpu.VMEM((tb, d), BF16)],
        compiler_params=_params(2),
        name="wkv7",
    )(r, lw, k, v, kk, a, g, vec, res, mod, ng, w_o)


def _ret_pre_kernel(res_ref, mod_ref, ng_ref, pos_ref, inv_ref, w_ref, q_ref, k_ref, v_ref, g_ref):
    h = _sublayer_in(res_ref[0], mod_ref[0], ng_ref[...], 1).astype(BF16)
    ang = pos_ref[0] * inv_ref[...]
    cos = jnp.cos(ang)
    sin = jnp.sin(ang)
    half = RET_DK // 2
    qk = RET_H * RET_DK
    for which, out_ref, scale in ((0, q_ref, 1.0), (1, k_ref, RET_DK ** -0.5)):
        for hd in range(RET_H):
            lo = which * qk + hd * RET_DK
            x = jnp.dot(h, w_ref[:, lo:lo + RET_DK], preferred_element_type=F32)
            x1, x2 = x[:, :half], x[:, half:]
            out_ref[0, :, hd * RET_DK:hd * RET_DK + half] = ((x1 * cos - x2 * sin) * scale).astype(BF16)
            out_ref[0, :, hd * RET_DK + half:(hd + 1) * RET_DK] = ((x1 * sin + x2 * cos) * scale).astype(BF16)
    nv = RET_H * RET_DV
    for hd in range(RET_H):
        lo = 2 * qk + hd * RET_DV
        v_ref[0, :, hd * RET_DV:(hd + 1) * RET_DV] = jnp.dot(
            h, w_ref[:, lo:lo + RET_DV], preferred_element_type=F32).astype(BF16)
        g_ref[0, :, hd * RET_DV:(hd + 1) * RET_DV] = jnp.dot(
            h, w_ref[:, lo + nv:lo + nv + RET_DV], preferred_element_type=F32).astype(BF16)


def _ret_pre(res, mod, ng, pos_f, w_in):
    bsz, t_len, d = res.shape
    tm = _tile(t_len, 512)
    qk = RET_H * RET_DK
    nv = RET_H * RET_DV
    half = RET_DK // 2
    inv = (1.0 / (ROPE_BASE ** jnp.linspace(0.0, 1.0, half, dtype=F32))).reshape(1, half)
    tok = lambda n: pl.BlockSpec((1, tm, n), lambda b, t: (b, t, 0))
    return pl.pallas_call(
        _ret_pre_kernel,
        grid=(bsz, t_len // tm),
        in_specs=[tok(d), pl.BlockSpec((1, 3 * N_SUB, d), lambda b, t: (b, 0, 0)), _resident(ng.shape),
                  tok(1), _resident((1, half)), _resident(w_in.shape)],
        out_specs=[tok(qk), tok(qk), tok(nv), tok(nv)],
        out_shape=[jax.ShapeDtypeStruct((bsz, t_len, qk), BF16), jax.ShapeDtypeStruct((bsz, t_len, qk), BF16),
                   jax.ShapeDtypeStruct((bsz, t_len, nv), BF16), jax.ShapeDtypeStruct((bsz, t_len, nv), BF16)],
        compiler_params=_params(2),
        name="ret_pre",
    )(res, mod, ng, pos_f, inv, w_in)


def _ret_mix_kernel(n_chunks, q_ref, k_ref, v_ref, g_ref, res_ref, mod_ref, ng_ref, wo_ref, out_ref, state, z_buf):
    @pl.when(pl.program_id(1) == 0)
    def _():
        state[...] = jnp.zeros(state.shape, F32)

    c = RET_CHUNK
    row = lax.broadcasted_iota(jnp.int32, (c, c), 0)
    col = lax.broadcasted_iota(jnp.int32, (c, c), 1)
    diff = (row - col).astype(F32)
    idx = lax.broadcasted_iota(jnp.int32, (c, 1), 0).astype(F32)
    for hd in range(RET_H):
        log_gamma = math.log(1.0 - 2.0 ** (-5.0 - hd))
        inner = jnp.where(diff >= 0, jnp.exp(log_gamma * jnp.maximum(diff, 0.0)), 0.0)
        q_decay = jnp.exp(log_gamma * (idx + 1.0))
        k_decay = jnp.exp(log_gamma * (c - 1.0 - idx))
        chunk_decay = math.exp(log_gamma * c)
        for i in range(n_chunks):
            rows = slice(i * c, (i + 1) * c)
            q_c = q_ref[0, rows, hd * RET_DK:(hd + 1) * RET_DK]
            k_c = k_ref[0, rows, hd * RET_DK:(hd + 1) * RET_DK]
            v_c = v_ref[0, rows, hd * RET_DV:(hd + 1) * RET_DV]
            r_state = state[hd]
            s = _mm_nt(q_c, k_c) * inner
            o = _mm(s, v_c) + _mm(q_c, r_state) * q_decay
            state[hd] = r_state * chunk_decay + _mm_tn(k_c.astype(F32) * k_decay, v_c)
            oc = o - jnp.mean(o, axis=-1, keepdims=True)
            on = oc * lax.rsqrt(jnp.mean(oc * oc, axis=-1, keepdims=True) + HEAD_NORM_EPS)
            gt = g_ref[0, rows, hd * RET_DV:(hd + 1) * RET_DV].astype(F32)
            z_buf[rows, hd * RET_DV:(hd + 1) * RET_DV] = (gt * _sigmoid(gt) * on).astype(BF16)
    y = jnp.dot(z_buf[...], wo_ref[...], preferred_element_type=F32)
    out_ref[0] = _sublayer_out(res_ref[0], y, mod_ref[0], ng_ref[...], 1, 1.0)


def _ret_mix(q, k, v, g, res, mod, ng, w_o):
    bsz, t_len, d = res.shape
    tm = _tile(t_len, 512)
    qk = RET_H * RET_DK
    nv = RET_H * RET_DV
    tok = lambda n: pl.BlockSpec((1, tm, n), lambda b, t: (b, t, 0))
    return pl.pallas_call(
        functools.partial(_ret_mix_kernel, tm // RET_CHUNK),
        grid=(bsz, t_len // tm),
        in_specs=[tok(qk), tok(qk), tok(nv), tok(nv), tok(d),
                  pl.BlockSpec((1, 3 * N_SUB, d), lambda b, t: (b, 0, 0)), _resident(ng.shape), _resident(w_o.shape)],
        out_specs=tok(d),
        out_shape=jax.ShapeDtypeStruct(res.shape, F32),
        scratch_shapes=[pltpu.VMEM((RET_H, RET_DK, RET_DV), F32), pltpu.VMEM((tm, nv), BF16)],
        compiler_params=_params(2),
        name="ret_mix",
    )(q, k, v, g, res, mod, ng, w_o)


def _lru_kernel(res_ref, mod_ref, ng_ref, win_ref, cw_ref, vec_ref, gw_ref, wo_ref, out_ref,
                xext, a_buf, u_buf, h_carry):
    tm = res_ref.shape[1]
    width = a_buf.shape[1]

    @pl.when(pl.program_id(1) == 0)
    def _():
        xext[0:SUBLANES, :] = jnp.zeros((SUBLANES, width), F32)
        h_carry[...] = jnp.zeros(h_carry.shape, F32)

    res = res_ref[0]
    mod = mod_ref[0]
    ng = ng_ref[...]
    h = _sublayer_in(res, mod, ng, 1).astype(BF16)
    gate_branch = jnp.dot(h, win_ref[:, :width], preferred_element_type=F32)
    xb = jnp.dot(h, win_ref[:, width:], preferred_element_type=F32)
    xext[SUBLANES:SUBLANES + tm, :] = xb
    cw = cw_ref[...]
    vec = vec_ref[...]
    conv_b, gate_bi, gate_br, lam = vec[0:1], vec[1:2], vec[2:3], vec[3:4]
    xc = conv_b + cw[CONV_W - 1:CONV_W] * xb
    for j in range(CONV_W - 1):
        lo = SUBLANES - (CONV_W - 1) + j
        xc = xc + cw[j:j + 1] * xext[lo:lo + tm, :]
    xext[0:SUBLANES, :] = xext[tm:tm + SUBLANES, :]

    xcb = xc.astype(BF16)
    gates = []
    for gi in range(2):
        parts = [jnp.dot(xcb[:, hd * LRU_BW:(hd + 1) * LRU_BW], gw_ref[gi * LRU_H + hd], preferred_element_type=F32)
                 for hd in range(LRU_H)]
        gates.append(jnp.concatenate(parts, axis=1))
    i_gate = _sigmoid(gates[0] + gate_bi)
    r_gate = _sigmoid(gates[1] + gate_br)
    neg_lam = -lam
    softplus = jnp.maximum(neg_lam, 0.0) + jnp.log1p(jnp.exp(-jnp.abs(neg_lam)))
    log_a = -LRU_C * r_gate * softplus
    a = jnp.exp(log_a)
    a_buf[...] = a
    u_buf[...] = jnp.sqrt(-jnp.tanh(log_a) * (a * a + 1.0)) * (i_gate * xc)

    row = lax.broadcasted_iota(jnp.int32, (SUBLANES, width), 0)

    def group(i, carry):
        rows = pl.ds(pl.multiple_of(i * SUBLANES, SUBLANES), SUBLANES)
        ag = a_buf[rows, :]
        ug = u_buf[rows, :]
        for d in (1, 2, 4):
            keep = row >= d
            u_prev = jnp.where(keep, pltpu.roll(ug, d, 0), 0.0)
            a_prev = jnp.where(keep, pltpu.roll(ag, d, 0), 1.0)
            ug = ug + ag * u_prev
            ag = ag * a_prev
        hg = ug + ag * carry
        u_buf[rows, :] = hg
        return jnp.broadcast_to(hg[SUBLANES - 1:SUBLANES, :], (SUBLANES, width))

    h_carry[...] = lax.fori_loop(0, tm // SUBLANES, group, h_carry[...])

    gb = gate_branch
    gelu = 0.5 * gb * (1.0 + jnp.tanh(math.sqrt(2.0 / math.pi) * (gb + 0.044715 * (gb * gb * gb))))
    z = (gelu * u_buf[...]).astype(BF16)
    y = jnp.dot(z, wo_ref[...], preferred_element_type=F32)
    out_ref[0] = _sublayer_out(res, y, mod, ng, 1, 1.0)


def _lru_sublayer(res, mod, ng, w_in, conv_w, vec, gate_w, w_o):
    bsz, t_len, d = res.shape
    width = w_o.shape[0]
    tm = _tile(t_len, 256)
    tok = pl.BlockSpec((1, tm, d), lambda b, t: (b, t, 0))
    return pl.pallas_call(
        _lru_kernel,
        grid=(bsz, t_len // tm),
        in_specs=[tok, pl.BlockSpec((1, 3 * N_SUB, d), lambda b, t: (b, 0, 0)), _resident(ng.shape),
                  _resident(w_in.shape), _resident(conv_w.shape), _resident(vec.shape),
                  _resident(gate_w.shape), _resident(w_o.shape)],
        out_specs=tok,
        out_shape=jax.ShapeDtypeStruct(res.shape, F32),
        scratch_shapes=[pltpu.VMEM((tm + SUBLANES, width), F32), pltpu.VMEM((tm, width), F32),
                        pltpu.VMEM((tm, width), F32), pltpu.VMEM((SUBLANES, width), F32)],
        compiler_params=_params(2),
        name="lru_sublayer",
    )(res, mod, ng, w_in, conv_w, vec, gate_w, w_o)


def kernel(x, c, positions, ada_w, ada_b, norm_g, ffn_w_in, ffn_w_out, rwkv_mu, rwkv_w_rkv, rwkv_w0, rwkv_w1, rwkv_w2, rwkv_a0, rwkv_a1, rwkv_a2, rwkv_g1, rwkv_g2, rwkv_k_k, rwkv_k_a, rwkv_r_k, rwkv_ln_w, rwkv_ln_b, rwkv_w_o, rwkv_v0, rwkv_v1, rwkv_v2, ret_w_in, ret_w_o, lru_w_in, lru_conv_w, lru_conv_b, lru_gate_w, lru_gate_b, lru_lambda, lru_w_o):
    depth = ada_w.shape[0]
    d_ff = ffn_w_out.shape[2]
    res = x.astype(F32)
    mod = _ada_mod(c.astype(F32), ada_w, ada_b)
    pos_f = positions.astype(F32)[..., None]
    v_first = None
    for i in range(depth):
        ng = norm_g[i]
        ffn = lambda res, m, s: _ffn_sublayer(
            res, mod[i], ng, ffn_w_in[i, m, :, :d_ff].astype(BF16), ffn_w_in[i, m, :, d_ff:].astype(BF16),
            ffn_w_out[i, m].astype(BF16), s)
        res = ffn(res, 0, 0)
        kind, j = i % 3, i // 3
        if kind == 0:
            p = dict(mu=rwkv_mu[j], w_rkv=rwkv_w_rkv[j], w0=rwkv_w0[j], w1=rwkv_w1[j], w2=rwkv_w2[j],
                     a0=rwkv_a0[j], a1=rwkv_a1[j], a2=rwkv_a2[j], g1=rwkv_g1[j], g2=rwkv_g2[j],
                     k_k=rwkv_k_k[j], k_a=rwkv_k_a[j], r_k=rwkv_r_k[j], ln_w=rwkv_ln_w[j], ln_b=rwkv_ln_b[j])
            if j > 0:
                p.update(v0=rwkv_v0[j - 1], v1=rwkv_v1[j - 1], v2=rwkv_v2[j - 1])
            r, lw, k, v, kk, a, g = _rwkv_pre(res, mod[i], ng, p, v_first if j > 0 else None)
            if j == 0:
                v_first = v
            res = _wkv_sublayer(r, lw, k, v, kk, a, g, p, res, mod[i], ng, rwkv_w_o[j].astype(BF16))
        elif kind == 1:
            q, k, v, g = _ret_pre(res, mod[i], ng, pos_f, ret_w_in[j].astype(BF16))
            res = _ret_mix(q, k, v, g, res, mod[i], ng, ret_w_o[j].astype(BF16))
        else:
            width = lru_w_o.shape[1]
            vec = jnp.stack([lru_conv_b[j], lru_gate_b[j, 0].reshape(width), lru_gate_b[j, 1].reshape(width),
                             lru_lambda[j]])
            gate_w = lru_gate_w[j].reshape(2 * LRU_H, LRU_BW, LRU_BW).astype(BF16)
            res = _lru_sublayer(res, mod[i], ng, lru_w_in[j].astype(BF16), lru_conv_w[j], vec, gate_w,
                                lru_w_o[j].astype(BF16))
        res = ffn(res, 1, 2)
    return res.astype(x.dtype)
```

```python
import functools
import math

import jax
import jax.numpy as jnp
from jax import lax
from jax.experimental import pallas as pl
from jax.experimental.pallas import tpu as pltpu

F32 = jnp.float32
BF16 = jnp.bfloat16

NORM_EPS = 1e-6
N_SUB = 3

RWKV_N = 64
RWKV_GROUP = 4
RWKV_GW = RWKV_GROUP * RWKV_N
RWKV_CHUNK = 64
RWKV_GN_EPS = 64e-5
LORA_PAD = 128

RET_H = 4
RET_DK = 256
RET_DV = 512
RET_CHUNK = 128
ROPE_BASE = 10000.0
HEAD_NORM_EPS = 1e-5

LRU_H = 5
LRU_BW = 256
CONV_W = 4
LRU_C = 8.0
SUBLANES = 8

VMEM_LIMIT = 56 * 1024 * 1024
FFN_TILE = 1024
FFN_ROW_PARTS = 4
RWKV_PRE_TILE = 512
RWKV_PRE_PARTS = 2
LRU_TILE = 256
LRU_ROW_PARTS = 2

NT_DIMS = (((1,), (1,)), ((), ()))
TN_DIMS = (((0,), (0,)), ((), ()))


def _mm(a, b):
    return jnp.dot(a.astype(BF16), b.astype(BF16), preferred_element_type=F32)


def _mm_nt(a, b):
    return lax.dot_general(a.astype(BF16), b.astype(BF16), NT_DIMS, preferred_element_type=F32)


def _mm_tn(a, b):
    return lax.dot_general(a.astype(BF16), b.astype(BF16), TN_DIMS, preferred_element_type=F32)


def _mm_split_lhs_exact(w, x):
    hi = x.astype(BF16)
    lo = (x - hi.astype(F32)).astype(BF16)
    return (jnp.dot(w, hi, preferred_element_type=F32) + jnp.dot(w, lo, preferred_element_type=F32))


def _rms(x, g):
    return x * lax.rsqrt(jnp.mean(x * x, axis=-1, keepdims=True) + NORM_EPS) * g


def _sublayer_in(res, mod, ng, s):
    return _rms(res, ng[2 * s:2 * s + 1]) * (1.0 + mod[3 * s + 1:3 * s + 2]) + mod[3 * s:3 * s + 1]


def _sublayer_out(res, y, mod, ng, s, weight):
    return res + weight * mod[3 * s + 2:3 * s + 3] * _rms(y, ng[2 * s + 1:2 * s + 2])


def _sigmoid(x):
    return jax.nn.sigmoid(x)


def _resident(shape):
    nd = len(shape)
    return pl.BlockSpec(shape, lambda *_: (0,) * nd, pipeline_mode=pl.Buffered(1))


def _params(n_axes):
    return pltpu.CompilerParams(dimension_semantics=("arbitrary",) * n_axes, vmem_limit_bytes=VMEM_LIMIT)


def _tile(t_len, want):
    tm = min(want, t_len)
    assert t_len % tm == 0, (t_len, tm)
    return tm


def _ada_kernel(c_ref, w_ref, b_ref, o_ref):
    c = c_ref[...]
    cond = c * _sigmoid(c)
    o_ref[0] = _mm(cond, w_ref[0]) + b_ref[0]


def _ada_mod(c, ada_w, ada_b):
    depth, d, n = ada_w.shape
    bsz = c.shape[0]
    tn = 1536
    assert n % tn == 0
    out = pl.pallas_call(
        _ada_kernel,
        grid=(depth, n // tn),
        in_specs=[pl.BlockSpec((bsz, d), lambda l, j: (0, 0)),
                  pl.BlockSpec((1, d, tn), lambda l, j: (l, 0, j)),
                  pl.BlockSpec((1, 1, tn), lambda l, j: (l, 0, j))],
        out_specs=pl.BlockSpec((1, bsz, tn), lambda l, j: (l, 0, j)),
        out_shape=jax.ShapeDtypeStruct((depth, bsz, n), F32),
        compiler_params=_params(2),
        name="ada_mod",
    )(c, ada_w, ada_b.reshape(depth, 1, n))
    return out.reshape(depth, bsz, 3 * N_SUB, d)


def _ffn_kernel(s, n_chunks, res_ref, mod_ref, ng_ref, wa_ref, wb_ref, wo_ref, out_ref):
    mod = mod_ref[0]
    ng = ng_ref[...]
    f = wa_ref.shape[1]
    tf = f // n_chunks
    tm = res_ref.shape[1]
    halves = [slice(i * tm // FFN_ROW_PARTS, (i + 1) * tm // FFN_ROW_PARTS) for i in range(FFN_ROW_PARTS)]
    res = [res_ref[0, rows, :] for rows in halves]
    h = [_sublayer_in(x, mod, ng, s).astype(BF16) for x in res]
    y = [jnp.zeros(x.shape, F32) for x in res]
    for c in range(n_chunks):
        for i in range(len(halves)):
            a = jnp.dot(h[i], wa_ref[:, c * tf:(c + 1) * tf], preferred_element_type=F32)
            b = jnp.dot(h[i], wb_ref[:, c * tf:(c + 1) * tf], preferred_element_type=F32)
            z = (a * _sigmoid(a) * b).astype(BF16)
            y[i] = y[i] + jnp.dot(z, wo_ref[c * tf:(c + 1) * tf, :], preferred_element_type=F32)
    for i, rows in enumerate(halves):
        out_ref[0, rows, :] = _sublayer_out(res[i], y[i], mod, ng, s, 0.5)


def _ffn_sublayer(res, mod, ng, w_a, w_b, w_o, s):
    bsz, t_len, d = res.shape
    f = w_a.shape[1]
    tm = _tile(t_len, FFN_TILE)
    return pl.pallas_call(
        functools.partial(_ffn_kernel, s, 2),
        grid=(bsz, t_len // tm),
        in_specs=[pl.BlockSpec((1, tm, d), lambda b, t: (b, t, 0)),
                  pl.BlockSpec((1, 3 * N_SUB, d), lambda b, t: (b, 0, 0)),
                  _resident(ng.shape), _resident((d, f)), _resident((d, f)), _resident((f, d))],
        out_specs=pl.BlockSpec((1, tm, d), lambda b, t: (b, t, 0)),
        out_shape=jax.ShapeDtypeStruct(res.shape, F32),
        compiler_params=_params(2),
        name="ffn_sublayer",
    )(res, mod, ng, w_a, w_b, w_o)


def _rwkv_pre_kernel(has_vres, *refs):
    if has_vres:
        (res_ref, mod_ref, ng_ref, mu_ref, vec_ref, wr_ref, wk_ref, wv_ref, w1_ref, w2_ref, a1_ref, a2_ref,
         g1_ref, g2_ref, v1_ref, v2_ref, v0_ref, vfirst_ref,
         r_ref, lw_ref, k_ref, v_ref, kk_ref, a_ref, g_ref, hext) = refs
    else:
        (res_ref, mod_ref, ng_ref, mu_ref, vec_ref, wr_ref, wk_ref, wv_ref, w1_ref, w2_ref, a1_ref, a2_ref,
         g1_ref, g2_ref,
         r_ref, lw_ref, k_ref, v_ref, kk_ref, a_ref, g_ref, hext) = refs
    tm = res_ref.shape[1]

    @pl.when(pl.program_id(1) == 0)
    def _():
        hext[0:SUBLANES, :] = jnp.zeros((SUBLANES, hext.shape[1]), F32)

    hext[SUBLANES:SUBLANES + tm, :] = _sublayer_in(res_ref[0], mod_ref[0], ng_ref[...], 1)
    mu = mu_ref[...]
    vec = vec_ref[...]
    w0, a0, k_k, k_a = vec[0:1], vec[1:2], vec[2:3], vec[3:4]
    for part in range(RWKV_PRE_PARTS):
        lo = part * tm // RWKV_PRE_PARTS
        rows = slice(lo, lo + tm // RWKV_PRE_PARTS)
        h = hext[SUBLANES + rows.start:SUBLANES + rows.stop, :]
        dx = hext[SUBLANES - 1 + rows.start:SUBLANES - 1 + rows.stop, :] - h
        xr, xw, xk, xv, xa, xg = ((h + dx * mu[j:j + 1]).astype(BF16) for j in range(6))
        r = jnp.dot(xr, wr_ref[...], preferred_element_type=F32)
        k = jnp.dot(xk, wk_ref[...], preferred_element_type=F32)
        v = jnp.dot(xv, wv_ref[...], preferred_element_type=F32)
        zw = w0 + _mm(jnp.tanh(jnp.dot(xw, w1_ref[...], preferred_element_type=F32)), w2_ref[...])
        lw_ref[0, rows, :] = -math.exp(-0.5) * _sigmoid(zw)
        a = _sigmoid(a0 + _mm(jnp.dot(xa, a1_ref[...], preferred_element_type=F32), a2_ref[...]))
        g_ref[0, rows, :] = _mm(_sigmoid(jnp.dot(xg, g1_ref[...], preferred_element_type=F32)),
                                g2_ref[...]).astype(BF16)
        if has_vres:
            mix = _sigmoid(v0_ref[...] + _mm(jnp.dot(xv, v1_ref[...], preferred_element_type=F32), v2_ref[...]))
            v = v + (vfirst_ref[0, rows, :].astype(F32) - v) * mix
        r_ref[0, rows, :] = r.astype(BF16)
        v_ref[0, rows, :] = v.astype(BF16)
        a_ref[0, rows, :] = a.astype(BF16)
        kk_ref[0, rows, :] = (k * k_k).astype(BF16)
        k_ref[0, rows, :] = (k * (1.0 + (a - 1.0) * k_a)).astype(BF16)
    hext[0:SUBLANES, :] = hext[tm:tm + SUBLANES, :]


def _pad_cols(w, n):
    return jnp.pad(w, ((0, 0), (0, n - w.shape[1])))


def _pad_rows(w, n):
    return jnp.pad(w, ((0, n - w.shape[0]), (0, 0)))


def _rwkv_pre(res, mod, ng, p, v_first):
    bsz, t_len, d = res.shape
    tm = _tile(t_len, RWKV_PRE_TILE)
    has_vres = v_first is not None
    tok = pl.BlockSpec((1, tm, d), lambda b, t: (b, t, 0))
    lora_in = lambda w: _pad_cols(w, LORA_PAD).astype(BF16)
    lora_out = lambda w: _pad_rows(w, LORA_PAD).astype(BF16)
    vec = jnp.stack([p["w0"], p["a0"], p["k_k"], p["k_a"]])
    args = [res, mod, ng, p["mu"], vec,
            p["w_rkv"][0].astype(BF16), p["w_rkv"][1].astype(BF16), p["w_rkv"][2].astype(BF16),
            lora_in(p["w1"]), lora_out(p["w2"]), lora_in(p["a1"]), lora_out(p["a2"]),
            lora_in(p["g1"]), lora_out(p["g2"])]
    in_specs = [tok, pl.BlockSpec((1, 3 * N_SUB, d), lambda b, t: (b, 0, 0))]
    in_specs += [_resident(a.shape) for a in args[2:]]
    if has_vres:
        extra = [lora_in(p["v1"]), lora_out(p["v2"]), p["v0"].reshape(1, d)]
        args += extra + [v_first]
        in_specs += [_resident(a.shape) for a in extra] + [tok]
    outs = pl.pallas_call(
        functools.partial(_rwkv_pre_kernel, has_vres),
        grid=(bsz, t_len // tm),
        in_specs=in_specs,
        out_specs=[tok] * 7,
        out_shape=[jax.ShapeDtypeStruct(res.shape, F32 if i == 1 else BF16) for i in range(7)],
        scratch_shapes=[pltpu.VMEM((tm + SUBLANES, d), F32)],
        compiler_params=_params(2),
        name="rwkv_pre",
    )(*args)
    return outs


def _block_diag(x, bd_mask):
    xb = x.astype(BF16)
    return jnp.where(bd_mask, jnp.concatenate([xb] * RWKV_GROUP, axis=0), jnp.zeros((), BF16))


class _Staged:
    def __init__(self, gen):
        self.gen, self.done, self.value = gen, False, None

    def step(self):
        if not self.done:
            try:
                next(self.gen)
            except StopIteration as stop:
                self.done, self.value = True, stop.value

    def finish(self):
        while not self.done:
            self.step()
        return self.value


def _wkv_prepare(raw, masks, consts):
    bd_mask, strict, incl, eye = masks
    tril = consts
    c = RWKV_CHUNK
    bd = lambda x: _block_diag(x, bd_mask)
    cum = [_mm_split_lhs_exact(tril, lw) for r, lw, k, v, kkn, a in raw]
    yield
    units = []
    for (r, lw, k, v, kkn, a), cum_i in zip(raw, cum):
        g_t = jnp.exp(cum_i)
        g_inv = jnp.exp(-cum_i)
        units.append((r * g_t, -kkn * jnp.exp(cum_i - lw), kkn * a * g_inv, k * g_inv, v, g_t[c - 1:c]))
    ar = [jnp.concatenate([at, rt], axis=0).astype(BF16) for rt, at, bt, kt, v, gl in units]
    sv = [bd(u[4]) for u in units]
    lb = [_mm_nt(x, bd(u[2])) for x, u in zip(ar, units)]
    lk = [_mm_nt(x, bd(u[3])) for x, u in zip(ar, units)]
    yield
    l_ab = [jnp.where(strict, x[:c], 0.0) for x in lb]
    l_rb = [jnp.where(incl, x[c:], 0.0) for x in lb]
    l_ak = [jnp.where(strict, x[:c], 0.0) for x in lk]
    l_rk = [jnp.where(incl, x[c:], 0.0) for x in lk]
    pw = [_mm(x, bd(x)) for x in l_ab]
    inv = [eye + x for x in l_ab]
    yield
    for _ in range(4):
        both = [_mm(jnp.concatenate([p, t], axis=0), bd(p)) for p, t in zip(pw, inv)]
        pw = [x[:c] for x in both]
        inv = [t + x[c:] for t, x in zip(inv, both)]
        yield
    inv = [t + _mm(t, bd(p)) for t, p in zip(inv, pw)]
    kv = [_mm(jnp.concatenate([x, y], axis=0), s) for x, y, s in zip(l_ak, l_rk, sv)]
    ak_v = [x[:c] for x in kv]
    rk_v = [x[c:] for x in kv]
    bk = [jnp.concatenate([bt, kt], axis=0).astype(BF16) for rt, at, bt, kt, v, gl in units]
    return [dict(ar=ar[i], inv=inv[i], l_rb=l_rb[i], ak_v=ak_v[i], rk_v=rk_v[i], bk=bk[i], v=units[i][4],
                 g_last=units[i][5]) for i in range(len(units))]


def _wkv_advance(get_prepared, n_chunks, z_states, masks):
    bd_mask = masks[0]
    c = RWKV_CHUNK
    bd = lambda x: _block_diag(x, bd_mask)
    ys = []
    for i in range(n_chunks):
        prepared = get_prepared(i)
        ars = [_mm_nt(p["ar"], z) for p, z in zip(prepared, z_states)]
        yield
        u = [_mm(p["inv"], bd(x[:c] + p["ak_v"])) for p, x in zip(prepared, ars)]
        yield
        y = [x[c:] + _mm(p["l_rb"], bd(uu)) + p["rk_v"] for p, x, uu in zip(prepared, ars, u)]
        dz = [_mm_tn(jnp.concatenate([uu, p["v"]], axis=0), p["bk"]) for p, uu in zip(prepared, u)]
        z_states = [(z + jnp.where(bd_mask, d, 0.0)) * p["g_last"] for z, d, p in zip(z_states, dz, prepared)]
        ys.append(y)
        yield
    return ys, z_states


def _wkv_masks():
    gw, c = RWKV_GW, RWKV_CHUNK
    row = lax.broadcasted_iota(jnp.int32, (gw, gw), 0)
    col = lax.broadcasted_iota(jnp.int32, (gw, gw), 1)
    bd_mask = (row // RWKV_N) == (col // RWKV_N)
    t = lax.broadcasted_iota(jnp.int32, (c, gw), 0)
    j = lax.broadcasted_iota(jnp.int32, (c, gw), 1) % c
    eye = jnp.where(t == j, 1.0, 0.0).astype(F32)
    rc = lax.broadcasted_iota(jnp.int32, (c, c), 0)
    cc = lax.broadcasted_iota(jnp.int32, (c, c), 1)
    tril = jnp.where(rc >= cc, 1.0, 0.0).astype(BF16)
    bd_ones = jnp.where(bd_mask, 1.0, 0.0).astype(BF16)
    return (bd_mask, t > j, t >= j, eye), tril, bd_ones


def _wkv_kernel(n_chunks, r_ref, lw_ref, k_ref, v_ref, kk_ref, a_ref, g_ref, vec_ref, res_ref, mod_ref, ng_ref, wo_ref,
                out_ref, state, z_buf):
    @pl.when(pl.program_id(1) == 0)
    def _():
        state[...] = jnp.zeros(state.shape, F32)

    c = RWKV_CHUNK
    n_groups = state.shape[0]
    masks, tril, bd_ones = _wkv_masks()
    lanes = [slice(gi * RWKV_GW, (gi + 1) * RWKV_GW) for gi in range(n_groups)]
    tb = n_chunks * c

    def head_sums(per_group):
        s = _mm(jnp.concatenate(per_group, axis=0), bd_ones)
        return [s[gi * tb:(gi + 1) * tb] for gi in range(n_groups)]

    load = lambda ref: [ref[0, :, ln].astype(F32) for ln in lanes]
    r, lw, k, v, kk, a = (load(ref) for ref in (r_ref, lw_ref, k_ref, v_ref, kk_ref, a_ref))
    kkn = [x / jnp.maximum(jnp.sqrt(n2), 1e-12) for x, n2 in zip(kk, head_sums([x * x for x in kk]))]
    bonus = head_sums([r[gi] * k[gi] * vec_ref[0:1, lanes[gi]] for gi in range(n_groups)])

    def raw_units(i):
        rows = slice(i * c, (i + 1) * c)
        return [tuple(x[gi][rows] for x in (r, lw, k, v, kkn, a)) for gi in range(n_groups)]

    lead = min(2, n_chunks)
    first = _Staged(_wkv_prepare([u for i in range(lead) for u in raw_units(i)], masks, tril)).finish()
    later = [_Staged(_wkv_prepare(raw_units(i), masks, tril)) for i in range(lead, n_chunks)]

    def get_prepared(i):
        if i < lead:
            return first[i * n_groups:(i + 1) * n_groups]
        return later[i - lead].finish()

    advance = _Staged(_wkv_advance(get_prepared, n_chunks, [state[gi] for gi in range(n_groups)], masks))
    while not advance.done:
        advance.step()
        for task in later:
            if not task.done:
                task.step()
                break
    ys, z_states = advance.value
    for gi in range(n_groups):
        state[gi] = z_states[gi]
    y = [jnp.concatenate([ys[i][gi] for i in range(n_chunks)], axis=0) for gi in range(n_groups)]
    yc = [x - m * (1.0 / RWKV_N) for x, m in zip(y, head_sums(y))]
    var = head_sums([x * x for x in yc])
    for gi, ln in enumerate(lanes):
        yn = yc[gi] * lax.rsqrt(var[gi] * (1.0 / RWKV_N) + RWKV_GN_EPS) * vec_ref[1:2, ln] + vec_ref[2:3, ln]
        z_buf[:, ln] = ((yn + bonus[gi] * v[gi]) * g_ref[0, :, ln].astype(F32)).astype(BF16)
    y_out = jnp.dot(z_buf[...], wo_ref[...], preferred_element_type=F32)
    out_ref[0] = _sublayer_out(res_ref[0], y_out, mod_ref[0], ng_ref[...], 1, 1.0)


def _wkv_sublayer(r, lw, k, v, kk, a, g, p, res, mod, ng, w_o):
    bsz, t_len, d = r.shape
    n_groups = d // RWKV_GW
    tb = _tile(t_len, 256)
    vec = jnp.stack([p["r_k"].reshape(d), p["ln_w"], p["ln_b"]])
    tok = pl.BlockSpec((1, tb, d), lambda b, t: (b, t, 0))
    return pl.pallas_call(
        functools.partial(_wkv_kernel, tb // RWKV_CHUNK),
        grid=(bsz, t_len // tb),
        in_specs=[tok] * 7 + [_resident(vec.shape), tok, pl.BlockSpec((1, 3 * N_SUB, d), lambda b, t: (b, 0, 0)),
                             _resident(ng.shape), _resident(w_o.shape)],
        out_specs=tok,
        out_shape=jax.ShapeDtypeStruct(res.shape, F32),
        scratch_shapes=[pltpu.VMEM((n_groups, RWKV_GW, RWKV_GW), F32), pltpu.VMEM((tb, d), BF16)],
        compiler_params=_params(2),
        name="wkv7",
    )(r, lw, k, v, kk, a, g, vec, res, mod, ng, w_o)


def _ret_pre_kernel(res_ref, mod_ref, ng_ref, pos_ref, inv_ref, w_ref, q_ref, k_ref, v_ref, g_ref):
    h = _sublayer_in(res_ref[0], mod_ref[0], ng_ref[...], 1).astype(BF16)
    ang = pos_ref[0] * inv_ref[...]
    cos = jnp.cos(ang)
    sin = jnp.sin(ang)
    half = RET_DK // 2
    qk = RET_H * RET_DK
    for which, out_ref, scale in ((0, q_ref, 1.0), (1, k_ref, RET_DK ** -0.5)):
        for hd in range(RET_H):
            lo = which * qk + hd * RET_DK
            x = jnp.dot(h, w_ref[:, lo:lo + RET_DK], preferred_element_type=F32)
            x1, x2 = x[:, :half], x[:, half:]
            out_ref[0, :, hd * RET_DK:hd * RET_DK + half] = ((x1 * cos - x2 * sin) * scale).astype(BF16)
            out_ref[0, :, hd * RET_DK + half:(hd + 1) * RET_DK] = ((x1 * sin + x2 * cos) * scale).astype(BF16)
    nv = RET_H * RET_DV
    for hd in range(RET_H):
        lo = 2 * qk + hd * RET_DV
        v_ref[0, :, hd * RET_DV:(hd + 1) * RET_DV] = jnp.dot(
            h, w_ref[:, lo:lo + RET_DV], preferred_element_type=F32).astype(BF16)
        g_ref[0, :, hd * RET_DV:(hd + 1) * RET_DV] = jnp.dot(
            h, w_ref[:, lo + nv:lo + nv + RET_DV], preferred_element_type=F32).astype(BF16)


def _ret_pre(res, mod, ng, pos_f, w_in):
    bsz, t_len, d = res.shape
    tm = _tile(t_len, 512)
    qk = RET_H * RET_DK
    nv = RET_H * RET_DV
    half = RET_DK // 2
    inv = (1.0 / (ROPE_BASE ** jnp.linspace(0.0, 1.0, half, dtype=F32))).reshape(1, half)
    tok = lambda n: pl.BlockSpec((1, tm, n), lambda b, t: (b, t, 0))
    return pl.pallas_call(
        _ret_pre_kernel,
        grid=(bsz, t_len // tm),
        in_specs=[tok(d), pl.BlockSpec((1, 3 * N_SUB, d), lambda b, t: (b, 0, 0)), _resident(ng.shape),
                  tok(1), _resident((1, half)), _resident(w_in.shape)],
        out_specs=[tok(qk), tok(qk), tok(nv), tok(nv)],
        out_shape=[jax.ShapeDtypeStruct((bsz, t_len, qk), BF16), jax.ShapeDtypeStruct((bsz, t_len, qk), BF16),
                   jax.ShapeDtypeStruct((bsz, t_len, nv), BF16), jax.ShapeDtypeStruct((bsz, t_len, nv), BF16)],
        compiler_params=_params(2),
        name="ret_pre",
    )(res, mod, ng, pos_f, inv, w_in)


def _ret_mix_kernel(n_chunks, q_ref, k_ref, v_ref, g_ref, res_ref, mod_ref, ng_ref, wo_ref, out_ref, state, z_buf):
    @pl.when(pl.program_id(1) == 0)
    def _():
        state[...] = jnp.zeros(state.shape, F32)

    c = RET_CHUNK
    row = lax.broadcasted_iota(jnp.int32, (c, c), 0)
    col = lax.broadcasted_iota(jnp.int32, (c, c), 1)
    diff = (row - col).astype(F32)
    idx = lax.broadcasted_iota(jnp.int32, (c, 1), 0).astype(F32)
    for hd in range(RET_H):
        log_gamma = math.log(1.0 - 2.0 ** (-5.0 - hd))
        inner = jnp.where(diff >= 0, jnp.exp(log_gamma * jnp.maximum(diff, 0.0)), 0.0)
        q_decay = jnp.exp(log_gamma * (idx + 1.0))
        k_decay = jnp.exp(log_gamma * (c - 1.0 - idx))
        chunk_decay = math.exp(log_gamma * c)
        for i in range(n_chunks):
            rows = slice(i * c, (i + 1) * c)
            q_c = q_ref[0, rows, hd * RET_DK:(hd + 1) * RET_DK]
            k_c = k_ref[0, rows, hd * RET_DK:(hd + 1) * RET_DK]
            v_c = v_ref[0, rows, hd * RET_DV:(hd + 1) * RET_DV]
            r_state = state[hd]
            s = _mm_nt(q_c, k_c) * inner
            o = _mm(s, v_c) + _mm(q_c, r_state) * q_decay
            state[hd] = r_state * chunk_decay + _mm_tn(k_c.astype(F32) * k_decay, v_c)
            oc = o - jnp.mean(o, axis=-1, keepdims=True)
            on = oc * lax.rsqrt(jnp.mean(oc * oc, axis=-1, keepdims=True) + HEAD_NORM_EPS)
            gt = g_ref[0, rows, hd * RET_DV:(hd + 1) * RET_DV].astype(F32)
            z_buf[rows, hd * RET_DV:(hd + 1) * RET_DV] = (gt * _sigmoid(gt) * on).astype(BF16)
    y = jnp.dot(z_buf[...], wo_ref[...], preferred_element_type=F32)
    out_ref[0] = _sublayer_out(res_ref[0], y, mod_ref[0], ng_ref[...], 1, 1.0)


def _ret_mix(q, k, v, g, res, mod, ng, w_o):
    bsz, t_len, d = res.shape
    tm = _tile(t_len, 512)
    qk = RET_H * RET_DK
    nv = RET_H * RET_DV
    tok = lambda n: pl.BlockSpec((1, tm, n), lambda b, t: (b, t, 0))
    return pl.pallas_call(
        functools.partial(_ret_mix_kernel, tm // RET_CHUNK),
        grid=(bsz, t_len // tm),
        in_specs=[tok(qk), tok(qk), tok(nv), tok(nv), tok(d),
                  pl.BlockSpec((1, 3 * N_SUB, d), lambda b, t: (b, 0, 0)), _resident(ng.shape), _resident(w_o.shape)],
        out_specs=tok(d),
        out_shape=jax.ShapeDtypeStruct(res.shape, F32),
        scratch_shapes=[pltpu.VMEM((RET_H, RET_DK, RET_DV), F32), pltpu.VMEM((tm, nv), BF16)],
        compiler_params=_params(2),
        name="ret_mix",
    )(q, k, v, g, res, mod, ng, w_o)


def _lru_kernel(res_ref, mod_ref, ng_ref, win_ref, cw_ref, vec_ref, gw_ref, wo_ref, out_ref, xext, h_carry):
    tm = res_ref.shape[1]
    width = xext.shape[1]

    @pl.when(pl.program_id(1) == 0)
    def _():
        xext[0:SUBLANES, :] = jnp.zeros((SUBLANES, width), F32)
        h_carry[...] = jnp.zeros(h_carry.shape, F32)

    mod = mod_ref[0]
    ng = ng_ref[...]
    cw = cw_ref[...]
    vec = vec_ref[...]
    conv_b, gate_bi, gate_br, lam = vec[0:1], vec[1:2], vec[2:3], vec[3:4]
    neg_lam = -lam
    softplus = jnp.maximum(neg_lam, 0.0) + jnp.log1p(jnp.exp(-jnp.abs(neg_lam)))
    parts = [slice(i * tm // LRU_ROW_PARTS, (i + 1) * tm // LRU_ROW_PARTS) for i in range(LRU_ROW_PARTS)]
    res = [res_ref[0, rows, :] for rows in parts]
    h = [_sublayer_in(x, mod, ng, 1).astype(BF16) for x in res]
    gate_branch = [jnp.dot(x, win_ref[:, :width], preferred_element_type=F32) for x in h]
    xb = [jnp.dot(x, win_ref[:, width:], preferred_element_type=F32) for x in h]
    for rows, x in zip(parts, xb):
        xext[SUBLANES + rows.start:SUBLANES + rows.stop, :] = x
    xc = []
    for rows, x in zip(parts, xb):
        acc = conv_b + cw[CONV_W - 1:CONV_W] * x
        for j in range(CONV_W - 1):
            lo = SUBLANES - (CONV_W - 1) + j
            acc = acc + cw[j:j + 1] * xext[lo + rows.start:lo + rows.stop, :]
        xc.append(acc)
    xext[0:SUBLANES, :] = xext[tm:tm + SUBLANES, :]

    a, u = [], []
    for x in xc:
        xcb = x.astype(BF16)
        gates = [jnp.concatenate([jnp.dot(xcb[:, hd * LRU_BW:(hd + 1) * LRU_BW], gw_ref[gi * LRU_H + hd],
                                          preferred_element_type=F32) for hd in range(LRU_H)], axis=1)
                 for gi in range(2)]
        i_gate = _sigmoid(gates[0] + gate_bi)
        r_gate = _sigmoid(gates[1] + gate_br)
        log_a = -LRU_C * r_gate * softplus
        a_p = jnp.exp(log_a)
        a.append(a_p)
        u.append(jnp.sqrt(-jnp.tanh(log_a) * (a_p * a_p + 1.0)) * (i_gate * x))

    row = lax.broadcasted_iota(jnp.int32, (SUBLANES, width), 0)
    carry = h_carry[...]
    hs = []
    for a_p, u_p in zip(a, u):
        groups = []
        for i in range(a_p.shape[0] // SUBLANES):
            ag = a_p[i * SUBLANES:(i + 1) * SUBLANES]
            ug = u_p[i * SUBLANES:(i + 1) * SUBLANES]
            for d in (1, 2, 4):
                keep = row >= d
                u_prev = jnp.where(keep, pltpu.roll(ug, d, 0), 0.0)
                a_prev = jnp.where(keep, pltpu.roll(ag, d, 0), 1.0)
                ug = ug + ag * u_prev
                ag = ag * a_prev
            hg = ug + ag * carry
            groups.append(hg)
            carry = jnp.broadcast_to(hg[SUBLANES - 1:SUBLANES, :], (SUBLANES, width))
        hs.append(jnp.concatenate(groups, axis=0))
    h_carry[...] = carry

    for rows, x, gb, hp in zip(parts, res, gate_branch, hs):
        gelu = 0.5 * gb * (1.0 + jnp.tanh(math.sqrt(2.0 / math.pi) * (gb + 0.044715 * (gb * gb * gb))))
        y = jnp.dot((gelu * hp).astype(BF16), wo_ref[...], preferred_element_type=F32)
        out_ref[0, rows, :] = _sublayer_out(x, y, mod, ng, 1, 1.0)


def _lru_sublayer(res, mod, ng, w_in, conv_w, vec, gate_w, w_o):
    bsz, t_len, d = res.shape
    width = w_o.shape[0]
    tm = _tile(t_len, LRU_TILE)
    tok = pl.BlockSpec((1, tm, d), lambda b, t: (b, t, 0))
    return pl.pallas_call(
        _lru_kernel,
        grid=(bsz, t_len // tm),
        in_specs=[tok, pl.BlockSpec((1, 3 * N_SUB, d), lambda b, t: (b, 0, 0)), _resident(ng.shape),
                  _resident(w_in.shape), _resident(conv_w.shape), _resident(vec.shape),
                  _resident(gate_w.shape), _resident(w_o.shape)],
        out_specs=tok,
        out_shape=jax.ShapeDtypeStruct(res.shape, F32),
        scratch_shapes=[pltpu.VMEM((tm + SUBLANES, width), F32), pltpu.VMEM((SUBLANES, width), F32)],
        compiler_params=_params(2),
        name="lru_sublayer",
    )(res, mod, ng, w_in, conv_w, vec, gate_w, w_o)


def kernel(x, c, positions, ada_w, ada_b, norm_g, ffn_w_in, ffn_w_out, rwkv_mu, rwkv_w_rkv, rwkv_w0, rwkv_w1, rwkv_w2, rwkv_a0, rwkv_a1, rwkv_a2, rwkv_g1, rwkv_g2, rwkv_k_k, rwkv_k_a, rwkv_r_k, rwkv_ln_w, rwkv_ln_b, rwkv_w_o, rwkv_v0, rwkv_v1, rwkv_v2, ret_w_in, ret_w_o, lru_w_in, lru_conv_w, lru_conv_b, lru_gate_w, lru_gate_b, lru_lambda, lru_w_o):
    depth = ada_w.shape[0]
    d_ff = ffn_w_out.shape[2]
    res = x.astype(F32)
    mod = _ada_mod(c.astype(F32), ada_w, ada_b)
    pos_f = positions.astype(F32)[..., None]
    v_first = None
    for i in range(depth):
        ng = norm_g[i]
        ffn = lambda res, m, s: _ffn_sublayer(
            res, mod[i], ng, ffn_w_in[i, m, :, :d_ff].astype(BF16), ffn_w_in[i, m, :, d_ff:].astype(BF16),
            ffn_w_out[i, m].astype(BF16), s)
        res = ffn(res, 0, 0)
        kind, j = i % 3, i // 3
        if kind == 0:
            p = dict(mu=rwkv_mu[j], w_rkv=rwkv_w_rkv[j], w0=rwkv_w0[j], w1=rwkv_w1[j], w2=rwkv_w2[j],
                     a0=rwkv_a0[j], a1=rwkv_a1[j], a2=rwkv_a2[j], g1=rwkv_g1[j], g2=rwkv_g2[j],
                     k_k=rwkv_k_k[j], k_a=rwkv_k_a[j], r_k=rwkv_r_k[j], ln_w=rwkv_ln_w[j], ln_b=rwkv_ln_b[j])
            if j > 0:
                p.update(v0=rwkv_v0[j - 1], v1=rwkv_v1[j - 1], v2=rwkv_v2[j - 1])
            r, lw, k, v, kk, a, g = _rwkv_pre(res, mod[i], ng, p, v_first if j > 0 else None)
            if j == 0:
                v_first = v
            res = _wkv_sublayer(r, lw, k, v, kk, a, g, p, res, mod[i], ng, rwkv_w_o[j].astype(BF16))
        elif kind == 1:
            q, k, v, g = _ret_pre(res, mod[i], ng, pos_f, ret_w_in[j].astype(BF16))
            res = _ret_mix(q, k, v, g, res, mod[i], ng, ret_w_o[j].astype(BF16))
        else:
            width = lru_w_o.shape[1]
            vec = jnp.stack([lru_conv_b[j], lru_gate_b[j, 0].reshape(width), lru_gate_b[j, 1].reshape(width),
                             lru_lambda[j]])
            gate_w = lru_gate_w[j].reshape(2 * LRU_H, LRU_BW, LRU_BW).astype(BF16)
            res = _lru_sublayer(res, mod[i], ng, lru_w_in[j].astype(BF16), lru_conv_w[j], vec, gate_w,
                                lru_w_o[j].astype(BF16))
        res = ffn(res, 1, 2)
    return res.astype(x.dtype)
```

```python
import functools
import math

import jax
import jax.numpy as jnp
from jax import lax
from jax.experimental import pallas as pl
from jax.experimental.pallas import tpu as pltpu

F32 = jnp.float32
BF16 = jnp.bfloat16

NORM_EPS = 1e-6
N_SUB = 3

RWKV_N = 64
RWKV_GROUP = 4
RWKV_GW = RWKV_GROUP * RWKV_N
RWKV_CHUNK = 64
RWKV_GN_EPS = 64e-5
LORA_PAD = 128

RET_H = 4
RET_DK = 256
RET_DV = 512
RET_CHUNK = 128
ROPE_BASE = 10000.0
HEAD_NORM_EPS = 1e-5

LRU_H = 5
LRU_BW = 256
CONV_W = 4
LRU_C = 8.0
SUBLANES = 8

VMEM_LIMIT = 56 * 1024 * 1024
FFN_TILE = 1024
FFN_ROW_PARTS = 4
RWKV_PRE_TILE = 512
RWKV_PRE_PARTS = 2
LRU_TILE = 256
LRU_ROW_PARTS = 2

NT_DIMS = (((1,), (1,)), ((), ()))
TN_DIMS = (((0,), (0,)), ((), ()))


def _mm(a, b):
    return jnp.dot(a.astype(BF16), b.astype(BF16), preferred_element_type=F32)


def _mm_nt(a, b):
    return lax.dot_general(a.astype(BF16), b.astype(BF16), NT_DIMS, preferred_element_type=F32)


def _mm_tn(a, b):
    return lax.dot_general(a.astype(BF16), b.astype(BF16), TN_DIMS, preferred_element_type=F32)


def _mm_split_lhs_exact(w, x):
    hi = x.astype(BF16)
    lo = (x - hi.astype(F32)).astype(BF16)
    return (jnp.dot(w, hi, preferred_element_type=F32) + jnp.dot(w, lo, preferred_element_type=F32))


def _rms(x, g):
    return x * lax.rsqrt(jnp.mean(x * x, axis=-1, keepdims=True) + NORM_EPS) * g


def _sublayer_in(res, mod, ng, s):
    return _rms(res, ng[2 * s:2 * s + 1]) * (1.0 + mod[3 * s + 1:3 * s + 2]) + mod[3 * s:3 * s + 1]


def _sublayer_out(res, y, mod, ng, s, weight):
    return res + weight * mod[3 * s + 2:3 * s + 3] * _rms(y, ng[2 * s + 1:2 * s + 2])


def _sigmoid(x):
    return jax.nn.sigmoid(x)


def _resident(shape):
    nd = len(shape)
    return pl.BlockSpec(shape, lambda *_: (0,) * nd, pipeline_mode=pl.Buffered(1))


def _params(n_axes):
    return pltpu.CompilerParams(dimension_semantics=("arbitrary",) * n_axes, vmem_limit_bytes=VMEM_LIMIT)


def _tile(t_len, want):
    tm = min(want, t_len)
    assert t_len % tm == 0, (t_len, tm)
    return tm


def _ada_kernel(c_ref, w_ref, b_ref, o_ref):
    c = c_ref[...]
    cond = c * _sigmoid(c)
    o_ref[0] = _mm(cond, w_ref[0]) + b_ref[0]


def _ada_mod(c, ada_w, ada_b):
    depth, d, n = ada_w.shape
    bsz = c.shape[0]
    tn = 1536
    assert n % tn == 0
    out = pl.pallas_call(
        _ada_kernel,
        grid=(depth, n // tn),
        in_specs=[pl.BlockSpec((bsz, d), lambda l, j: (0, 0)),
                  pl.BlockSpec((1, d, tn), lambda l, j: (l, 0, j)),
                  pl.BlockSpec((1, 1, tn), lambda l, j: (l, 0, j))],
        out_specs=pl.BlockSpec((1, bsz, tn), lambda l, j: (l, 0, j)),
        out_shape=jax.ShapeDtypeStruct((depth, bsz, n), F32),
        compiler_params=_params(2),
        name="ada_mod",
    )(c, ada_w, ada_b.reshape(depth, 1, n))
    return out.reshape(depth, bsz, 3 * N_SUB, d)


def _ffn_kernel(s, n_chunks, res_ref, mod_ref, ng_ref, wa_ref, wb_ref, wo_ref, out_ref):
    mod = mod_ref[0]
    ng = ng_ref[...]
    f = wa_ref.shape[1]
    tf = f // n_chunks
    tm = res_ref.shape[1]
    halves = [slice(i * tm // FFN_ROW_PARTS, (i + 1) * tm // FFN_ROW_PARTS) for i in range(FFN_ROW_PARTS)]
    res = [res_ref[0, rows, :] for rows in halves]
    h = [_sublayer_in(x, mod, ng, s).astype(BF16) for x in res]
    y = [jnp.zeros(x.shape, F32) for x in res]
    for c in range(n_chunks):
        for i in range(len(halves)):
            a = jnp.dot(h[i], wa_ref[:, c * tf:(c + 1) * tf], preferred_element_type=F32)
            b = jnp.dot(h[i], wb_ref[:, c * tf:(c + 1) * tf], preferred_element_type=F32)
            z = (a * _sigmoid(a) * b).astype(BF16)
            y[i] = y[i] + jnp.dot(z, wo_ref[c * tf:(c + 1) * tf, :], preferred_element_type=F32)
    for i, rows in enumerate(halves):
        out_ref[0, rows, :] = _sublayer_out(res[i], y[i], mod, ng, s, 0.5)


def _ffn_sublayer(res, mod, ng, w_a, w_b, w_o, s):
    bsz, t_len, d = res.shape
    f = w_a.shape[1]
    tm = _tile(t_len, FFN_TILE)
    return pl.pallas_call(
        functools.partial(_ffn_kernel, s, 2),
        grid=(bsz, t_len // tm),
        in_specs=[pl.BlockSpec((1, tm, d), lambda b, t: (b, t, 0)),
                  pl.BlockSpec((1, 3 * N_SUB, d), lambda b, t: (b, 0, 0)),
                  _resident(ng.shape), _resident((d, f)), _resident((d, f)), _resident((f, d))],
        out_specs=pl.BlockSpec((1, tm, d), lambda b, t: (b, t, 0)),
        out_shape=jax.ShapeDtypeStruct(res.shape, F32),
        compiler_params=_params(2),
        name="ffn_sublayer",
    )(res, mod, ng, w_a, w_b, w_o)


RWKV_STREAMS = ("r", "lw", "k", "v", "kk", "a", "g")


def _rwkv_pre_stages(has_vres, res, mod, ng_ref, mu_ref, vec_ref, w, vfirst, hext, store):
    tm, d = res.shape
    h = _sublayer_in(res, mod, ng_ref[...], 1)
    hext[SUBLANES:SUBLANES + tm, :] = h
    dx = hext[SUBLANES - 1:SUBLANES - 1 + tm, :] - h
    hext[0:SUBLANES, :] = hext[tm:tm + SUBLANES, :]
    mu = mu_ref[...]
    xr, xw, xk, xv, xa, xg = ((h + dx * mu[j:j + 1]).astype(BF16) for j in range(6))
    vec = vec_ref[...]
    w0, a0, k_k, k_a = vec[0:1], vec[1:2], vec[2:3], vec[3:4]
    full = slice(0, d)
    yield
    tw = jnp.tanh(jnp.dot(xw, w["w1"][...], preferred_element_type=F32))
    ta = jnp.dot(xa, w["a1"][...], preferred_element_type=F32)
    yield
    tg = _sigmoid(jnp.dot(xg, w["g1"][...], preferred_element_type=F32))
    if has_vres:
        tv = jnp.dot(xv, w["v1"][...], preferred_element_type=F32)
    yield
    store("lw", full, -math.exp(-0.5) * _sigmoid(w0 + _mm(tw, w["w2"][...])))
    a = _sigmoid(a0 + _mm(ta, w["a2"][...]))
    store("a", full, a)
    yield
    store("g", full, _mm(tg, w["g2"][...]))
    if has_vres:
        mix = _sigmoid(w["v0"][...] + _mm(tv, w["v2"][...]))
    yield
    for lo in range(0, d, RWKV_GW):
        cols = slice(lo, lo + RWKV_GW)
        store("r", cols, jnp.dot(xr, w["wr"][:, cols], preferred_element_type=F32))
        yield
    for lo in range(0, d, RWKV_GW):
        cols = slice(lo, lo + RWKV_GW)
        k = jnp.dot(xk, w["wk"][:, cols], preferred_element_type=F32)
        store("kk", cols, k * k_k[:, cols])
        store("k", cols, k * (1.0 + (a[:, cols] - 1.0) * k_a[:, cols]))
        yield
    for lo in range(0, d, RWKV_GW):
        cols = slice(lo, lo + RWKV_GW)
        v = jnp.dot(xv, w["wv"][:, cols], preferred_element_type=F32)
        if has_vres:
            v = v + (vfirst[:, cols].astype(F32) - v) * mix[:, cols]
        store("v", cols, v)
        yield


def _pad_cols(w, n):
    return jnp.pad(w, ((0, 0), (0, n - w.shape[1])))


def _pad_rows(w, n):
    return jnp.pad(w, ((0, n - w.shape[0]), (0, 0)))


def _block_diag(x, bd_mask):
    xb = x.astype(BF16)
    return jnp.where(bd_mask, jnp.concatenate([xb] * RWKV_GROUP, axis=0), jnp.zeros((), BF16))


class _Staged:
    def __init__(self, gen):
        self.gen, self.done, self.value = gen, False, None

    def step(self):
        if not self.done:
            try:
                next(self.gen)
            except StopIteration as stop:
                self.done, self.value = True, stop.value

    def finish(self):
        while not self.done:
            self.step()
        return self.value


def _wkv_prepare(raw, masks, consts):
    bd_mask, strict, incl, eye = masks
    tril = consts
    c = RWKV_CHUNK
    bd = lambda x: _block_diag(x, bd_mask)
    cum = [_mm_split_lhs_exact(tril, lw) for r, lw, k, v, kkn, a in raw]
    yield
    units = []
    for (r, lw, k, v, kkn, a), cum_i in zip(raw, cum):
        g_t = jnp.exp(cum_i)
        g_inv = jnp.exp(-cum_i)
        units.append((r * g_t, -kkn * jnp.exp(cum_i - lw), kkn * a * g_inv, k * g_inv, v, g_t[c - 1:c]))
    ar = [jnp.concatenate([at, rt], axis=0).astype(BF16) for rt, at, bt, kt, v, gl in units]
    sv = [bd(u[4]) for u in units]
    lb = [_mm_nt(x, bd(u[2])) for x, u in zip(ar, units)]
    lk = [_mm_nt(x, bd(u[3])) for x, u in zip(ar, units)]
    yield
    l_ab = [jnp.where(strict, x[:c], 0.0) for x in lb]
    l_rb = [jnp.where(incl, x[c:], 0.0) for x in lb]
    l_ak = [jnp.where(strict, x[:c], 0.0) for x in lk]
    l_rk = [jnp.where(incl, x[c:], 0.0) for x in lk]
    pw = [_mm(x, bd(x)) for x in l_ab]
    inv = [eye + x for x in l_ab]
    yield
    for _ in range(4):
        both = [_mm(jnp.concatenate([p, t], axis=0), bd(p)) for p, t in zip(pw, inv)]
        pw = [x[:c] for x in both]
        inv = [t + x[c:] for t, x in zip(inv, both)]
        yield
    inv = [t + _mm(t, bd(p)) for t, p in zip(inv, pw)]
    kv = [_mm(jnp.concatenate([x, y], axis=0), s) for x, y, s in zip(l_ak, l_rk, sv)]
    ak_v = [x[:c] for x in kv]
    rk_v = [x[c:] for x in kv]
    bk = [jnp.concatenate([bt, kt], axis=0).astype(BF16) for rt, at, bt, kt, v, gl in units]
    return [dict(ar=ar[i], inv=inv[i], l_rb=l_rb[i], ak_v=ak_v[i], rk_v=rk_v[i], bk=bk[i], v=units[i][4],
                 g_last=units[i][5]) for i in range(len(units))]


def _wkv_advance(get_prepared, n_chunks, z_states, masks):
    bd_mask = masks[0]
    c = RWKV_CHUNK
    bd = lambda x: _block_diag(x, bd_mask)
    ys = []
    for i in range(n_chunks):
        prepared = get_prepared(i)
        ars = [_mm_nt(p["ar"], z) for p, z in zip(prepared, z_states)]
        yield
        u = [_mm(p["inv"], bd(x[:c] + p["ak_v"])) for p, x in zip(prepared, ars)]
        yield
        y = [x[c:] + _mm(p["l_rb"], bd(uu)) + p["rk_v"] for p, x, uu in zip(prepared, ars, u)]
        dz = [_mm_tn(jnp.concatenate([uu, p["v"]], axis=0), p["bk"]) for p, uu in zip(prepared, u)]
        z_states = [(z + jnp.where(bd_mask, d, 0.0)) * p["g_last"] for z, d, p in zip(z_states, dz, prepared)]
        ys.append(y)
        yield
    return ys, z_states


def _wkv_masks():
    gw, c = RWKV_GW, RWKV_CHUNK
    row = lax.broadcasted_iota(jnp.int32, (gw, gw), 0)
    col = lax.broadcasted_iota(jnp.int32, (gw, gw), 1)
    bd_mask = (row // RWKV_N) == (col // RWKV_N)
    t = lax.broadcasted_iota(jnp.int32, (c, gw), 0)
    j = lax.broadcasted_iota(jnp.int32, (c, gw), 1) % c
    eye = jnp.where(t == j, 1.0, 0.0).astype(F32)
    rc = lax.broadcasted_iota(jnp.int32, (c, c), 0)
    cc = lax.broadcasted_iota(jnp.int32, (c, c), 1)
    tril = jnp.where(rc >= cc, 1.0, 0.0).astype(BF16)
    bd_ones = jnp.where(bd_mask, 1.0, 0.0).astype(BF16)
    return (bd_mask, t > j, t >= j, eye), tril, bd_ones


def _rwkv_kernel(has_vres, emit_vfirst, n_chunks, tiles_per_row, n_tiles, *refs):
    names = ["res_pre", "mod_pre", "res_wkv", "mod_wkv", "ng", "mu", "vec4", "wr", "wk", "wv", "w1", "w2", "a1", "a2",
             "g1", "g2"] + (["v1", "v2", "v0", "vfirst"] if has_vres else []) + ["vec3", "wo", "out"]
    names += (["vfirst_out"] if emit_vfirst else []) + ["hext", "state", "z_buf"] + ["p_" + n for n in RWKV_STREAMS]
    ref = dict(zip(names, refs))
    assert len(names) == len(refs)
    step = pl.program_id(0)
    slot_w = step % 2
    slot_r = 1 - slot_w
    pre_tile = jnp.minimum(step, n_tiles - 1)
    wkv_tile = jnp.maximum(step - 1, 0)
    hext, state, z_buf = ref["hext"], ref["state"], ref["z_buf"]

    @pl.when(step == 0)
    def _():
        for n in RWKV_STREAMS:
            ref["p_" + n][1] = jnp.zeros(ref["p_" + n].shape[1:], ref["p_" + n].dtype)

    @pl.when(pre_tile % tiles_per_row == 0)
    def _():
        hext[0:SUBLANES, :] = jnp.zeros((SUBLANES, hext.shape[1]), F32)

    @pl.when(wkv_tile % tiles_per_row == 0)
    def _():
        state[...] = jnp.zeros(state.shape, F32)

    def store(name, cols, value):
        dst = ref["p_" + name]
        dst[slot_w, :, cols] = value.astype(dst.dtype)

    pre = _Staged(_rwkv_pre_stages(has_vres, ref["res_pre"][0], ref["mod_pre"][0], ref["ng"], ref["mu"], ref["vec4"],
                                   ref, ref["vfirst"][0] if has_vres else None, hext, store))

    c = RWKV_CHUNK
    n_groups = state.shape[0]
    masks, tril, bd_ones = _wkv_masks()
    lanes = [slice(gi * RWKV_GW, (gi + 1) * RWKV_GW) for gi in range(n_groups)]
    tb = n_chunks * c
    vec3 = ref["vec3"]

    def head_sums(per_group):
        s = _mm(jnp.concatenate(per_group, axis=0), bd_ones)
        return [s[gi * tb:(gi + 1) * tb] for gi in range(n_groups)]

    load = lambda n: [ref["p_" + n][slot_r, :, ln].astype(F32) for ln in lanes]
    r, lw, k, v, kk, a = (load(n) for n in RWKV_STREAMS[:6])
    if emit_vfirst:
        ref["vfirst_out"][0] = ref["p_v"][slot_r]
    kkn = [x / jnp.maximum(jnp.sqrt(n2), 1e-12) for x, n2 in zip(kk, head_sums([x * x for x in kk]))]
    bonus = head_sums([r[gi] * k[gi] * vec3[0:1, lanes[gi]] for gi in range(n_groups)])
    pre.step()

    def raw_units(i):
        rows = slice(i * c, (i + 1) * c)
        return [tuple(x[gi][rows] for x in (r, lw, k, v, kkn, a)) for gi in range(n_groups)]

    lead = min(2, n_chunks)
    first_task = _Staged(_wkv_prepare([u for i in range(lead) for u in raw_units(i)], masks, tril))
    while not first_task.done:
        first_task.step()
        pre.step()
    first = first_task.value
    later = [_Staged(_wkv_prepare(raw_units(i), masks, tril)) for i in range(lead, n_chunks)]

    def get_prepared(i):
        if i < lead:
            return first[i * n_groups:(i + 1) * n_groups]
        return later[i - lead].finish()

    advance = _Staged(_wkv_advance(get_prepared, n_chunks, [state[gi] for gi in range(n_groups)], masks))
    while not advance.done:
        advance.step()
        for task in later:
            if not task.done:
                task.step()
                break
        pre.step()
    ys, z_states = advance.value
    for gi in range(n_groups):
        state[gi] = z_states[gi]
    y = [jnp.concatenate([ys[i][gi] for i in range(n_chunks)], axis=0) for gi in range(n_groups)]
    yc = [x - m * (1.0 / RWKV_N) for x, m in zip(y, head_sums(y))]
    pre.step()
    var = head_sums([x * x for x in yc])
    pre.step()
    for gi, ln in enumerate(lanes):
        yn = yc[gi] * lax.rsqrt(var[gi] * (1.0 / RWKV_N) + RWKV_GN_EPS) * vec3[1:2, ln] + vec3[2:3, ln]
        z_buf[:, ln] = ((yn + bonus[gi] * v[gi]) * ref["p_g"][slot_r, :, ln].astype(F32)).astype(BF16)
    pre.finish()
    y_out = jnp.dot(z_buf[...], ref["wo"][...], preferred_element_type=F32)
    ref["out"][0] = _sublayer_out(ref["res_wkv"][0], y_out, ref["mod_wkv"][0], ref["ng"][...], 1, 1.0)


def _rwkv_sublayer(res, mod, ng, p, v_first, emit_vfirst):
    bsz, t_len, d = res.shape
    n_groups = d // RWKV_GW
    tb = _tile(t_len, 256)
    per_row = t_len // tb
    n_tiles = bsz * per_row
    has_vres = v_first is not None
    pre = lambda s: jnp.minimum(s, n_tiles - 1)
    wkv = lambda s: jnp.maximum(s - 1, 0)
    tok_pre = pl.BlockSpec((1, tb, d), lambda s: (pre(s) // per_row, pre(s) % per_row, 0))
    tok_wkv = pl.BlockSpec((1, tb, d), lambda s: (wkv(s) // per_row, wkv(s) % per_row, 0))
    mod_pre = pl.BlockSpec((1, 3 * N_SUB, d), lambda s: (pre(s) // per_row, 0, 0))
    mod_wkv = pl.BlockSpec((1, 3 * N_SUB, d), lambda s: (wkv(s) // per_row, 0, 0))
    lora_in = lambda w: _pad_cols(w, LORA_PAD).astype(BF16)
    lora_out = lambda w: _pad_rows(w, LORA_PAD).astype(BF16)
    vec4 = jnp.stack([p["w0"], p["a0"], p["k_k"], p["k_a"]])
    vec3 = jnp.stack([p["r_k"].reshape(d), p["ln_w"], p["ln_b"]])
    weights = [ng, p["mu"], vec4, p["w_rkv"][0].astype(BF16), p["w_rkv"][1].astype(BF16), p["w_rkv"][2].astype(BF16),
               lora_in(p["w1"]), lora_out(p["w2"]), lora_in(p["a1"]), lora_out(p["a2"]),
               lora_in(p["g1"]), lora_out(p["g2"])]
    args = [res, mod, res, mod] + weights
    in_specs = [tok_pre, mod_pre, tok_wkv, mod_wkv] + [_resident(x.shape) for x in weights]
    if has_vres:
        extra = [lora_in(p["v1"]), lora_out(p["v2"]), p["v0"].reshape(1, d)]
        args += extra + [v_first]
        in_specs += [_resident(x.shape) for x in extra] + [tok_pre]
    tail = [vec3, p["w_o"].astype(BF16)]
    args += tail
    in_specs += [_resident(x.shape) for x in tail]
    out_shape = [jax.ShapeDtypeStruct(res.shape, F32)]
    out_specs = [tok_wkv]
    if emit_vfirst:
        out_shape.append(jax.ShapeDtypeStruct(res.shape, BF16))
        out_specs.append(tok_wkv)
    streams = [pltpu.VMEM((2, tb, d), F32 if n == "lw" else BF16) for n in RWKV_STREAMS]
    outs = pl.pallas_call(
        functools.partial(_rwkv_kernel, has_vres, emit_vfirst, tb // RWKV_CHUNK, per_row, n_tiles),
        grid=(n_tiles + 1,),
        in_specs=in_specs,
        out_specs=out_specs,
        out_shape=out_shape,
        scratch_shapes=[pltpu.VMEM((tb + SUBLANES, d), F32), pltpu.VMEM((n_groups, RWKV_GW, RWKV_GW), F32),
                        pltpu.VMEM((tb, d), BF16)] + streams,
        compiler_params=_params(1),
        name="rwkv_sublayer",
    )(*args)
    return outs[0], (outs[1] if emit_vfirst else None)


def _ret_pre_kernel(res_ref, mod_ref, ng_ref, pos_ref, inv_ref, w_ref, q_ref, k_ref, v_ref, g_ref):
    h = _sublayer_in(res_ref[0], mod_ref[0], ng_ref[...], 1).astype(BF16)
    ang = pos_ref[0] * inv_ref[...]
    cos = jnp.cos(ang)
    sin = jnp.sin(ang)
    half = RET_DK // 2
    qk = RET_H * RET_DK
    for which, out_ref, scale in ((0, q_ref, 1.0), (1, k_ref, RET_DK ** -0.5)):
        for hd in range(RET_H):
            lo = which * qk + hd * RET_DK
            x = jnp.dot(h, w_ref[:, lo:lo + RET_DK], preferred_element_type=F32)
            x1, x2 = x[:, :half], x[:, half:]
            out_ref[0, :, hd * RET_DK:hd * RET_DK + half] = ((x1 * cos - x2 * sin) * scale).astype(BF16)
            out_ref[0, :, hd * RET_DK + half:(hd + 1) * RET_DK] = ((x1 * sin + x2 * cos) * scale).astype(BF16)
    nv = RET_H * RET_DV
    for hd in range(RET_H):
        lo = 2 * qk + hd * RET_DV
        v_ref[0, :, hd * RET_DV:(hd + 1) * RET_DV] = jnp.dot(
            h, w_ref[:, lo:lo + RET_DV], preferred_element_type=F32).astype(BF16)
        g_ref[0, :, hd * RET_DV:(hd + 1) * RET_DV] = jnp.dot(
            h, w_ref[:, lo + nv:lo + nv + RET_DV], preferred_element_type=F32).astype(BF16)


def _ret_pre(res, mod, ng, pos_f, w_in):
    bsz, t_len, d = res.shape
    tm = _tile(t_len, 512)
    qk = RET_H * RET_DK
    nv = RET_H * RET_DV
    half = RET_DK // 2
    inv = (1.0 / (ROPE_BASE ** jnp.linspace(0.0, 1.0, half, dtype=F32))).reshape(1, half)
    tok = lambda n: pl.BlockSpec((1, tm, n), lambda b, t: (b, t, 0))
    return pl.pallas_call(
        _ret_pre_kernel,
        grid=(bsz, t_len // tm),
        in_specs=[tok(d), pl.BlockSpec((1, 3 * N_SUB, d), lambda b, t: (b, 0, 0)), _resident(ng.shape),
                  tok(1), _resident((1, half)), _resident(w_in.shape)],
        out_specs=[tok(qk), tok(qk), tok(nv), tok(nv)],
        out_shape=[jax.ShapeDtypeStruct((bsz, t_len, qk), BF16), jax.ShapeDtypeStruct((bsz, t_len, qk), BF16),
                   jax.ShapeDtypeStruct((bsz, t_len, nv), BF16), jax.ShapeDtypeStruct((bsz, t_len, nv), BF16)],
        compiler_params=_params(2),
        name="ret_pre",
    )(res, mod, ng, pos_f, inv, w_in)


def _ret_mix_kernel(n_chunks, q_ref, k_ref, v_ref, g_ref, res_ref, mod_ref, ng_ref, wo_ref, out_ref, state, z_buf):
    @pl.when(pl.program_id(1) == 0)
    def _():
        state[...] = jnp.zeros(state.shape, F32)

    c = RET_CHUNK
    row = lax.broadcasted_iota(jnp.int32, (c, c), 0)
    col = lax.broadcasted_iota(jnp.int32, (c, c), 1)
    diff = (row - col).astype(F32)
    idx = lax.broadcasted_iota(jnp.int32, (c, 1), 0).astype(F32)
    heads = range(RET_H)
    log_gamma = [math.log(1.0 - 2.0 ** (-5.0 - hd)) for hd in heads]
    inner = [jnp.where(diff >= 0, jnp.exp(lg * jnp.maximum(diff, 0.0)), 0.0) for lg in log_gamma]
    q_decay = [jnp.exp(lg * (idx + 1.0)) for lg in log_gamma]
    k_decay = [jnp.exp(lg * (c - 1.0 - idx)) for lg in log_gamma]
    chunk_decay = [math.exp(lg * c) for lg in log_gamma]
    dk = lambda hd: slice(hd * RET_DK, (hd + 1) * RET_DK)
    dv = lambda hd: slice(hd * RET_DV, (hd + 1) * RET_DV)
    r_state = [state[hd] for hd in heads]
    for i in range(n_chunks):
        rows = slice(i * c, (i + 1) * c)
        q_c = [q_ref[0, rows, dk(hd)] for hd in heads]
        k_c = [k_ref[0, rows, dk(hd)] for hd in heads]
        v_c = [v_ref[0, rows, dv(hd)] for hd in heads]
        s = [_mm_nt(q_c[hd], k_c[hd]) * inner[hd] for hd in heads]
        cross = [_mm(q_c[hd], r_state[hd]) * q_decay[hd] for hd in heads]
        o = [_mm(s[hd], v_c[hd]) + cross[hd] for hd in heads]
        kv = [_mm_tn(k_c[hd].astype(F32) * k_decay[hd], v_c[hd]) for hd in heads]
        r_state = [r_state[hd] * chunk_decay[hd] + kv[hd] for hd in heads]
        for hd in heads:
            oc = o[hd] - jnp.mean(o[hd], axis=-1, keepdims=True)
            on = oc * lax.rsqrt(jnp.mean(oc * oc, axis=-1, keepdims=True) + HEAD_NORM_EPS)
            gt = g_ref[0, rows, dv(hd)].astype(F32)
            z_buf[rows, dv(hd)] = (gt * _sigmoid(gt) * on).astype(BF16)
    for hd in heads:
        state[hd] = r_state[hd]
    y = jnp.dot(z_buf[...], wo_ref[...], preferred_element_type=F32)
    out_ref[0] = _sublayer_out(res_ref[0], y, mod_ref[0], ng_ref[...], 1, 1.0)


def _ret_mix(q, k, v, g, res, mod, ng, w_o):
    bsz, t_len, d = res.shape
    tm = _tile(t_len, 512)
    qk = RET_H * RET_DK
    nv = RET_H * RET_DV
    tok = lambda n: pl.BlockSpec((1, tm, n), lambda b, t: (b, t, 0))
    return pl.pallas_call(
        functools.partial(_ret_mix_kernel, tm // RET_CHUNK),
        grid=(bsz, t_len // tm),
        in_specs=[tok(qk), tok(qk), tok(nv), tok(nv), tok(d),
                  pl.BlockSpec((1, 3 * N_SUB, d), lambda b, t: (b, 0, 0)), _resident(ng.shape), _resident(w_o.shape)],
        out_specs=tok(d),
        out_shape=jax.ShapeDtypeStruct(res.shape, F32),
        scratch_shapes=[pltpu.VMEM((RET_H, RET_DK, RET_DV), F32), pltpu.VMEM((tm, nv), BF16)],
        compiler_params=_params(2),
        name="ret_mix",
    )(q, k, v, g, res, mod, ng, w_o)


def _lru_kernel(res_ref, mod_ref, ng_ref, win_ref, cw_ref, vec_ref, gw_ref, wo_ref, out_ref, xext, h_carry):
    tm = res_ref.shape[1]
    width = xext.shape[1]

    @pl.when(pl.program_id(1) == 0)
    def _():
        xext[0:SUBLANES, :] = jnp.zeros((SUBLANES, width), F32)
        h_carry[...] = jnp.zeros(h_carry.shape, F32)

    mod = mod_ref[0]
    ng = ng_ref[...]
    cw = cw_ref[...]
    vec = vec_ref[...]
    conv_b, gate_bi, gate_br, lam = vec[0:1], vec[1:2], vec[2:3], vec[3:4]
    neg_lam = -lam
    softplus = jnp.maximum(neg_lam, 0.0) + jnp.log1p(jnp.exp(-jnp.abs(neg_lam)))
    parts = [slice(i * tm // LRU_ROW_PARTS, (i + 1) * tm // LRU_ROW_PARTS) for i in range(LRU_ROW_PARTS)]
    res = [res_ref[0, rows, :] for rows in parts]
    h = [_sublayer_in(x, mod, ng, 1).astype(BF16) for x in res]
    gate_branch = [jnp.dot(x, win_ref[:, :width], preferred_element_type=F32) for x in h]
    xb = [jnp.dot(x, win_ref[:, width:], preferred_element_type=F32) for x in h]
    for rows, x in zip(parts, xb):
        xext[SUBLANES + rows.start:SUBLANES + rows.stop, :] = x
    xc = []
    for rows, x in zip(parts, xb):
        acc = conv_b + cw[CONV_W - 1:CONV_W] * x
        for j in range(CONV_W - 1):
            lo = SUBLANES - (CONV_W - 1) + j
            acc = acc + cw[j:j + 1] * xext[lo + rows.start:lo + rows.stop, :]
        xc.append(acc)
    xext[0:SUBLANES, :] = xext[tm:tm + SUBLANES, :]

    a, u = [], []
    for x in xc:
        xcb = x.astype(BF16)
        gates = [jnp.concatenate([jnp.dot(xcb[:, hd * LRU_BW:(hd + 1) * LRU_BW], gw_ref[gi * LRU_H + hd],
                                          preferred_element_type=F32) for hd in range(LRU_H)], axis=1)
                 for gi in range(2)]
        i_gate = _sigmoid(gates[0] + gate_bi)
        r_gate = _sigmoid(gates[1] + gate_br)
        log_a = -LRU_C * r_gate * softplus
        a_p = jnp.exp(log_a)
        a.append(a_p)
        u.append(jnp.sqrt(-jnp.tanh(log_a) * (a_p * a_p + 1.0)) * (i_gate * x))

    row = lax.broadcasted_iota(jnp.int32, (SUBLANES, width), 0)
    carry = h_carry[...]
    hs = []
    for a_p, u_p in zip(a, u):
        groups = []
        for i in range(a_p.shape[0] // SUBLANES):
            ag = a_p[i * SUBLANES:(i + 1) * SUBLANES]
            ug = u_p[i * SUBLANES:(i + 1) * SUBLANES]
            for d in (1, 2, 4):
                keep = row >= d
                u_prev = jnp.where(keep, pltpu.roll(ug, d, 0), 0.0)
                a_prev = jnp.where(keep, pltpu.roll(ag, d, 0), 1.0)
                ug = ug + ag * u_prev
                ag = ag * a_prev
            hg = ug + ag * carry
            groups.append(hg)
            carry = jnp.broadcast_to(hg[SUBLANES - 1:SUBLANES, :], (SUBLANES, width))
        hs.append(jnp.concatenate(groups, axis=0))
    h_carry[...] = carry

    for rows, x, gb, hp in zip(parts, res, gate_branch, hs):
        gelu = 0.5 * gb * (1.0 + jnp.tanh(math.sqrt(2.0 / math.pi) * (gb + 0.044715 * (gb * gb * gb))))
        y = jnp.dot((gelu * hp).astype(BF16), wo_ref[...], preferred_element_type=F32)
        out_ref[0, rows, :] = _sublayer_out(x, y, mod, ng, 1, 1.0)


def _lru_sublayer(res, mod, ng, w_in, conv_w, vec, gate_w, w_o):
    bsz, t_len, d = res.shape
    width = w_o.shape[0]
    tm = _tile(t_len, LRU_TILE)
    tok = pl.BlockSpec((1, tm, d), lambda b, t: (b, t, 0))
    return pl.pallas_call(
        _lru_kernel,
        grid=(bsz, t_len // tm),
        in_specs=[tok, pl.BlockSpec((1, 3 * N_SUB, d), lambda b, t: (b, 0, 0)), _resident(ng.shape),
                  _resident(w_in.shape), _resident(conv_w.shape), _resident(vec.shape),
                  _resident(gate_w.shape), _resident(w_o.shape)],
        out_specs=tok,
        out_shape=jax.ShapeDtypeStruct(res.shape, F32),
        scratch_shapes=[pltpu.VMEM((tm + SUBLANES, width), F32), pltpu.VMEM((SUBLANES, width), F32)],
        compiler_params=_params(2),
        name="lru_sublayer",
    )(res, mod, ng, w_in, conv_w, vec, gate_w, w_o)


def kernel(x, c, positions, ada_w, ada_b, norm_g, ffn_w_in, ffn_w_out, rwkv_mu, rwkv_w_rkv, rwkv_w0, rwkv_w1, rwkv_w2, rwkv_a0, rwkv_a1, rwkv_a2, rwkv_g1, rwkv_g2, rwkv_k_k, rwkv_k_a, rwkv_r_k, rwkv_ln_w, rwkv_ln_b, rwkv_w_o, rwkv_v0, rwkv_v1, rwkv_v2, ret_w_in, ret_w_o, lru_w_in, lru_conv_w, lru_conv_b, lru_gate_w, lru_gate_b, lru_lambda, lru_w_o):
    depth = ada_w.shape[0]
    d_ff = ffn_w_out.shape[2]
    res = x.astype(F32)
    mod = _ada_mod(c.astype(F32), ada_w, ada_b)
    pos_f = positions.astype(F32)[..., None]
    v_first = None
    for i in range(depth):
        ng = norm_g[i]
        ffn = lambda res, m, s: _ffn_sublayer(
            res, mod[i], ng, ffn_w_in[i, m, :, :d_ff].astype(BF16), ffn_w_in[i, m, :, d_ff:].astype(BF16),
            ffn_w_out[i, m].astype(BF16), s)
        res = ffn(res, 0, 0)
        kind, j = i % 3, i // 3
        if kind == 0:
            p = dict(mu=rwkv_mu[j], w_rkv=rwkv_w_rkv[j], w0=rwkv_w0[j], w1=rwkv_w1[j], w2=rwkv_w2[j],
                     a0=rwkv_a0[j], a1=rwkv_a1[j], a2=rwkv_a2[j], g1=rwkv_g1[j], g2=rwkv_g2[j],
                     k_k=rwkv_k_k[j], k_a=rwkv_k_a[j], r_k=rwkv_r_k[j], ln_w=rwkv_ln_w[j], ln_b=rwkv_ln_b[j])
            if j > 0:
                p.update(v0=rwkv_v0[j - 1], v1=rwkv_v1[j - 1], v2=rwkv_v2[j - 1])
            p["w_o"] = rwkv_w_o[j]
            res, v_out = _rwkv_sublayer(res, mod[i], ng, p, v_first if j > 0 else None, j == 0)
            if j == 0:
                v_first = v_out
        elif kind == 1:
            q, k, v, g = _ret_pre(res, mod[i], ng, pos_f, ret_w_in[j].astype(BF16))
            res = _ret_mix(q, k, v, g, res, mod[i], ng, ret_w_o[j].astype(BF16))
        else:
            width = lru_w_o.shape[1]
            vec = jnp.stack([lru_conv_b[j], lru_gate_b[j, 0].reshape(width), lru_gate_b[j, 1].reshape(width),
                             lru_lambda[j]])
            gate_w = lru_gate_w[j].reshape(2 * LRU_H, LRU_BW, LRU_BW).astype(BF16)
            res = _lru_sublayer(res, mod[i], ng, lru_w_in[j].astype(BF16), lru_conv_w[j], vec, gate_w,
                                lru_w_o[j].astype(BF16))
        res = ffn(res, 1, 2)
    return res.astype(x.dtype)
```

```python
import functools
import math

import jax
import jax.numpy as jnp
from jax import lax
from jax.experimental import pallas as pl
from jax.experimental.pallas import tpu as pltpu

F32 = jnp.float32
BF16 = jnp.bfloat16

NORM_EPS = 1e-6
N_SUB = 3

RWKV_N = 64
RWKV_GROUP = 4
RWKV_GW = RWKV_GROUP * RWKV_N
RWKV_CHUNK = 64
RWKV_GN_EPS = 64e-5
LORA_PAD = 128

RET_H = 4
RET_DK = 256
RET_DV = 512
RET_CHUNK = 256
ROPE_BASE = 10000.0
HEAD_NORM_EPS = 1e-5

LRU_H = 5
LRU_BW = 256
CONV_W = 4
LRU_C = 8.0
SUBLANES = 8

VMEM_LIMIT = 56 * 1024 * 1024
MXU_WIDTH = 256
FFN_TILE = 1024
FFN_ROW_PARTS = 4
RET_PRE_PARTS = 2
LRU_TILE = 512
LRU_ROW_PARTS = 4

NT_DIMS = (((1,), (1,)), ((), ()))
TN_DIMS = (((0,), (0,)), ((), ()))


def _mm(a, b):
    return jnp.dot(a.astype(BF16), b.astype(BF16), preferred_element_type=F32)


def _mm_nt(a, b):
    return lax.dot_general(a.astype(BF16), b.astype(BF16), NT_DIMS, preferred_element_type=F32)


def _mm_tn(a, b):
    return lax.dot_general(a.astype(BF16), b.astype(BF16), TN_DIMS, preferred_element_type=F32)


def _mm_split_lhs_exact(w, x):
    hi = x.astype(BF16)
    lo = (x - hi.astype(F32)).astype(BF16)
    return (jnp.dot(w, hi, preferred_element_type=F32) + jnp.dot(w, lo, preferred_element_type=F32))


def _rms(x, g):
    return x * lax.rsqrt(jnp.mean(x * x, axis=-1, keepdims=True) + NORM_EPS) * g


def _sublayer_in(res, mod, ng, s):
    return _rms(res, ng[2 * s:2 * s + 1]) * (1.0 + mod[3 * s + 1:3 * s + 2]) + mod[3 * s:3 * s + 1]


def _sublayer_out(res, y, mod, ng, s, weight):
    return res + weight * mod[3 * s + 2:3 * s + 3] * _rms(y, ng[2 * s + 1:2 * s + 2])


def _sigmoid(x):
    return jax.nn.sigmoid(x)


def _resident(shape):
    nd = len(shape)
    return pl.BlockSpec(shape, lambda *_: (0,) * nd, pipeline_mode=pl.Buffered(1))


def _params(n_axes):
    return pltpu.CompilerParams(dimension_semantics=("arbitrary",) * n_axes, vmem_limit_bytes=VMEM_LIMIT)


def _tile(t_len, want):
    tm = min(want, t_len)
    assert t_len % tm == 0, (t_len, tm)
    return tm


def _ada_kernel(c_ref, w_ref, b_ref, o_ref):
    c = c_ref[...]
    cond = c * _sigmoid(c)
    o_ref[0] = _mm(cond, w_ref[0]) + b_ref[0]


def _ada_mod(c, ada_w, ada_b):
    depth, d, n = ada_w.shape
    bsz = c.shape[0]
    tn = 1536
    assert n % tn == 0
    out = pl.pallas_call(
        _ada_kernel,
        grid=(depth, n // tn),
        in_specs=[pl.BlockSpec((bsz, d), lambda l, j: (0, 0)),
                  pl.BlockSpec((1, d, tn), lambda l, j: (l, 0, j)),
                  pl.BlockSpec((1, 1, tn), lambda l, j: (l, 0, j))],
        out_specs=pl.BlockSpec((1, bsz, tn), lambda l, j: (l, 0, j)),
        out_shape=jax.ShapeDtypeStruct((depth, bsz, n), F32),
        compiler_params=_params(2),
        name="ada_mod",
    )(c, ada_w, ada_b.reshape(depth, 1, n))
    return out.reshape(depth, bsz, 3 * N_SUB, d)


def _ffn_kernel(s, res_ref, mod_ref, ng_ref, wa_ref, wb_ref, wo_ref, out_ref):
    mod = mod_ref[0]
    ng = ng_ref[...]
    f = wa_ref.shape[1]
    cut = -(-f // (2 * MXU_WIDTH)) * MXU_WIDTH
    chunks = [slice(0, cut), slice(cut, f)]
    tm = res_ref.shape[1]
    halves = [slice(i * tm // FFN_ROW_PARTS, (i + 1) * tm // FFN_ROW_PARTS) for i in range(FFN_ROW_PARTS)]
    res = [res_ref[0, rows, :] for rows in halves]
    h = [_sublayer_in(x, mod, ng, s).astype(BF16) for x in res]
    y = [jnp.zeros(x.shape, F32) for x in res]
    for cols in chunks:
        for i in range(len(halves)):
            a = jnp.dot(h[i], wa_ref[:, cols], preferred_element_type=F32)
            b = jnp.dot(h[i], wb_ref[:, cols], preferred_element_type=F32)
            z = (a * _sigmoid(a) * b).astype(BF16)
            y[i] = y[i] + jnp.dot(z, wo_ref[cols, :], preferred_element_type=F32)
    for i, rows in enumerate(halves):
        out_ref[0, rows, :] = _sublayer_out(res[i], y[i], mod, ng, s, 0.5)


def _ffn_sublayer(res, mod, ng, w_a, w_b, w_o, s):
    bsz, t_len, d = res.shape
    f = w_a.shape[1]
    tm = _tile(t_len, FFN_TILE)
    return pl.pallas_call(
        functools.partial(_ffn_kernel, s),
        grid=(bsz, t_len // tm),
        in_specs=[pl.BlockSpec((1, tm, d), lambda b, t: (b, t, 0)),
                  pl.BlockSpec((1, 3 * N_SUB, d), lambda b, t: (b, 0, 0)),
                  _resident(ng.shape), _resident((d, f)), _resident((d, f)), _resident((f, d))],
        out_specs=pl.BlockSpec((1, tm, d), lambda b, t: (b, t, 0)),
        out_shape=jax.ShapeDtypeStruct(res.shape, F32),
        compiler_params=_params(2),
        name="ffn_sublayer",
    )(res, mod, ng, w_a, w_b, w_o)


RWKV_STREAMS = ("r", "lw", "k", "v", "kk", "a", "g")


def _rwkv_pre_stages(has_vres, res, mod, ng_ref, mu_ref, vec_ref, w, vfirst, hext, store):
    tm, d = res.shape
    h = _sublayer_in(res, mod, ng_ref[...], 1)
    hext[SUBLANES:SUBLANES + tm, :] = h
    dx = hext[SUBLANES - 1:SUBLANES - 1 + tm, :] - h
    hext[0:SUBLANES, :] = hext[tm:tm + SUBLANES, :]
    mu = mu_ref[...]
    xr, xw, xk, xv, xa, xg = ((h + dx * mu[j:j + 1]).astype(BF16) for j in range(6))
    vec = vec_ref[...]
    w0, a0, k_k, k_a = vec[0:1], vec[1:2], vec[2:3], vec[3:4]
    full = slice(0, d)
    yield
    tw = jnp.tanh(jnp.dot(xw, w["w1"][...], preferred_element_type=F32))
    ta = jnp.dot(xa, w["a1"][...], preferred_element_type=F32)
    yield
    tg = _sigmoid(jnp.dot(xg, w["g1"][...], preferred_element_type=F32))
    if has_vres:
        tv = jnp.dot(xv, w["v1"][...], preferred_element_type=F32)
    yield
    store("lw", full, -math.exp(-0.5) * _sigmoid(w0 + _mm(tw, w["w2"][...])))
    a = _sigmoid(a0 + _mm(ta, w["a2"][...]))
    store("a", full, a)
    yield
    store("g", full, _mm(tg, w["g2"][...]))
    if has_vres:
        mix = _sigmoid(w["v0"][...] + _mm(tv, w["v2"][...]))
    yield
    for lo in range(0, d, RWKV_GW):
        cols = slice(lo, lo + RWKV_GW)
        store("r", cols, jnp.dot(xr, w["wr"][:, cols], preferred_element_type=F32))
        yield
    for lo in range(0, d, RWKV_GW):
        cols = slice(lo, lo + RWKV_GW)
        k = jnp.dot(xk, w["wk"][:, cols], preferred_element_type=F32)
        store("kk", cols, k * k_k[:, cols])
        store("k", cols, k * (1.0 + (a[:, cols] - 1.0) * k_a[:, cols]))
        yield
    for lo in range(0, d, RWKV_GW):
        cols = slice(lo, lo + RWKV_GW)
        v = jnp.dot(xv, w["wv"][:, cols], preferred_element_type=F32)
        if has_vres:
            v = v + (vfirst[:, cols].astype(F32) - v) * mix[:, cols]
        store("v", cols, v)
        yield


def _pad_cols(w, n):
    return jnp.pad(w, ((0, 0), (0, n - w.shape[1])))


def _pad_rows(w, n):
    return jnp.pad(w, ((0, n - w.shape[0]), (0, 0)))


def _block_diag(x, bd_mask):
    xb = x.astype(BF16)
    return jnp.where(bd_mask, jnp.concatenate([xb] * RWKV_GROUP, axis=0), jnp.zeros((), BF16))


class _Staged:
    def __init__(self, gen):
        self.gen, self.done, self.value = gen, False, None

    def step(self):
        if not self.done:
            try:
                next(self.gen)
            except StopIteration as stop:
                self.done, self.value = True, stop.value

    def finish(self):
        while not self.done:
            self.step()
        return self.value


def _wkv_prepare(raw, masks, consts):
    bd_mask, strict, incl, eye = masks
    tril = consts
    c = RWKV_CHUNK
    bd = lambda x: _block_diag(x, bd_mask)
    cum = [_mm_split_lhs_exact(tril, lw) for r, lw, k, v, kkn, a in raw]
    yield
    units = []
    for (r, lw, k, v, kkn, a), cum_i in zip(raw, cum):
        g_t = jnp.exp(cum_i)
        g_inv = jnp.exp(-cum_i)
        units.append((r * g_t, -kkn * jnp.exp(cum_i - lw), kkn * a * g_inv, k * g_inv, v, g_t[c - 1:c]))
    ar = [jnp.concatenate([at, rt], axis=0).astype(BF16) for rt, at, bt, kt, v, gl in units]
    sv = [bd(u[4]) for u in units]
    lb = [_mm_nt(x, bd(u[2])) for x, u in zip(ar, units)]
    lk = [_mm_nt(x, bd(u[3])) for x, u in zip(ar, units)]
    yield
    l_ab = [jnp.where(strict, x[:c], 0.0) for x in lb]
    l_rb = [jnp.where(incl, x[c:], 0.0) for x in lb]
    l_ak = [jnp.where(strict, x[:c], 0.0) for x in lk]
    l_rk = [jnp.where(incl, x[c:], 0.0) for x in lk]
    pw = [_mm(x, bd(x)) for x in l_ab]
    inv = [eye + x for x in l_ab]
    yield
    for _ in range(4):
        both = [_mm(jnp.concatenate([p, t], axis=0), bd(p)) for p, t in zip(pw, inv)]
        pw = [x[:c] for x in both]
        inv = [t + x[c:] for t, x in zip(inv, both)]
        yield
    inv = [t + _mm(t, bd(p)) for t, p in zip(inv, pw)]
    kv = [_mm(jnp.concatenate([x, y], axis=0), s) for x, y, s in zip(l_ak, l_rk, sv)]
    ak_v = [x[:c] for x in kv]
    rk_v = [x[c:] for x in kv]
    bk = [jnp.concatenate([bt, kt], axis=0).astype(BF16) for rt, at, bt, kt, v, gl in units]
    return [dict(ar=ar[i], inv=inv[i], l_rb=l_rb[i], ak_v=ak_v[i], rk_v=rk_v[i], bk=bk[i], v=units[i][4],
                 g_last=units[i][5]) for i in range(len(units))]


def _wkv_advance(get_prepared, n_chunks, z_states, masks):
    bd_mask = masks[0]
    c = RWKV_CHUNK
    bd = lambda x: _block_diag(x, bd_mask)
    ys = []
    for i in range(n_chunks):
        prepared = get_prepared(i)
        ars = [_mm_nt(p["ar"], z) for p, z in zip(prepared, z_states)]
        yield
        u = [_mm(p["inv"], bd(x[:c] + p["ak_v"])) for p, x in zip(prepared, ars)]
        yield
        y = [x[c:] + _mm(p["l_rb"], bd(uu)) + p["rk_v"] for p, x, uu in zip(prepared, ars, u)]
        dz = [_mm_tn(jnp.concatenate([uu, p["v"]], axis=0), p["bk"]) for p, uu in zip(prepared, u)]
        z_states = [(z + jnp.where(bd_mask, d, 0.0)) * p["g_last"] for z, d, p in zip(z_states, dz, prepared)]
        ys.append(y)
        yield
    return ys, z_states


def _wkv_masks():
    gw, c = RWKV_GW, RWKV_CHUNK
    row = lax.broadcasted_iota(jnp.int32, (gw, gw), 0)
    col = lax.broadcasted_iota(jnp.int32, (gw, gw), 1)
    bd_mask = (row // RWKV_N) == (col // RWKV_N)
    t = lax.broadcasted_iota(jnp.int32, (c, gw), 0)
    j = lax.broadcasted_iota(jnp.int32, (c, gw), 1) % c
    eye = jnp.where(t == j, 1.0, 0.0).astype(F32)
    rc = lax.broadcasted_iota(jnp.int32, (c, c), 0)
    cc = lax.broadcasted_iota(jnp.int32, (c, c), 1)
    tril = jnp.where(rc >= cc, 1.0, 0.0).astype(BF16)
    bd_ones = jnp.where(bd_mask, 1.0, 0.0).astype(BF16)
    return (bd_mask, t > j, t >= j, eye), tril, bd_ones


def _rwkv_kernel(has_vres, emit_vfirst, n_chunks, tiles_per_row, n_tiles, *refs):
    names = ["res_pre", "mod_pre", "res_wkv", "mod_wkv", "ng", "mu", "vec4", "wr", "wk", "wv", "w1", "w2", "a1", "a2",
             "g1", "g2"] + (["v1", "v2", "v0", "vfirst"] if has_vres else []) + ["vec3", "wo", "out"]
    names += (["vfirst_out"] if emit_vfirst else []) + ["hext", "state", "z_buf"] + ["p_" + n for n in RWKV_STREAMS]
    ref = dict(zip(names, refs))
    assert len(names) == len(refs)
    step = pl.program_id(0)
    slot_w = step % 2
    slot_r = 1 - slot_w
    pre_tile = jnp.minimum(step, n_tiles - 1)
    wkv_tile = jnp.maximum(step - 1, 0)
    hext, state, z_buf = ref["hext"], ref["state"], ref["z_buf"]

    @pl.when(step == 0)
    def _():
        for n in RWKV_STREAMS:
            ref["p_" + n][1] = jnp.zeros(ref["p_" + n].shape[1:], ref["p_" + n].dtype)

    @pl.when(pre_tile % tiles_per_row == 0)
    def _():
        hext[0:SUBLANES, :] = jnp.zeros((SUBLANES, hext.shape[1]), F32)

    @pl.when(wkv_tile % tiles_per_row == 0)
    def _():
        state[...] = jnp.zeros(state.shape, F32)

    def store(name, cols, value):
        dst = ref["p_" + name]
        dst[slot_w, :, cols] = value.astype(dst.dtype)

    pre = _Staged(_rwkv_pre_stages(has_vres, ref["res_pre"][0], ref["mod_pre"][0], ref["ng"], ref["mu"], ref["vec4"],
                                   ref, ref["vfirst"][0] if has_vres else None, hext, store))

    c = RWKV_CHUNK
    n_groups = state.shape[0]
    masks, tril, bd_ones = _wkv_masks()
    lanes = [slice(gi * RWKV_GW, (gi + 1) * RWKV_GW) for gi in range(n_groups)]
    tb = n_chunks * c
    vec3 = ref["vec3"]

    def head_sums(per_group):
        s = _mm(jnp.concatenate(per_group, axis=0), bd_ones)
        return [s[gi * tb:(gi + 1) * tb] for gi in range(n_groups)]

    load = lambda n: [ref["p_" + n][slot_r, :, ln].astype(F32) for ln in lanes]
    r, lw, k, v, kk, a = (load(n) for n in RWKV_STREAMS[:6])
    if emit_vfirst:
        ref["vfirst_out"][0] = ref["p_v"][slot_r]
    kkn = [x / jnp.maximum(jnp.sqrt(n2), 1e-12) for x, n2 in zip(kk, head_sums([x * x for x in kk]))]
    bonus = head_sums([r[gi] * k[gi] * vec3[0:1, lanes[gi]] for gi in range(n_groups)])
    pre.step()

    def raw_units(i):
        rows = slice(i * c, (i + 1) * c)
        return [tuple(x[gi][rows] for x in (r, lw, k, v, kkn, a)) for gi in range(n_groups)]

    lead = min(2, n_chunks)
    first_task = _Staged(_wkv_prepare([u for i in range(lead) for u in raw_units(i)], masks, tril))
    while not first_task.done:
        first_task.step()
        pre.step()
    first = first_task.value
    later = [_Staged(_wkv_prepare(raw_units(i), masks, tril)) for i in range(lead, n_chunks)]

    def get_prepared(i):
        if i < lead:
            return first[i * n_groups:(i + 1) * n_groups]
        return later[i - lead].finish()

    advance = _Staged(_wkv_advance(get_prepared, n_chunks, [state[gi] for gi in range(n_groups)], masks))
    while not advance.done:
        advance.step()
        for task in later:
            if not task.done:
                task.step()
                break
        pre.step()
    ys, z_states = advance.value
    for gi in range(n_groups):
        state[gi] = z_states[gi]
    y = [jnp.concatenate([ys[i][gi] for i in range(n_chunks)], axis=0) for gi in range(n_groups)]
    yc = [x - m * (1.0 / RWKV_N) for x, m in zip(y, head_sums(y))]
    pre.step()
    var = head_sums([x * x for x in yc])
    pre.step()
    for gi, ln in enumerate(lanes):
        yn = yc[gi] * lax.rsqrt(var[gi] * (1.0 / RWKV_N) + RWKV_GN_EPS) * vec3[1:2, ln] + vec3[2:3, ln]
        z_buf[:, ln] = ((yn + bonus[gi] * v[gi]) * ref["p_g"][slot_r, :, ln].astype(F32)).astype(BF16)
    pre.finish()
    y_out = jnp.dot(z_buf[...], ref["wo"][...], preferred_element_type=F32)
    ref["out"][0] = _sublayer_out(ref["res_wkv"][0], y_out, ref["mod_wkv"][0], ref["ng"][...], 1, 1.0)


def _rwkv_sublayer(res, mod, ng, p, v_first, emit_vfirst):
    bsz, t_len, d = res.shape
    n_groups = d // RWKV_GW
    tb = _tile(t_len, 256)
    per_row = t_len // tb
    n_tiles = bsz * per_row
    has_vres = v_first is not None
    pre = lambda s: jnp.minimum(s, n_tiles - 1)
    wkv = lambda s: jnp.maximum(s - 1, 0)
    tok_pre = pl.BlockSpec((1, tb, d), lambda s: (pre(s) // per_row, pre(s) % per_row, 0))
    tok_wkv = pl.BlockSpec((1, tb, d), lambda s: (wkv(s) // per_row, wkv(s) % per_row, 0))
    mod_pre = pl.BlockSpec((1, 3 * N_SUB, d), lambda s: (pre(s) // per_row, 0, 0))
    mod_wkv = pl.BlockSpec((1, 3 * N_SUB, d), lambda s: (wkv(s) // per_row, 0, 0))
    lora_in = lambda w: _pad_cols(w, LORA_PAD).astype(BF16)
    lora_out = lambda w: _pad_rows(w, LORA_PAD).astype(BF16)
    vec4 = jnp.stack([p["w0"], p["a0"], p["k_k"], p["k_a"]])
    vec3 = jnp.stack([p["r_k"].reshape(d), p["ln_w"], p["ln_b"]])
    weights = [ng, p["mu"], vec4, p["w_rkv"][0].astype(BF16), p["w_rkv"][1].astype(BF16), p["w_rkv"][2].astype(BF16),
               lora_in(p["w1"]), lora_out(p["w2"]), lora_in(p["a1"]), lora_out(p["a2"]),
               lora_in(p["g1"]), lora_out(p["g2"])]
    args = [res, mod, res, mod] + weights
    in_specs = [tok_pre, mod_pre, tok_wkv, mod_wkv] + [_resident(x.shape) for x in weights]
    if has_vres:
        extra = [lora_in(p["v1"]), lora_out(p["v2"]), p["v0"].reshape(1, d)]
        args += extra + [v_first]
        in_specs += [_resident(x.shape) for x in extra] + [tok_pre]
    tail = [vec3, p["w_o"].astype(BF16)]
    args += tail
    in_specs += [_resident(x.shape) for x in tail]
    out_shape = [jax.ShapeDtypeStruct(res.shape, F32)]
    out_specs = [tok_wkv]
    if emit_vfirst:
        out_shape.append(jax.ShapeDtypeStruct(res.shape, BF16))
        out_specs.append(tok_wkv)
    streams = [pltpu.VMEM((2, tb, d), F32 if n == "lw" else BF16) for n in RWKV_STREAMS]
    outs = pl.pallas_call(
        functools.partial(_rwkv_kernel, has_vres, emit_vfirst, tb // RWKV_CHUNK, per_row, n_tiles),
        grid=(n_tiles + 1,),
        in_specs=in_specs,
        out_specs=out_specs,
        out_shape=out_shape,
        scratch_shapes=[pltpu.VMEM((tb + SUBLANES, d), F32), pltpu.VMEM((n_groups, RWKV_GW, RWKV_GW), F32),
                        pltpu.VMEM((tb, d), BF16)] + streams,
        compiler_params=_params(1),
        name="rwkv_sublayer",
    )(*args)
    return outs[0], (outs[1] if emit_vfirst else None)


def _ret_pre_kernel(res_ref, mod_ref, ng_ref, pos_ref, inv_ref, w_ref, q_ref, k_ref, v_ref, g_ref):
    tm = res_ref.shape[1]
    half = RET_DK // 2
    qk = RET_H * RET_DK
    nv = RET_H * RET_DV
    mod = mod_ref[0]
    ng = ng_ref[...]
    for part in range(RET_PRE_PARTS):
        rows = slice(part * tm // RET_PRE_PARTS, (part + 1) * tm // RET_PRE_PARTS)
        h = _sublayer_in(res_ref[0, rows, :], mod, ng, 1).astype(BF16)
        for hd in range(RET_H):
            lo = 2 * qk + hd * RET_DV
            v_ref[0, rows, hd * RET_DV:(hd + 1) * RET_DV] = jnp.dot(
                h, w_ref[:, lo:lo + RET_DV], preferred_element_type=F32).astype(BF16)
            g_ref[0, rows, hd * RET_DV:(hd + 1) * RET_DV] = jnp.dot(
                h, w_ref[:, lo + nv:lo + nv + RET_DV], preferred_element_type=F32).astype(BF16)
        ang = pos_ref[0, rows, :] * inv_ref[...]
        cos = jnp.cos(ang)
        sin = jnp.sin(ang)
        for which, out_ref, scale in ((0, q_ref, 1.0), (1, k_ref, RET_DK ** -0.5)):
            for hd in range(RET_H):
                lo = which * qk + hd * RET_DK
                x = jnp.dot(h, w_ref[:, lo:lo + RET_DK], preferred_element_type=F32)
                x1, x2 = x[:, :half], x[:, half:]
                out_ref[0, rows, hd * RET_DK:hd * RET_DK + half] = ((x1 * cos - x2 * sin) * scale).astype(BF16)
                out_ref[0, rows, hd * RET_DK + half:(hd + 1) * RET_DK] = ((x1 * sin + x2 * cos) * scale).astype(BF16)


def _ret_pre(res, mod, ng, pos_f, w_in):
    bsz, t_len, d = res.shape
    tm = _tile(t_len, 512)
    qk = RET_H * RET_DK
    nv = RET_H * RET_DV
    half = RET_DK // 2
    inv = (1.0 / (ROPE_BASE ** jnp.linspace(0.0, 1.0, half, dtype=F32))).reshape(1, half)
    tok = lambda n: pl.BlockSpec((1, tm, n), lambda b, t: (b, t, 0))
    return pl.pallas_call(
        _ret_pre_kernel,
        grid=(bsz, t_len // tm),
        in_specs=[tok(d), pl.BlockSpec((1, 3 * N_SUB, d), lambda b, t: (b, 0, 0)), _resident(ng.shape),
                  tok(1), _resident((1, half)), _resident(w_in.shape)],
        out_specs=[tok(qk), tok(qk), tok(nv), tok(nv)],
        out_shape=[jax.ShapeDtypeStruct((bsz, t_len, qk), BF16), jax.ShapeDtypeStruct((bsz, t_len, qk), BF16),
                   jax.ShapeDtypeStruct((bsz, t_len, nv), BF16), jax.ShapeDtypeStruct((bsz, t_len, nv), BF16)],
        compiler_params=_params(2),
        name="ret_pre",
    )(res, mod, ng, pos_f, inv, w_in)


def _ret_mix_kernel(n_chunks, q_ref, k_ref, v_ref, g_ref, res_ref, mod_ref, ng_ref, wo_ref, out_ref, state, z_buf):
    @pl.when(pl.program_id(1) == 0)
    def _():
        state[...] = jnp.zeros(state.shape, F32)

    c = RET_CHUNK
    row = lax.broadcasted_iota(jnp.int32, (c, c), 0)
    col = lax.broadcasted_iota(jnp.int32, (c, c), 1)
    diff = (row - col).astype(F32)
    idx = lax.broadcasted_iota(jnp.int32, (c, 1), 0).astype(F32)
    heads = range(RET_H)
    log_gamma = [math.log(1.0 - 2.0 ** (-5.0 - hd)) for hd in heads]
    inner = [jnp.where(diff >= 0, jnp.exp(lg * jnp.maximum(diff, 0.0)), 0.0) for lg in log_gamma]
    q_decay = [jnp.exp(lg * (idx + 1.0)) for lg in log_gamma]
    k_decay = [jnp.exp(lg * (c - 1.0 - idx)) for lg in log_gamma]
    chunk_decay = [math.exp(lg * c) for lg in log_gamma]
    dk = lambda hd: slice(hd * RET_DK, (hd + 1) * RET_DK)
    dv = lambda hd: slice(hd * RET_DV, (hd + 1) * RET_DV)
    r_state = [state[hd] for hd in heads]
    for i in range(n_chunks):
        rows = slice(i * c, (i + 1) * c)
        q_c = [q_ref[0, rows, dk(hd)] for hd in heads]
        k_c = [k_ref[0, rows, dk(hd)] for hd in heads]
        v_c = [v_ref[0, rows, dv(hd)] for hd in heads]
        s = [_mm_nt(q_c[hd], k_c[hd]) * inner[hd] for hd in heads]
        cross = [_mm(q_c[hd], r_state[hd]) * q_decay[hd] for hd in heads]
        o = [_mm(s[hd], v_c[hd]) + cross[hd] for hd in heads]
        kv = [_mm_tn(k_c[hd].astype(F32) * k_decay[hd], v_c[hd]) for hd in heads]
        r_state = [r_state[hd] * chunk_decay[hd] + kv[hd] for hd in heads]
        for hd in heads:
            oc = o[hd] - jnp.mean(o[hd], axis=-1, keepdims=True)
            on = oc * lax.rsqrt(jnp.mean(oc * oc, axis=-1, keepdims=True) + HEAD_NORM_EPS)
            gt = g_ref[0, rows, dv(hd)].astype(F32)
            z_buf[rows, dv(hd)] = (gt * _sigmoid(gt) * on).astype(BF16)
    for hd in heads:
        state[hd] = r_state[hd]
    y = jnp.dot(z_buf[...], wo_ref[...], preferred_element_type=F32)
    out_ref[0] = _sublayer_out(res_ref[0], y, mod_ref[0], ng_ref[...], 1, 1.0)


def _ret_mix(q, k, v, g, res, mod, ng, w_o):
    bsz, t_len, d = res.shape
    tm = _tile(t_len, 512)
    qk = RET_H * RET_DK
    nv = RET_H * RET_DV
    tok = lambda n: pl.BlockSpec((1, tm, n), lambda b, t: (b, t, 0))
    return pl.pallas_call(
        functools.partial(_ret_mix_kernel, tm // RET_CHUNK),
        grid=(bsz, t_len // tm),
        in_specs=[tok(qk), tok(qk), tok(nv), tok(nv), tok(d),
                  pl.BlockSpec((1, 3 * N_SUB, d), lambda b, t: (b, 0, 0)), _resident(ng.shape), _resident(w_o.shape)],
        out_specs=tok(d),
        out_shape=jax.ShapeDtypeStruct(res.shape, F32),
        scratch_shapes=[pltpu.VMEM((RET_H, RET_DK, RET_DV), F32), pltpu.VMEM((tm, nv), BF16)],
        compiler_params=_params(2),
        name="ret_mix",
    )(q, k, v, g, res, mod, ng, w_o)


def _lru_kernel(res_ref, mod_ref, ng_ref, win_ref, cw_ref, vec_ref, gw_ref, wo_ref, out_ref, xext, h_carry):
    tm = res_ref.shape[1]
    width = xext.shape[1]

    @pl.when(pl.program_id(1) == 0)
    def _():
        xext[0:SUBLANES, :] = jnp.zeros((SUBLANES, width), F32)
        h_carry[...] = jnp.zeros(h_carry.shape, F32)

    mod = mod_ref[0]
    ng = ng_ref[...]
    cw = cw_ref[...]
    vec = vec_ref[...]
    conv_b, gate_bi, gate_br, lam = vec[0:1], vec[1:2], vec[2:3], vec[3:4]
    neg_lam = -lam
    softplus = jnp.maximum(neg_lam, 0.0) + jnp.log1p(jnp.exp(-jnp.abs(neg_lam)))
    parts = [slice(i * tm // LRU_ROW_PARTS, (i + 1) * tm // LRU_ROW_PARTS) for i in range(LRU_ROW_PARTS)]
    res = [res_ref[0, rows, :] for rows in parts]
    h = [_sublayer_in(x, mod, ng, 1).astype(BF16) for x in res]
    gate_branch = [jnp.dot(x, win_ref[:, :width], preferred_element_type=F32) for x in h]
    xb = [jnp.dot(x, win_ref[:, width:], preferred_element_type=F32) for x in h]
    for rows, x in zip(parts, xb):
        xext[SUBLANES + rows.start:SUBLANES + rows.stop, :] = x
    xc = []
    for rows, x in zip(parts, xb):
        acc = conv_b + cw[CONV_W - 1:CONV_W] * x
        for j in range(CONV_W - 1):
            lo = SUBLANES - (CONV_W - 1) + j
            acc = acc + cw[j:j + 1] * xext[lo + rows.start:lo + rows.stop, :]
        xc.append(acc)
    xext[0:SUBLANES, :] = xext[tm:tm + SUBLANES, :]

    a, u = [], []
    for x in xc:
        xcb = x.astype(BF16)
        gates = [jnp.concatenate([jnp.dot(xcb[:, hd * LRU_BW:(hd + 1) * LRU_BW], gw_ref[gi * LRU_H + hd],
                                          preferred_element_type=F32) for hd in range(LRU_H)], axis=1)
                 for gi in range(2)]
        i_gate = _sigmoid(gates[0] + gate_bi)
        r_gate = _sigmoid(gates[1] + gate_br)
        log_a = -LRU_C * r_gate * softplus
        a_p = jnp.exp(log_a)
        a.append(a_p)
        u.append(jnp.sqrt(-jnp.tanh(log_a) * (a_p * a_p + 1.0)) * (i_gate * x))

    row = lax.broadcasted_iota(jnp.int32, (SUBLANES, width), 0)
    carry = h_carry[...]
    hs = []
    for a_p, u_p in zip(a, u):
        groups = []
        for i in range(a_p.shape[0] // SUBLANES):
            ag = a_p[i * SUBLANES:(i + 1) * SUBLANES]
            ug = u_p[i * SUBLANES:(i + 1) * SUBLANES]
            for d in (1, 2, 4):
                keep = row >= d
                u_prev = jnp.where(keep, pltpu.roll(ug, d, 0), 0.0)
                a_prev = jnp.where(keep, pltpu.roll(ag, d, 0), 1.0)
                ug = ug + ag * u_prev
                ag = ag * a_prev
            hg = ug + ag * carry
            groups.append(hg)
            carry = jnp.broadcast_to(hg[SUBLANES - 1:SUBLANES, :], (SUBLANES, width))
        hs.append(jnp.concatenate(groups, axis=0))
    h_carry[...] = carry

    for rows, x, gb, hp in zip(parts, res, gate_branch, hs):
        gelu = 0.5 * gb * (1.0 + jnp.tanh(math.sqrt(2.0 / math.pi) * (gb + 0.044715 * (gb * gb * gb))))
        y = jnp.dot((gelu * hp).astype(BF16), wo_ref[...], preferred_element_type=F32)
        out_ref[0, rows, :] = _sublayer_out(x, y, mod, ng, 1, 1.0)


def _lru_sublayer(res, mod, ng, w_in, conv_w, vec, gate_w, w_o):
    bsz, t_len, d = res.shape
    width = w_o.shape[0]
    tm = _tile(t_len, LRU_TILE)
    tok = pl.BlockSpec((1, tm, d), lambda b, t: (b, t, 0))
    return pl.pallas_call(
        _lru_kernel,
        grid=(bsz, t_len // tm),
        in_specs=[tok, pl.BlockSpec((1, 3 * N_SUB, d), lambda b, t: (b, 0, 0)), _resident(ng.shape),
                  _resident(w_in.shape), _resident(conv_w.shape), _resident(vec.shape),
                  _resident(gate_w.shape), _resident(w_o.shape)],
        out_specs=tok,
        out_shape=jax.ShapeDtypeStruct(res.shape, F32),
        scratch_shapes=[pltpu.VMEM((tm + SUBLANES, width), F32), pltpu.VMEM((SUBLANES, width), F32)],
        compiler_params=_params(2),
        name="lru_sublayer",
    )(res, mod, ng, w_in, conv_w, vec, gate_w, w_o)


def kernel(x, c, positions, ada_w, ada_b, norm_g, ffn_w_in, ffn_w_out, rwkv_mu, rwkv_w_rkv, rwkv_w0, rwkv_w1, rwkv_w2, rwkv_a0, rwkv_a1, rwkv_a2, rwkv_g1, rwkv_g2, rwkv_k_k, rwkv_k_a, rwkv_r_k, rwkv_ln_w, rwkv_ln_b, rwkv_w_o, rwkv_v0, rwkv_v1, rwkv_v2, ret_w_in, ret_w_o, lru_w_in, lru_conv_w, lru_conv_b, lru_gate_w, lru_gate_b, lru_lambda, lru_w_o):
    depth = ada_w.shape[0]
    d_ff = ffn_w_out.shape[2]
    res = x.astype(F32)
    mod = _ada_mod(c.astype(F32), ada_w, ada_b)
    pos_f = positions.astype(F32)[..., None]
    v_first = None
    for i in range(depth):
        ng = norm_g[i]
        ffn = lambda res, m, s: _ffn_sublayer(
            res, mod[i], ng, ffn_w_in[i, m, :, :d_ff].astype(BF16), ffn_w_in[i, m, :, d_ff:].astype(BF16),
            ffn_w_out[i, m].astype(BF16), s)
        res = ffn(res, 0, 0)
        kind, j = i % 3, i // 3
        if kind == 0:
            p = dict(mu=rwkv_mu[j], w_rkv=rwkv_w_rkv[j], w0=rwkv_w0[j], w1=rwkv_w1[j], w2=rwkv_w2[j],
                     a0=rwkv_a0[j], a1=rwkv_a1[j], a2=rwkv_a2[j], g1=rwkv_g1[j], g2=rwkv_g2[j],
                     k_k=rwkv_k_k[j], k_a=rwkv_k_a[j], r_k=rwkv_r_k[j], ln_w=rwkv_ln_w[j], ln_b=rwkv_ln_b[j])
            if j > 0:
                p.update(v0=rwkv_v0[j - 1], v1=rwkv_v1[j - 1], v2=rwkv_v2[j - 1])
            p["w_o"] = rwkv_w_o[j]
            res, v_out = _rwkv_sublayer(res, mod[i], ng, p, v_first if j > 0 else None, j == 0)
            if j == 0:
                v_first = v_out
        elif kind == 1:
            q, k, v, g = _ret_pre(res, mod[i], ng, pos_f, ret_w_in[j].astype(BF16))
            res = _ret_mix(q, k, v, g, res, mod[i], ng, ret_w_o[j].astype(BF16))
        else:
            width = lru_w_o.shape[1]
            vec = jnp.stack([lru_conv_b[j], lru_gate_b[j, 0].reshape(width), lru_gate_b[j, 1].reshape(width),
                             lru_lambda[j]])
            gate_w = lru_gate_w[j].reshape(2 * LRU_H, LRU_BW, LRU_BW).astype(BF16)
            res = _lru_sublayer(res, mod[i], ng, lru_w_in[j].astype(BF16), lru_conv_w[j], vec, gate_w,
                                lru_w_o[j].astype(BF16))
        res = ffn(res, 1, 2)
    return res.astype(x.dtype)
```

```python
import functools
import math

import jax
import jax.numpy as jnp
from jax import lax
from jax.experimental import pallas as pl
from jax.experimental.pallas import tpu as pltpu

F32 = jnp.float32
BF16 = jnp.bfloat16

NORM_EPS = 1e-6
N_SUB = 3

RWKV_N = 64
RWKV_GROUP = 4
RWKV_GW = RWKV_GROUP * RWKV_N
RWKV_CHUNK = 64
RWKV_GN_EPS = 64e-5
LORA_PAD = 128

RET_H = 4
RET_DK = 256
RET_DV = 512
RET_CHUNK = 256
ROPE_BASE = 10000.0
HEAD_NORM_EPS = 1e-5

LRU_H = 5
LRU_BW = 256
CONV_W = 4
LRU_C = 8.0
SUBLANES = 8

VMEM_LIMIT = 56 * 1024 * 1024
MXU_WIDTH = 256
FFN_TILE = 1024
FFN_ROW_PARTS = 4
RET_PRE_PARTS = 2
LRU_TILE = 512
LRU_ROW_PARTS = 4

NT_DIMS = (((1,), (1,)), ((), ()))
TN_DIMS = (((0,), (0,)), ((), ()))


def _mm(a, b):
    return jnp.dot(a.astype(BF16), b.astype(BF16), preferred_element_type=F32)


def _mm_nt(a, b):
    return lax.dot_general(a.astype(BF16), b.astype(BF16), NT_DIMS, preferred_element_type=F32)


def _mm_tn(a, b):
    return lax.dot_general(a.astype(BF16), b.astype(BF16), TN_DIMS, preferred_element_type=F32)


def _mm_split_lhs_exact(w, x):
    hi = x.astype(BF16)
    lo = (x - hi.astype(F32)).astype(BF16)
    return (jnp.dot(w, hi, preferred_element_type=F32) + jnp.dot(w, lo, preferred_element_type=F32))


def _rms(x, g):
    return x * lax.rsqrt(jnp.mean(x * x, axis=-1, keepdims=True) + NORM_EPS) * g


def _sublayer_in(res, mod, ng, s):
    return _rms(res, ng[2 * s:2 * s + 1]) * (1.0 + mod[3 * s + 1:3 * s + 2]) + mod[3 * s:3 * s + 1]


def _sublayer_out(res, y, mod, ng, s, weight):
    return res + weight * mod[3 * s + 2:3 * s + 3] * _rms(y, ng[2 * s + 1:2 * s + 2])


def _sigmoid(x):
    return jax.nn.sigmoid(x)


def _resident(shape):
    nd = len(shape)
    return pl.BlockSpec(shape, lambda *_: (0,) * nd, pipeline_mode=pl.Buffered(1))


def _params(n_axes):
    return pltpu.CompilerParams(dimension_semantics=("arbitrary",) * n_axes, vmem_limit_bytes=VMEM_LIMIT)


def _tile(t_len, want):
    tm = min(want, t_len)
    assert t_len % tm == 0, (t_len, tm)
    return tm


def _ada_kernel(c_ref, w_ref, b_ref, o_ref):
    c = c_ref[...]
    cond = c * _sigmoid(c)
    o_ref[0] = _mm(cond, w_ref[0]) + b_ref[0]


def _ada_mod(c, ada_w, ada_b):
    depth, d, n = ada_w.shape
    bsz = c.shape[0]
    tn = 1536
    assert n % tn == 0
    out = pl.pallas_call(
        _ada_kernel,
        grid=(depth, n // tn),
        in_specs=[pl.BlockSpec((bsz, d), lambda l, j: (0, 0)),
                  pl.BlockSpec((1, d, tn), lambda l, j: (l, 0, j)),
                  pl.BlockSpec((1, 1, tn), lambda l, j: (l, 0, j))],
        out_specs=pl.BlockSpec((1, bsz, tn), lambda l, j: (l, 0, j)),
        out_shape=jax.ShapeDtypeStruct((depth, bsz, n), F32),
        compiler_params=_params(2),
        name="ada_mod",
    )(c, ada_w, ada_b.reshape(depth, 1, n))
    return out.reshape(depth, bsz, 3 * N_SUB, d)


def _ffn_kernel(s, res_ref, mod_ref, ng_ref, wa_ref, wb_ref, wo_ref, out_ref):
    mod = mod_ref[0]
    ng = ng_ref[...]
    f = wa_ref.shape[1]
    cut = -(-f // (2 * MXU_WIDTH)) * MXU_WIDTH
    chunks = [slice(0, cut), slice(cut, f)]
    tm = res_ref.shape[1]
    halves = [slice(i * tm // FFN_ROW_PARTS, (i + 1) * tm // FFN_ROW_PARTS) for i in range(FFN_ROW_PARTS)]
    res = [res_ref[0, rows, :] for rows in halves]
    h = [_sublayer_in(x, mod, ng, s).astype(BF16) for x in res]
    y = [jnp.zeros(x.shape, F32) for x in res]
    for cols in chunks:
        for i in range(len(halves)):
            a = jnp.dot(h[i], wa_ref[:, cols], preferred_element_type=F32)
            b = jnp.dot(h[i], wb_ref[:, cols], preferred_element_type=F32)
            z = (a * _sigmoid(a) * b).astype(BF16)
            y[i] = y[i] + jnp.dot(z, wo_ref[cols, :], preferred_element_type=F32)
    for i, rows in enumerate(halves):
        out_ref[0, rows, :] = _sublayer_out(res[i], y[i], mod, ng, s, 0.5)


def _ffn_sublayer(res, mod, ng, w_a, w_b, w_o, s):
    bsz, t_len, d = res.shape
    f = w_a.shape[1]
    tm = _tile(t_len, FFN_TILE)
    return pl.pallas_call(
        functools.partial(_ffn_kernel, s),
        grid=(bsz, t_len // tm),
        in_specs=[pl.BlockSpec((1, tm, d), lambda b, t: (b, t, 0)),
                  pl.BlockSpec((1, 3 * N_SUB, d), lambda b, t: (b, 0, 0)),
                  _resident(ng.shape), _resident((d, f)), _resident((d, f)), _resident((f, d))],
        out_specs=pl.BlockSpec((1, tm, d), lambda b, t: (b, t, 0)),
        out_shape=jax.ShapeDtypeStruct(res.shape, F32),
        compiler_params=_params(2),
        name="ffn_sublayer",
    )(res, mod, ng, w_a, w_b, w_o)


RWKV_STREAMS = ("r", "lw", "k", "v", "kk", "a", "g")


def _rwkv_pre_stages(has_vres, res, mod, ng_ref, mu_ref, vec_ref, w, vfirst, hext, store):
    tm, d = res.shape
    h = _sublayer_in(res, mod, ng_ref[...], 1)
    hext[SUBLANES:SUBLANES + tm, :] = h
    dx = hext[SUBLANES - 1:SUBLANES - 1 + tm, :] - h
    hext[0:SUBLANES, :] = hext[tm:tm + SUBLANES, :]
    mu = mu_ref[...]
    xr, xw, xk, xv, xa, xg = ((h + dx * mu[j:j + 1]).astype(BF16) for j in range(6))
    vec = vec_ref[...]
    w0, a0, k_k, k_a = vec[0:1], vec[1:2], vec[2:3], vec[3:4]
    full = slice(0, d)
    yield
    tw = jnp.tanh(jnp.dot(xw, w["w1"][...], preferred_element_type=F32))
    ta = jnp.dot(xa, w["a1"][...], preferred_element_type=F32)
    yield
    tg = _sigmoid(jnp.dot(xg, w["g1"][...], preferred_element_type=F32))
    if has_vres:
        tv = jnp.dot(xv, w["v1"][...], preferred_element_type=F32)
    yield
    store("lw", full, -math.exp(-0.5) * _sigmoid(w0 + _mm(tw, w["w2"][...])))
    a = _sigmoid(a0 + _mm(ta, w["a2"][...]))
    store("a", full, a)
    yield
    store("g", full, _mm(tg, w["g2"][...]))
    if has_vres:
        mix = _sigmoid(w["v0"][...] + _mm(tv, w["v2"][...]))
    yield
    for lo in range(0, d, RWKV_GW):
        cols = slice(lo, lo + RWKV_GW)
        store("r", cols, jnp.dot(xr, w["wr"][:, cols], preferred_element_type=F32))
        yield
    for lo in range(0, d, RWKV_GW):
        cols = slice(lo, lo + RWKV_GW)
        k = jnp.dot(xk, w["wk"][:, cols], preferred_element_type=F32)
        store("kk", cols, k * k_k[:, cols])
        store("k", cols, k * (1.0 + (a[:, cols] - 1.0) * k_a[:, cols]))
        yield
    for lo in range(0, d, RWKV_GW):
        cols = slice(lo, lo + RWKV_GW)
        v = jnp.dot(xv, w["wv"][:, cols], preferred_element_type=F32)
        if has_vres:
            v = v + (vfirst[:, cols].astype(F32) - v) * mix[:, cols]
        store("v", cols, v)
        yield


def _pad_cols(w, n):
    return jnp.pad(w, ((0, 0), (0, n - w.shape[1])))


def _pad_rows(w, n):
    return jnp.pad(w, ((0, n - w.shape[0]), (0, 0)))


def _block_diag(x, bd_mask):
    xb = x.astype(BF16)
    return jnp.where(bd_mask, jnp.concatenate([xb] * RWKV_GROUP, axis=0), jnp.zeros((), BF16))


class _Staged:
    def __init__(self, gen):
        self.gen, self.done, self.value = gen, False, None

    def step(self):
        if not self.done:
            try:
                next(self.gen)
            except StopIteration as stop:
                self.done, self.value = True, stop.value

    def finish(self):
        while not self.done:
            self.step()
        return self.value


def _wkv_prepare(raw, masks, consts):
    bd_mask, strict, incl, eye = masks
    tril = consts
    c = RWKV_CHUNK
    bd = lambda x: _block_diag(x, bd_mask)
    cum = [_mm_split_lhs_exact(tril, lw) for r, lw, k, v, kkn, a in raw]
    yield
    units = []
    for (r, lw, k, v, kkn, a), cum_i in zip(raw, cum):
        g_t = jnp.exp(cum_i)
        g_inv = jnp.exp(-cum_i)
        units.append((r * g_t, -kkn * jnp.exp(cum_i - lw), kkn * a * g_inv, k * g_inv, v, g_t[c - 1:c]))
    ar = [jnp.concatenate([at, rt], axis=0).astype(BF16) for rt, at, bt, kt, v, gl in units]
    sv = [bd(u[4]) for u in units]
    lb = [_mm_nt(x, bd(u[2])) for x, u in zip(ar, units)]
    lk = [_mm_nt(x, bd(u[3])) for x, u in zip(ar, units)]
    yield
    l_ab = [jnp.where(strict, x[:c], 0.0) for x in lb]
    l_rb = [jnp.where(incl, x[c:], 0.0) for x in lb]
    l_ak = [jnp.where(strict, x[:c], 0.0) for x in lk]
    l_rk = [jnp.where(incl, x[c:], 0.0) for x in lk]
    pw = [_mm(x, bd(x)) for x in l_ab]
    inv = [eye + x for x in l_ab]
    yield
    for _ in range(4):
        both = [_mm(jnp.concatenate([p, t], axis=0), bd(p)) for p, t in zip(pw, inv)]
        pw = [x[:c] for x in both]
        inv = [t + x[c:] for t, x in zip(inv, both)]
        yield
    inv = [t + _mm(t, bd(p)) for t, p in zip(inv, pw)]
    kv = [_mm(jnp.concatenate([x, y], axis=0), s) for x, y, s in zip(l_ak, l_rk, sv)]
    ak_v = [x[:c] for x in kv]
    rk_v = [x[c:] for x in kv]
    bk = [jnp.concatenate([bt, kt], axis=0).astype(BF16) for rt, at, bt, kt, v, gl in units]
    return [dict(ar=ar[i], inv=inv[i], l_rb=l_rb[i], ak_v=ak_v[i], rk_v=rk_v[i], bk=bk[i], v=units[i][4],
                 g_last=units[i][5]) for i in range(len(units))]


def _wkv_advance(get_prepared, n_chunks, z_states, masks):
    bd_mask = masks[0]
    c = RWKV_CHUNK
    bd = lambda x: _block_diag(x, bd_mask)
    ys = []
    for i in range(n_chunks):
        prepared = get_prepared(i)
        ars = [_mm_nt(p["ar"], z) for p, z in zip(prepared, z_states)]
        yield
        u = [_mm(p["inv"], bd(x[:c] + p["ak_v"])) for p, x in zip(prepared, ars)]
        yield
        y = [x[c:] + _mm(p["l_rb"], bd(uu)) + p["rk_v"] for p, x, uu in zip(prepared, ars, u)]
        dz = [_mm_tn(jnp.concatenate([uu, p["v"]], axis=0), p["bk"]) for p, uu in zip(prepared, u)]
        z_states = [(z + jnp.where(bd_mask, d, 0.0)) * p["g_last"] for z, d, p in zip(z_states, dz, prepared)]
        ys.append(y)
        yield
    return ys, z_states


def _wkv_masks():
    gw, c = RWKV_GW, RWKV_CHUNK
    row = lax.broadcasted_iota(jnp.int32, (gw, gw), 0)
    col = lax.broadcasted_iota(jnp.int32, (gw, gw), 1)
    bd_mask = (row // RWKV_N) == (col // RWKV_N)
    t = lax.broadcasted_iota(jnp.int32, (c, gw), 0)
    j = lax.broadcasted_iota(jnp.int32, (c, gw), 1) % c
    eye = jnp.where(t == j, 1.0, 0.0).astype(F32)
    rc = lax.broadcasted_iota(jnp.int32, (c, c), 0)
    cc = lax.broadcasted_iota(jnp.int32, (c, c), 1)
    tril = jnp.where(rc >= cc, 1.0, 0.0).astype(BF16)
    bd_ones = jnp.where(bd_mask, 1.0, 0.0).astype(BF16)
    return (bd_mask, t > j, t >= j, eye), tril, bd_ones


def _rwkv_kernel(has_vres, emit_vfirst, n_chunks, tiles_per_row, n_tiles, *refs):
    names = ["res_pre", "mod_pre", "res_wkv", "mod_wkv", "ng", "mu", "vec4", "wr", "wk", "wv", "w1", "w2", "a1", "a2",
             "g1", "g2"] + (["v1", "v2", "v0", "vfirst"] if has_vres else []) + ["vec3", "wo", "out"]
    names += (["vfirst_out"] if emit_vfirst else []) + ["hext", "state", "z_buf"] + ["p_" + n for n in RWKV_STREAMS]
    ref = dict(zip(names, refs))
    assert len(names) == len(refs)
    step = pl.program_id(0)
    slot_w = step % 2
    slot_r = 1 - slot_w
    pre_tile = jnp.minimum(step, n_tiles - 1)
    wkv_tile = jnp.maximum(step - 1, 0)
    hext, state, z_buf = ref["hext"], ref["state"], ref["z_buf"]

    @pl.when(step == 0)
    def _():
        for n in RWKV_STREAMS:
            ref["p_" + n][1] = jnp.zeros(ref["p_" + n].shape[1:], ref["p_" + n].dtype)

    @pl.when(pre_tile % tiles_per_row == 0)
    def _():
        hext[0:SUBLANES, :] = jnp.zeros((SUBLANES, hext.shape[1]), F32)

    @pl.when(wkv_tile % tiles_per_row == 0)
    def _():
        state[...] = jnp.zeros(state.shape, F32)

    def store(name, cols, value):
        dst = ref["p_" + name]
        dst[slot_w, :, cols] = value.astype(dst.dtype)

    pre = _Staged(_rwkv_pre_stages(has_vres, ref["res_pre"][0], ref["mod_pre"][0], ref["ng"], ref["mu"], ref["vec4"],
                                   ref, ref["vfirst"][0] if has_vres else None, hext, store))

    c = RWKV_CHUNK
    n_groups = state.shape[0]
    masks, tril, bd_ones = _wkv_masks()
    lanes = [slice(gi * RWKV_GW, (gi + 1) * RWKV_GW) for gi in range(n_groups)]
    tb = n_chunks * c
    vec3 = ref["vec3"]

    def head_sums(per_group):
        s = _mm(jnp.concatenate(per_group, axis=0), bd_ones)
        return [s[gi * tb:(gi + 1) * tb] for gi in range(n_groups)]

    load = lambda n: [ref["p_" + n][slot_r, :, ln].astype(F32) for ln in lanes]
    r, lw, k, v, kk, a = (load(n) for n in RWKV_STREAMS[:6])
    if emit_vfirst:
        ref["vfirst_out"][0] = ref["p_v"][slot_r]
    kkn = [x / jnp.maximum(jnp.sqrt(n2), 1e-12) for x, n2 in zip(kk, head_sums([x * x for x in kk]))]
    bonus = head_sums([r[gi] * k[gi] * vec3[0:1, lanes[gi]] for gi in range(n_groups)])
    pre.step()

    def raw_units(i):
        rows = slice(i * c, (i + 1) * c)
        return [tuple(x[gi][rows] for x in (r, lw, k, v, kkn, a)) for gi in range(n_groups)]

    lead = min(2, n_chunks)
    first_task = _Staged(_wkv_prepare([u for i in range(lead) for u in raw_units(i)], masks, tril))
    while not first_task.done:
        first_task.step()
        pre.step()
    first = first_task.value
    later = [_Staged(_wkv_prepare(raw_units(i), masks, tril)) for i in range(lead, n_chunks)]

    def get_prepared(i):
        if i < lead:
            return first[i * n_groups:(i + 1) * n_groups]
        return later[i - lead].finish()

    advance = _Staged(_wkv_advance(get_prepared, n_chunks, [state[gi] for gi in range(n_groups)], masks))
    while not advance.done:
        advance.step()
        for task in later:
            if not task.done:
                task.step()
                break
        pre.step()
    ys, z_states = advance.value
    for gi in range(n_groups):
        state[gi] = z_states[gi]
    y = [jnp.concatenate([ys[i][gi] for i in range(n_chunks)], axis=0) for gi in range(n_groups)]
    yc = [x - m * (1.0 / RWKV_N) for x, m in zip(y, head_sums(y))]
    pre.step()
    var = head_sums([x * x for x in yc])
    pre.step()
    for gi, ln in enumerate(lanes):
        yn = yc[gi] * lax.rsqrt(var[gi] * (1.0 / RWKV_N) + RWKV_GN_EPS) * vec3[1:2, ln] + vec3[2:3, ln]
        z_buf[:, ln] = ((yn + bonus[gi] * v[gi]) * ref["p_g"][slot_r, :, ln].astype(F32)).astype(BF16)
    pre.finish()
    y_out = jnp.dot(z_buf[...], ref["wo"][...], preferred_element_type=F32)
    ref["out"][0] = _sublayer_out(ref["res_wkv"][0], y_out, ref["mod_wkv"][0], ref["ng"][...], 1, 1.0)


def _rwkv_sublayer(res, mod, ng, p, v_first, emit_vfirst):
    bsz, t_len, d = res.shape
    n_groups = d // RWKV_GW
    tb = _tile(t_len, 256)
    per_row = t_len // tb
    n_tiles = bsz * per_row
    has_vres = v_first is not None
    pre = lambda s: jnp.minimum(s, n_tiles - 1)
    wkv = lambda s: jnp.maximum(s - 1, 0)
    tok_pre = pl.BlockSpec((1, tb, d), lambda s: (pre(s) // per_row, pre(s) % per_row, 0))
    tok_wkv = pl.BlockSpec((1, tb, d), lambda s: (wkv(s) // per_row, wkv(s) % per_row, 0))
    mod_pre = pl.BlockSpec((1, 3 * N_SUB, d), lambda s: (pre(s) // per_row, 0, 0))
    mod_wkv = pl.BlockSpec((1, 3 * N_SUB, d), lambda s: (wkv(s) // per_row, 0, 0))
    lora_in = lambda w: _pad_cols(w, LORA_PAD).astype(BF16)
    lora_out = lambda w: _pad_rows(w, LORA_PAD).astype(BF16)
    vec4 = jnp.stack([p["w0"], p["a0"], p["k_k"], p["k_a"]])
    vec3 = jnp.stack([p["r_k"].reshape(d), p["ln_w"], p["ln_b"]])
    weights = [ng, p["mu"], vec4, p["w_rkv"][0].astype(BF16), p["w_rkv"][1].astype(BF16), p["w_rkv"][2].astype(BF16),
               lora_in(p["w1"]), lora_out(p["w2"]), lora_in(p["a1"]), lora_out(p["a2"]),
               lora_in(p["g1"]), lora_out(p["g2"])]
    args = [res, mod, res, mod] + weights
    in_specs = [tok_pre, mod_pre, tok_wkv, mod_wkv] + [_resident(x.shape) for x in weights]
    if has_vres:
        extra = [lora_in(p["v1"]), lora_out(p["v2"]), p["v0"].reshape(1, d)]
        args += extra + [v_first]
        in_specs += [_resident(x.shape) for x in extra] + [tok_pre]
    tail = [vec3, p["w_o"].astype(BF16)]
    args += tail
    in_specs += [_resident(x.shape) for x in tail]
    out_shape = [jax.ShapeDtypeStruct(res.shape, F32)]
    out_specs = [tok_wkv]
    if emit_vfirst:
        out_shape.append(jax.ShapeDtypeStruct(res.shape, BF16))
        out_specs.append(tok_wkv)
    streams = [pltpu.VMEM((2, tb, d), F32 if n == "lw" else BF16) for n in RWKV_STREAMS]
    outs = pl.pallas_call(
        functools.partial(_rwkv_kernel, has_vres, emit_vfirst, tb // RWKV_CHUNK, per_row, n_tiles),
        grid=(n_tiles + 1,),
        in_specs=in_specs,
        out_specs=out_specs,
        out_shape=out_shape,
        scratch_shapes=[pltpu.VMEM((tb + SUBLANES, d), F32), pltpu.VMEM((n_groups, RWKV_GW, RWKV_GW), F32),
                        pltpu.VMEM((tb, d), BF16)] + streams,
        compiler_params=_params(1),
        name="rwkv_sublayer",
    )(*args)
    return outs[0], (outs[1] if emit_vfirst else None)


def _ret_pre_kernel(res_ref, mod_ref, ng_ref, pos_ref, inv_ref, w_ref, q_ref, k_ref, v_ref, g_ref):
    tm = res_ref.shape[1]
    half = RET_DK // 2
    qk = RET_H * RET_DK
    nv = RET_H * RET_DV
    mod = mod_ref[0]
    ng = ng_ref[...]
    for part in range(RET_PRE_PARTS):
        rows = slice(part * tm // RET_PRE_PARTS, (part + 1) * tm // RET_PRE_PARTS)
        h = _sublayer_in(res_ref[0, rows, :], mod, ng, 1).astype(BF16)
        for hd in range(RET_H):
            lo = 2 * qk + hd * RET_DV
            v_ref[0, rows, hd * RET_DV:(hd + 1) * RET_DV] = jnp.dot(
                h, w_ref[:, lo:lo + RET_DV], preferred_element_type=F32).astype(BF16)
            g_ref[0, rows, hd * RET_DV:(hd + 1) * RET_DV] = jnp.dot(
                h, w_ref[:, lo + nv:lo + nv + RET_DV], preferred_element_type=F32).astype(BF16)
        ang = pos_ref[0, rows, :] * inv_ref[...]
        cos = jnp.cos(ang)
        sin = jnp.sin(ang)
        for which, out_ref, scale in ((0, q_ref, 1.0), (1, k_ref, RET_DK ** -0.5)):
            for hd in range(RET_H):
                lo = which * qk + hd * RET_DK
                x = jnp.dot(h, w_ref[:, lo:lo + RET_DK], preferred_element_type=F32)
                x1, x2 = x[:, :half], x[:, half:]
                out_ref[0, rows, hd * RET_DK:hd * RET_DK + half] = ((x1 * cos - x2 * sin) * scale).astype(BF16)
                out_ref[0, rows, hd * RET_DK + half:(hd + 1) * RET_DK] = ((x1 * sin + x2 * cos) * scale).astype(BF16)


def _ret_pre(res, mod, ng, pos_f, w_in):
    bsz, t_len, d = res.shape
    tm = _tile(t_len, 512)
    qk = RET_H * RET_DK
    nv = RET_H * RET_DV
    half = RET_DK // 2
    inv = (1.0 / (ROPE_BASE ** jnp.linspace(0.0, 1.0, half, dtype=F32))).reshape(1, half)
    tok = lambda n: pl.BlockSpec((1, tm, n), lambda b, t: (b, t, 0))
    return pl.pallas_call(
        _ret_pre_kernel,
        grid=(bsz, t_len // tm),
        in_specs=[tok(d), pl.BlockSpec((1, 3 * N_SUB, d), lambda b, t: (b, 0, 0)), _resident(ng.shape),
                  tok(1), _resident((1, half)), _resident(w_in.shape)],
        out_specs=[tok(qk), tok(qk), tok(nv), tok(nv)],
        out_shape=[jax.ShapeDtypeStruct((bsz, t_len, qk), BF16), jax.ShapeDtypeStruct((bsz, t_len, qk), BF16),
                   jax.ShapeDtypeStruct((bsz, t_len, nv), BF16), jax.ShapeDtypeStruct((bsz, t_len, nv), BF16)],
        compiler_params=_params(2),
        name="ret_pre",
    )(res, mod, ng, pos_f, inv, w_in)


def _ret_mix_kernel(n_chunks, q_ref, k_ref, v_ref, g_ref, res_ref, mod_ref, ng_ref, wo_ref, out_ref, state, z_buf):
    @pl.when(pl.program_id(1) == 0)
    def _():
        state[...] = jnp.zeros(state.shape, F32)

    c = RET_CHUNK
    row = lax.broadcasted_iota(jnp.int32, (c, c), 0)
    col = lax.broadcasted_iota(jnp.int32, (c, c), 1)
    diff = (row - col).astype(F32)
    idx = lax.broadcasted_iota(jnp.int32, (c, 1), 0).astype(F32)
    heads = range(RET_H)
    log_gamma = [math.log(1.0 - 2.0 ** (-5.0 - hd)) for hd in heads]
    inner = [jnp.where(diff >= 0, jnp.exp(lg * jnp.maximum(diff, 0.0)), 0.0) for lg in log_gamma]
    q_decay = [jnp.exp(lg * (idx + 1.0)) for lg in log_gamma]
    k_decay = [jnp.exp(lg * (c - 1.0 - idx)) for lg in log_gamma]
    chunk_decay = [math.exp(lg * c) for lg in log_gamma]
    dk = lambda hd: slice(hd * RET_DK, (hd + 1) * RET_DK)
    dv = lambda hd: slice(hd * RET_DV, (hd + 1) * RET_DV)
    r_state = [state[hd] for hd in heads]
    outs = []
    for i in range(n_chunks):
        rows = slice(i * c, (i + 1) * c)
        q_c = [q_ref[0, rows, dk(hd)] for hd in heads]
        k_c = [k_ref[0, rows, dk(hd)] for hd in heads]
        v_c = [v_ref[0, rows, dv(hd)] for hd in heads]
        s = [_mm_nt(q_c[hd], k_c[hd]) * inner[hd] for hd in heads]
        cross = [_mm(q_c[hd], r_state[hd]) * q_decay[hd] for hd in heads]
        o = [_mm(s[hd], v_c[hd]) + cross[hd] for hd in heads]
        kv = [_mm_tn(k_c[hd].astype(F32) * k_decay[hd], v_c[hd]) for hd in heads]
        r_state = [r_state[hd] * chunk_decay[hd] + kv[hd] for hd in heads]
        outs.append(o)
    for hd in heads:
        state[hd] = r_state[hd]
    mod = mod_ref[0]
    ng = ng_ref[...]
    for i, o in enumerate(outs):
        rows = slice(i * c, (i + 1) * c)
        for hd in heads:
            oc = o[hd] - jnp.mean(o[hd], axis=-1, keepdims=True)
            on = oc * lax.rsqrt(jnp.mean(oc * oc, axis=-1, keepdims=True) + HEAD_NORM_EPS)
            gt = g_ref[0, rows, dv(hd)].astype(F32)
            z_buf[rows, dv(hd)] = (gt * _sigmoid(gt) * on).astype(BF16)
        y = jnp.dot(z_buf[rows, :], wo_ref[...], preferred_element_type=F32)
        out_ref[0, rows, :] = _sublayer_out(res_ref[0, rows, :], y, mod, ng, 1, 1.0)


def _ret_mix(q, k, v, g, res, mod, ng, w_o):
    bsz, t_len, d = res.shape
    tm = _tile(t_len, 512)
    qk = RET_H * RET_DK
    nv = RET_H * RET_DV
    tok = lambda n: pl.BlockSpec((1, tm, n), lambda b, t: (b, t, 0))
    return pl.pallas_call(
        functools.partial(_ret_mix_kernel, tm // RET_CHUNK),
        grid=(bsz, t_len // tm),
        in_specs=[tok(qk), tok(qk), tok(nv), tok(nv), tok(d),
                  pl.BlockSpec((1, 3 * N_SUB, d), lambda b, t: (b, 0, 0)), _resident(ng.shape), _resident(w_o.shape)],
        out_specs=tok(d),
        out_shape=jax.ShapeDtypeStruct(res.shape, F32),
        scratch_shapes=[pltpu.VMEM((RET_H, RET_DK, RET_DV), F32), pltpu.VMEM((tm, nv), BF16)],
        compiler_params=_params(2),
        name="ret_mix",
    )(q, k, v, g, res, mod, ng, w_o)


def _lru_kernel(res_ref, mod_ref, ng_ref, win_ref, cw_ref, vec_ref, gw_ref, wo_ref, out_ref, xext, h_carry):
    tm = res_ref.shape[1]
    width = xext.shape[1]

    @pl.when(pl.program_id(1) == 0)
    def _():
        xext[0:SUBLANES, :] = jnp.zeros((SUBLANES, width), F32)
        h_carry[...] = jnp.zeros(h_carry.shape, F32)

    mod = mod_ref[0]
    ng = ng_ref[...]
    cw = cw_ref[...]
    vec = vec_ref[...]
    conv_b, gate_bi, gate_br, lam = vec[0:1], vec[1:2], vec[2:3], vec[3:4]
    neg_lam = -lam
    softplus = jnp.maximum(neg_lam, 0.0) + jnp.log1p(jnp.exp(-jnp.abs(neg_lam)))
    parts = [slice(i * tm // LRU_ROW_PARTS, (i + 1) * tm // LRU_ROW_PARTS) for i in range(LRU_ROW_PARTS)]
    res = [res_ref[0, rows, :] for rows in parts]
    h = [_sublayer_in(x, mod, ng, 1).astype(BF16) for x in res]
    gate_branch = [jnp.dot(x, win_ref[:, :width], preferred_element_type=F32) for x in h]
    xb = [jnp.dot(x, win_ref[:, width:], preferred_element_type=F32) for x in h]
    for rows, x in zip(parts, xb):
        xext[SUBLANES + rows.start:SUBLANES + rows.stop, :] = x
    xc = []
    for rows, x in zip(parts, xb):
        acc = conv_b + cw[CONV_W - 1:CONV_W] * x
        for j in range(CONV_W - 1):
            lo = SUBLANES - (CONV_W - 1) + j
            acc = acc + cw[j:j + 1] * xext[lo + rows.start:lo + rows.stop, :]
        xc.append(acc)
    xext[0:SUBLANES, :] = xext[tm:tm + SUBLANES, :]

    a, u = [], []
    for x in xc:
        xcb = x.astype(BF16)
        gates = [jnp.concatenate([jnp.dot(xcb[:, hd * LRU_BW:(hd + 1) * LRU_BW], gw_ref[gi * LRU_H + hd],
                                          preferred_element_type=F32) for hd in range(LRU_H)], axis=1)
                 for gi in range(2)]
        i_gate = _sigmoid(gates[0] + gate_bi)
        r_gate = _sigmoid(gates[1] + gate_br)
        log_a = -LRU_C * r_gate * softplus
        a_p = jnp.exp(log_a)
        a.append(a_p)
        u.append(jnp.sqrt(-jnp.tanh(log_a) * (a_p * a_p + 1.0)) * (i_gate * x))

    row = lax.broadcasted_iota(jnp.int32, (SUBLANES, width), 0)
    carry = h_carry[...]
    hs = []
    for a_p, u_p in zip(a, u):
        groups = []
        for i in range(a_p.shape[0] // SUBLANES):
            ag = a_p[i * SUBLANES:(i + 1) * SUBLANES]
            ug = u_p[i * SUBLANES:(i + 1) * SUBLANES]
            for d in (1, 2, 4):
                keep = row >= d
                u_prev = jnp.where(keep, pltpu.roll(ug, d, 0), 0.0)
                a_prev = jnp.where(keep, pltpu.roll(ag, d, 0), 1.0)
                ug = ug + ag * u_prev
                ag = ag * a_prev
            hg = ug + ag * carry
            groups.append(hg)
            carry = jnp.broadcast_to(hg[SUBLANES - 1:SUBLANES, :], (SUBLANES, width))
        hs.append(jnp.concatenate(groups, axis=0))
    h_carry[...] = carry

    for rows, x, gb, hp in zip(parts, res, gate_branch, hs):
        gelu = 0.5 * gb * (1.0 + jnp.tanh(math.sqrt(2.0 / math.pi) * (gb + 0.044715 * (gb * gb * gb))))
        y = jnp.dot((gelu * hp).astype(BF16), wo_ref[...], preferred_element_type=F32)
        out_ref[0, rows, :] = _sublayer_out(x, y, mod, ng, 1, 1.0)


def _lru_sublayer(res, mod, ng, w_in, conv_w, vec, gate_w, w_o):
    bsz, t_len, d = res.shape
    width = w_o.shape[0]
    tm = _tile(t_len, LRU_TILE)
    tok = pl.BlockSpec((1, tm, d), lambda b, t: (b, t, 0))
    return pl.pallas_call(
        _lru_kernel,
        grid=(bsz, t_len // tm),
        in_specs=[tok, pl.BlockSpec((1, 3 * N_SUB, d), lambda b, t: (b, 0, 0)), _resident(ng.shape),
                  _resident(w_in.shape), _resident(conv_w.shape), _resident(vec.shape),
                  _resident(gate_w.shape), _resident(w_o.shape)],
        out_specs=tok,
        out_shape=jax.ShapeDtypeStruct(res.shape, F32),
        scratch_shapes=[pltpu.VMEM((tm + SUBLANES, width), F32), pltpu.VMEM((SUBLANES, width), F32)],
        compiler_params=_params(2),
        name="lru_sublayer",
    )(res, mod, ng, w_in, conv_w, vec, gate_w, w_o)


def kernel(x, c, positions, ada_w, ada_b, norm_g, ffn_w_in, ffn_w_out, rwkv_mu, rwkv_w_rkv, rwkv_w0, rwkv_w1, rwkv_w2, rwkv_a0, rwkv_a1, rwkv_a2, rwkv_g1, rwkv_g2, rwkv_k_k, rwkv_k_a, rwkv_r_k, rwkv_ln_w, rwkv_ln_b, rwkv_w_o, rwkv_v0, rwkv_v1, rwkv_v2, ret_w_in, ret_w_o, lru_w_in, lru_conv_w, lru_conv_b, lru_gate_w, lru_gate_b, lru_lambda, lru_w_o):
    depth = ada_w.shape[0]
    d_ff = ffn_w_out.shape[2]
    res = x.astype(F32)
    mod = _ada_mod(c.astype(F32), ada_w, ada_b)
    pos_f = positions.astype(F32)[..., None]
    v_first = None
    for i in range(depth):
        ng = norm_g[i]
        ffn = lambda res, m, s: _ffn_sublayer(
            res, mod[i], ng, ffn_w_in[i, m, :, :d_ff].astype(BF16), ffn_w_in[i, m, :, d_ff:].astype(BF16),
            ffn_w_out[i, m].astype(BF16), s)
        res = ffn(res, 0, 0)
        kind, j = i % 3, i // 3
        if kind == 0:
            p = dict(mu=rwkv_mu[j], w_rkv=rwkv_w_rkv[j], w0=rwkv_w0[j], w1=rwkv_w1[j], w2=rwkv_w2[j],
                     a0=rwkv_a0[j], a1=rwkv_a1[j], a2=rwkv_a2[j], g1=rwkv_g1[j], g2=rwkv_g2[j],
                     k_k=rwkv_k_k[j], k_a=rwkv_k_a[j], r_k=rwkv_r_k[j], ln_w=rwkv_ln_w[j], ln_b=rwkv_ln_b[j])
            if j > 0:
                p.update(v0=rwkv_v0[j - 1], v1=rwkv_v1[j - 1], v2=rwkv_v2[j - 1])
            p["w_o"] = rwkv_w_o[j]
            res, v_out = _rwkv_sublayer(res, mod[i], ng, p, v_first if j > 0 else None, j == 0)
            if j == 0:
                v_first = v_out
        elif kind == 1:
            q, k, v, g = _ret_pre(res, mod[i], ng, pos_f, ret_w_in[j].astype(BF16))
            res = _ret_mix(q, k, v, g, res, mod[i], ng, ret_w_o[j].astype(BF16))
        else:
            width = lru_w_o.shape[1]
            vec = jnp.stack([lru_conv_b[j], lru_gate_b[j, 0].reshape(width), lru_gate_b[j, 1].reshape(width),
                             lru_lambda[j]])
            gate_w = lru_gate_w[j].reshape(2 * LRU_H, LRU_BW, LRU_BW).astype(BF16)
            res = _lru_sublayer(res, mod[i], ng, lru_w_in[j].astype(BF16), lru_conv_w[j], vec, gate_w,
                                lru_w_o[j].astype(BF16))
        res = ffn(res, 1, 2)
    return res.astype(x.dtype)
```

```python
import functools
import math

import jax
import jax.numpy as jnp
from jax import lax
from jax.experimental import pallas as pl
from jax.experimental.pallas import tpu as pltpu

F32 = jnp.float32
BF16 = jnp.bfloat16

NORM_EPS = 1e-6
N_SUB = 3

MXU_WIDTH = 256
SUBLANES = 8

RWKV_N = 64
RWKV_GROUP = MXU_WIDTH // RWKV_N
RWKV_GW = RWKV_GROUP * RWKV_N
RWKV_CHUNK = 64
RWKV_GN_EPS = 64e-5
LORA_PAD = 128

RET_H = 4
RET_DK = 256
RET_DV = 512
RET_CHUNK = 256
ROPE_BASE = 10000.0
HEAD_NORM_EPS = 1e-5

LRU_H = 5
LRU_BW = 256
CONV_W = 4
LRU_C = 8.0

VMEM_LIMIT = 56 * 1024 * 1024
ADA_TILE = 1536
FFN_TILE = 1024
FFN_ROW_PARTS = 4
RWKV_TILE = 256
RET_TILE = 512
RET_PRE_PARTS = 2
LRU_TILE = 512
LRU_ROW_PARTS = 4

NT_DIMS = (((1,), (1,)), ((), ()))
TN_DIMS = (((0,), (0,)), ((), ()))


def _mm(a, b):
    return jnp.dot(a.astype(BF16), b.astype(BF16), preferred_element_type=F32)


def _mm_nt(a, b):
    return lax.dot_general(a.astype(BF16), b.astype(BF16), NT_DIMS, preferred_element_type=F32)


def _mm_tn(a, b):
    return lax.dot_general(a.astype(BF16), b.astype(BF16), TN_DIMS, preferred_element_type=F32)


def _mm_split_lhs_exact(w, x):
    hi = x.astype(BF16)
    lo = (x - hi.astype(F32)).astype(BF16)
    return (jnp.dot(w, hi, preferred_element_type=F32) + jnp.dot(w, lo, preferred_element_type=F32))


def _rms(x, g):
    return x * lax.rsqrt(jnp.mean(x * x, axis=-1, keepdims=True) + NORM_EPS) * g


def _sublayer_in(res, mod, ng, s):
    return _rms(res, ng[2 * s:2 * s + 1]) * (1.0 + mod[3 * s + 1:3 * s + 2]) + mod[3 * s:3 * s + 1]


def _sublayer_out(res, y, mod, ng, s, weight):
    return res + weight * mod[3 * s + 2:3 * s + 3] * _rms(y, ng[2 * s + 1:2 * s + 2])


def _sigmoid(x):
    return jax.nn.sigmoid(x)


def _resident(shape):
    nd = len(shape)
    return pl.BlockSpec(shape, lambda *_: (0,) * nd, pipeline_mode=pl.Buffered(1))


def _params(n_axes):
    return pltpu.CompilerParams(dimension_semantics=("arbitrary",) * n_axes, vmem_limit_bytes=VMEM_LIMIT)


def _tile(t_len, want):
    tm = min(want, t_len)
    assert t_len % tm == 0, (t_len, tm)
    return tm


def _ada_kernel(c_ref, w_ref, b_ref, o_ref):
    c = c_ref[...]
    cond = c * _sigmoid(c)
    o_ref[0] = _mm(cond, w_ref[0]) + b_ref[0]


def _ada_mod(c, ada_w, ada_b):
    depth, d, n = ada_w.shape
    bsz = c.shape[0]
    tn = ADA_TILE
    assert n % tn == 0
    out = pl.pallas_call(
        _ada_kernel,
        grid=(depth, n // tn),
        in_specs=[pl.BlockSpec((bsz, d), lambda l, j: (0, 0)),
                  pl.BlockSpec((1, d, tn), lambda l, j: (l, 0, j)),
                  pl.BlockSpec((1, 1, tn), lambda l, j: (l, 0, j))],
        out_specs=pl.BlockSpec((1, bsz, tn), lambda l, j: (l, 0, j)),
        out_shape=jax.ShapeDtypeStruct((depth, bsz, n), F32),
        compiler_params=_params(2),
        name="ada_mod",
    )(c, ada_w, ada_b.reshape(depth, 1, n))
    return out.reshape(depth, bsz, 3 * N_SUB, d)


def _ffn_kernel(s, res_ref, mod_ref, ng_ref, wa_ref, wb_ref, wo_ref, out_ref):
    mod = mod_ref[0]
    ng = ng_ref[...]
    f = wa_ref.shape[1]
    cut = -(-f // (2 * MXU_WIDTH)) * MXU_WIDTH
    chunks = [slice(0, cut), slice(cut, f)]
    tm = res_ref.shape[1]
    parts = [slice(i * tm // FFN_ROW_PARTS, (i + 1) * tm // FFN_ROW_PARTS) for i in range(FFN_ROW_PARTS)]
    res = [res_ref[0, rows, :] for rows in parts]
    h = [_sublayer_in(x, mod, ng, s).astype(BF16) for x in res]
    y = [jnp.zeros(x.shape, F32) for x in res]
    for cols in chunks:
        for i in range(len(parts)):
            a = jnp.dot(h[i], wa_ref[:, cols], preferred_element_type=F32)
            b = jnp.dot(h[i], wb_ref[:, cols], preferred_element_type=F32)
            z = (a * _sigmoid(a) * b).astype(BF16)
            y[i] = y[i] + jnp.dot(z, wo_ref[cols, :], preferred_element_type=F32)
    for i, rows in enumerate(parts):
        out_ref[0, rows, :] = _sublayer_out(res[i], y[i], mod, ng, s, 0.5)


def _ffn_sublayer(res, mod, ng, w_a, w_b, w_o, s):
    bsz, t_len, d = res.shape
    f = w_a.shape[1]
    tm = _tile(t_len, FFN_TILE)
    return pl.pallas_call(
        functools.partial(_ffn_kernel, s),
        grid=(bsz, t_len // tm),
        in_specs=[pl.BlockSpec((1, tm, d), lambda b, t: (b, t, 0)),
                  pl.BlockSpec((1, 3 * N_SUB, d), lambda b, t: (b, 0, 0)),
                  _resident(ng.shape), _resident((d, f)), _resident((d, f)), _resident((f, d))],
        out_specs=pl.BlockSpec((1, tm, d), lambda b, t: (b, t, 0)),
        out_shape=jax.ShapeDtypeStruct(res.shape, F32),
        compiler_params=_params(2),
        name="ffn_sublayer",
    )(res, mod, ng, w_a, w_b, w_o)


RWKV_STREAMS = ("r", "lw", "k", "v", "kk", "a", "g")


def _rwkv_pre_stages(has_vres, res, mod, ng_ref, mu_ref, vec_ref, w, vfirst, hext, store):
    tm, d = res.shape
    h = _sublayer_in(res, mod, ng_ref[...], 1)
    hext[SUBLANES:SUBLANES + tm, :] = h
    dx = hext[SUBLANES - 1:SUBLANES - 1 + tm, :] - h
    hext[0:SUBLANES, :] = hext[tm:tm + SUBLANES, :]
    mu = mu_ref[...]
    xr, xw, xk, xv, xa, xg = ((h + dx * mu[j:j + 1]).astype(BF16) for j in range(6))
    vec = vec_ref[...]
    w0, a0, k_k, k_a = vec[0:1], vec[1:2], vec[2:3], vec[3:4]
    full = slice(0, d)
    yield
    tw = jnp.tanh(jnp.dot(xw, w["w1"][...], preferred_element_type=F32))
    ta = jnp.dot(xa, w["a1"][...], preferred_element_type=F32)
    yield
    tg = _sigmoid(jnp.dot(xg, w["g1"][...], preferred_element_type=F32))
    if has_vres:
        tv = jnp.dot(xv, w["v1"][...], preferred_element_type=F32)
    yield
    store("lw", full, -math.exp(-0.5) * _sigmoid(w0 + _mm(tw, w["w2"][...])))
    a = _sigmoid(a0 + _mm(ta, w["a2"][...]))
    store("a", full, a)
    yield
    store("g", full, _mm(tg, w["g2"][...]))
    if has_vres:
        mix = _sigmoid(w["v0"][...] + _mm(tv, w["v2"][...]))
    yield
    for lo in range(0, d, RWKV_GW):
        cols = slice(lo, lo + RWKV_GW)
        store("r", cols, jnp.dot(xr, w["wr"][:, cols], preferred_element_type=F32))
        yield
    for lo in range(0, d, RWKV_GW):
        cols = slice(lo, lo + RWKV_GW)
        k = jnp.dot(xk, w["wk"][:, cols], preferred_element_type=F32)
        store("kk", cols, k * k_k[:, cols])
        store("k", cols, k * (1.0 + (a[:, cols] - 1.0) * k_a[:, cols]))
        yield
    for lo in range(0, d, RWKV_GW):
        cols = slice(lo, lo + RWKV_GW)
        v = jnp.dot(xv, w["wv"][:, cols], preferred_element_type=F32)
        if has_vres:
            v = v + (vfirst[:, cols].astype(F32) - v) * mix[:, cols]
        store("v", cols, v)
        yield


def _pad_cols(w, n):
    return jnp.pad(w, ((0, 0), (0, n - w.shape[1])))


def _pad_rows(w, n):
    return jnp.pad(w, ((0, n - w.shape[0]), (0, 0)))


def _block_diag(x, bd_mask):
    xb = x.astype(BF16)
    return jnp.where(bd_mask, jnp.concatenate([xb] * RWKV_GROUP, axis=0), jnp.zeros((), BF16))


class _Staged:
    def __init__(self, gen):
        self.gen, self.done, self.value = gen, False, None

    def step(self):
        if not self.done:
            try:
                next(self.gen)
            except StopIteration as stop:
                self.done, self.value = True, stop.value

    def finish(self):
        while not self.done:
            self.step()
        return self.value


def _wkv_prepare(raw, masks, consts):
    bd_mask, strict, incl, eye = masks
    tril = consts
    c = RWKV_CHUNK
    bd = lambda x: _block_diag(x, bd_mask)
    cum = [_mm_split_lhs_exact(tril, lw) for r, lw, k, v, kkn, a in raw]
    yield
    units = []
    for (r, lw, k, v, kkn, a), cum_i in zip(raw, cum):
        g_t = jnp.exp(cum_i)
        g_inv = jnp.exp(-cum_i)
        units.append((r * g_t, -kkn * jnp.exp(cum_i - lw), kkn * a * g_inv, k * g_inv, v, g_t[c - 1:c]))
    ar = [jnp.concatenate([at, rt], axis=0).astype(BF16) for rt, at, bt, kt, v, gl in units]
    sv = [bd(u[4]) for u in units]
    lb = [_mm_nt(x, bd(u[2])) for x, u in zip(ar, units)]
    lk = [_mm_nt(x, bd(u[3])) for x, u in zip(ar, units)]
    yield
    l_ab = [jnp.where(strict, x[:c], 0.0) for x in lb]
    l_rb = [jnp.where(incl, x[c:], 0.0) for x in lb]
    l_ak = [jnp.where(strict, x[:c], 0.0) for x in lk]
    l_rk = [jnp.where(incl, x[c:], 0.0) for x in lk]
    pw = [_mm(x, bd(x)) for x in l_ab]
    inv = [eye + x for x in l_ab]
    yield
    for _ in range(4):
        both = [_mm(jnp.concatenate([p, t], axis=0), bd(p)) for p, t in zip(pw, inv)]
        pw = [x[:c] for x in both]
        inv = [t + x[c:] for t, x in zip(inv, both)]
        yield
    inv = [t + _mm(t, bd(p)) for t, p in zip(inv, pw)]
    kv = [_mm(jnp.concatenate([x, y], axis=0), s) for x, y, s in zip(l_ak, l_rk, sv)]
    ak_v = [x[:c] for x in kv]
    rk_v = [x[c:] for x in kv]
    bk = [jnp.concatenate([bt, kt], axis=0).astype(BF16) for rt, at, bt, kt, v, gl in units]
    return [dict(ar=ar[i], inv=inv[i], l_rb=l_rb[i], ak_v=ak_v[i], rk_v=rk_v[i], bk=bk[i], v=units[i][4],
                 g_last=units[i][5]) for i in range(len(units))]


def _wkv_advance(get_prepared, n_chunks, z_states, masks):
    bd_mask = masks[0]
    c = RWKV_CHUNK
    bd = lambda x: _block_diag(x, bd_mask)
    ys = []
    for i in range(n_chunks):
        prepared = get_prepared(i)
        ars = [_mm_nt(p["ar"], z) for p, z in zip(prepared, z_states)]
        yield
        u = [_mm(p["inv"], bd(x[:c] + p["ak_v"])) for p, x in zip(prepared, ars)]
        yield
        y = [x[c:] + _mm(p["l_rb"], bd(uu)) + p["rk_v"] for p, x, uu in zip(prepared, ars, u)]
        dz = [_mm_tn(jnp.concatenate([uu, p["v"]], axis=0), p["bk"]) for p, uu in zip(prepared, u)]
        z_states = [(z + jnp.where(bd_mask, d, 0.0)) * p["g_last"] for z, d, p in zip(z_states, dz, prepared)]
        ys.append(y)
        yield
    return ys, z_states


def _wkv_masks():
    gw, c = RWKV_GW, RWKV_CHUNK
    row = lax.broadcasted_iota(jnp.int32, (gw, gw), 0)
    col = lax.broadcasted_iota(jnp.int32, (gw, gw), 1)
    bd_mask = (row // RWKV_N) == (col // RWKV_N)
    t = lax.broadcasted_iota(jnp.int32, (c, gw), 0)
    j = lax.broadcasted_iota(jnp.int32, (c, gw), 1) % c
    eye = jnp.where(t == j, 1.0, 0.0).astype(F32)
    rc = lax.broadcasted_iota(jnp.int32, (c, c), 0)
    cc = lax.broadcasted_iota(jnp.int32, (c, c), 1)
    tril = jnp.where(rc >= cc, 1.0, 0.0).astype(BF16)
    bd_ones = jnp.where(bd_mask, 1.0, 0.0).astype(BF16)
    return (bd_mask, t > j, t >= j, eye), tril, bd_ones


def _rwkv_kernel(has_vres, emit_vfirst, n_chunks, tiles_per_row, n_tiles, *refs):
    names = ["res_pre", "mod_pre", "res_wkv", "mod_wkv", "ng", "mu", "vec4", "wr", "wk", "wv", "w1", "w2", "a1", "a2",
             "g1", "g2"] + (["v1", "v2", "v0", "vfirst"] if has_vres else []) + ["vec3", "wo", "out"]
    names += (["vfirst_out"] if emit_vfirst else []) + ["hext", "state", "z_buf"] + ["p_" + n for n in RWKV_STREAMS]
    ref = dict(zip(names, refs))
    assert len(names) == len(refs)
    step = pl.program_id(0)
    slot_w = step % 2
    slot_r = 1 - slot_w
    pre_tile = jnp.minimum(step, n_tiles - 1)
    wkv_tile = jnp.maximum(step - 1, 0)
    hext, state, z_buf = ref["hext"], ref["state"], ref["z_buf"]

    @pl.when(step == 0)
    def _():
        for n in RWKV_STREAMS:
            ref["p_" + n][1] = jnp.zeros(ref["p_" + n].shape[1:], ref["p_" + n].dtype)

    @pl.when(pre_tile % tiles_per_row == 0)
    def _():
        hext[0:SUBLANES, :] = jnp.zeros((SUBLANES, hext.shape[1]), F32)

    @pl.when(wkv_tile % tiles_per_row == 0)
    def _():
        state[...] = jnp.zeros(state.shape, F32)

    def store(name, cols, value):
        dst = ref["p_" + name]
        dst[slot_w, :, cols] = value.astype(dst.dtype)

    pre = _Staged(_rwkv_pre_stages(has_vres, ref["res_pre"][0], ref["mod_pre"][0], ref["ng"], ref["mu"], ref["vec4"],
                                   ref, ref["vfirst"][0] if has_vres else None, hext, store))

    c = RWKV_CHUNK
    n_groups = state.shape[0]
    masks, tril, bd_ones = _wkv_masks()
    lanes = [slice(gi * RWKV_GW, (gi + 1) * RWKV_GW) for gi in range(n_groups)]
    tb = n_chunks * c
    vec3 = ref["vec3"]

    def head_sums(per_group):
        s = _mm(jnp.concatenate(per_group, axis=0), bd_ones)
        return [s[gi * tb:(gi + 1) * tb] for gi in range(n_groups)]

    load = lambda n: [ref["p_" + n][slot_r, :, ln].astype(F32) for ln in lanes]
    r, lw, k, v, kk, a = (load(n) for n in RWKV_STREAMS[:6])
    if emit_vfirst:
        ref["vfirst_out"][0] = ref["p_v"][slot_r]
    kkn = [x / jnp.maximum(jnp.sqrt(n2), 1e-12) for x, n2 in zip(kk, head_sums([x * x for x in kk]))]
    bonus = head_sums([r[gi] * k[gi] * vec3[0:1, lanes[gi]] for gi in range(n_groups)])
    pre.step()

    def raw_units(i):
        rows = slice(i * c, (i + 1) * c)
        return [tuple(x[gi][rows] for x in (r, lw, k, v, kkn, a)) for gi in range(n_groups)]

    lead = min(2, n_chunks)
    first_task = _Staged(_wkv_prepare([u for i in range(lead) for u in raw_units(i)], masks, tril))
    while not first_task.done:
        first_task.step()
        pre.step()
    first = first_task.value
    later = [_Staged(_wkv_prepare(raw_units(i), masks, tril)) for i in range(lead, n_chunks)]

    def get_prepared(i):
        if i < lead:
            return first[i * n_groups:(i + 1) * n_groups]
        return later[i - lead].finish()

    advance = _Staged(_wkv_advance(get_prepared, n_chunks, [state[gi] for gi in range(n_groups)], masks))
    while not advance.done:
        advance.step()
        for task in later:
            if not task.done:
                task.step()
                break
        pre.step()
    ys, z_states = advance.value
    for gi in range(n_groups):
        state[gi] = z_states[gi]
    y = [jnp.concatenate([ys[i][gi] for i in range(n_chunks)], axis=0) for gi in range(n_groups)]
    yc = [x - m * (1.0 / RWKV_N) for x, m in zip(y, head_sums(y))]
    pre.step()
    var = head_sums([x * x for x in yc])
    pre.step()
    for gi, ln in enumerate(lanes):
        yn = yc[gi] * lax.rsqrt(var[gi] * (1.0 / RWKV_N) + RWKV_GN_EPS) * vec3[1:2, ln] + vec3[2:3, ln]
        z_buf[:, ln] = ((yn + bonus[gi] * v[gi]) * ref["p_g"][slot_r, :, ln].astype(F32)).astype(BF16)
    pre.finish()
    y_out = jnp.dot(z_buf[...], ref["wo"][...], preferred_element_type=F32)
    ref["out"][0] = _sublayer_out(ref["res_wkv"][0], y_out, ref["mod_wkv"][0], ref["ng"][...], 1, 1.0)


def _rwkv_sublayer(res, mod, ng, p, v_first, emit_vfirst):
    bsz, t_len, d = res.shape
    n_groups = d // RWKV_GW
    tb = _tile(t_len, RWKV_TILE)
    per_row = t_len // tb
    n_tiles = bsz * per_row
    has_vres = v_first is not None
    pre = lambda s: jnp.minimum(s, n_tiles - 1)
    wkv = lambda s: jnp.maximum(s - 1, 0)
    tok_pre = pl.BlockSpec((1, tb, d), lambda s: (pre(s) // per_row, pre(s) % per_row, 0))
    tok_wkv = pl.BlockSpec((1, tb, d), lambda s: (wkv(s) // per_row, wkv(s) % per_row, 0))
    mod_pre = pl.BlockSpec((1, 3 * N_SUB, d), lambda s: (pre(s) // per_row, 0, 0))
    mod_wkv = pl.BlockSpec((1, 3 * N_SUB, d), lambda s: (wkv(s) // per_row, 0, 0))
    lora_in = lambda w: _pad_cols(w, LORA_PAD).astype(BF16)
    lora_out = lambda w: _pad_rows(w, LORA_PAD).astype(BF16)
    vec4 = jnp.stack([p["w0"], p["a0"], p["k_k"], p["k_a"]])
    vec3 = jnp.stack([p["r_k"].reshape(d), p["ln_w"], p["ln_b"]])
    weights = [ng, p["mu"], vec4, p["w_rkv"][0].astype(BF16), p["w_rkv"][1].astype(BF16), p["w_rkv"][2].astype(BF16),
               lora_in(p["w1"]), lora_out(p["w2"]), lora_in(p["a1"]), lora_out(p["a2"]),
               lora_in(p["g1"]), lora_out(p["g2"])]
    args = [res, mod, res, mod] + weights
    in_specs = [tok_pre, mod_pre, tok_wkv, mod_wkv] + [_resident(x.shape) for x in weights]
    if has_vres:
        extra = [lora_in(p["v1"]), lora_out(p["v2"]), p["v0"].reshape(1, d)]
        args += extra + [v_first]
        in_specs += [_resident(x.shape) for x in extra] + [tok_pre]
    tail = [vec3, p["w_o"].astype(BF16)]
    args += tail
    in_specs += [_resident(x.shape) for x in tail]
    out_shape = [jax.ShapeDtypeStruct(res.shape, F32)]
    out_specs = [tok_wkv]
    if emit_vfirst:
        out_shape.append(jax.ShapeDtypeStruct(res.shape, BF16))
        out_specs.append(tok_wkv)
    streams = [pltpu.VMEM((2, tb, d), F32 if n == "lw" else BF16) for n in RWKV_STREAMS]
    outs = pl.pallas_call(
        functools.partial(_rwkv_kernel, has_vres, emit_vfirst, tb // RWKV_CHUNK, per_row, n_tiles),
        grid=(n_tiles + 1,),
        in_specs=in_specs,
        out_specs=out_specs,
        out_shape=out_shape,
        scratch_shapes=[pltpu.VMEM((tb + SUBLANES, d), F32), pltpu.VMEM((n_groups, RWKV_GW, RWKV_GW), F32),
                        pltpu.VMEM((tb, d), BF16)] + streams,
        compiler_params=_params(1),
        name="rwkv_sublayer",
    )(*args)
    return outs[0], (outs[1] if emit_vfirst else None)


def _ret_pre_kernel(res_ref, mod_ref, ng_ref, pos_ref, inv_ref, w_ref, q_ref, k_ref, v_ref, g_ref):
    tm = res_ref.shape[1]
    half = RET_DK // 2
    qk = RET_H * RET_DK
    nv = RET_H * RET_DV
    mod = mod_ref[0]
    ng = ng_ref[...]
    for part in range(RET_PRE_PARTS):
        rows = slice(part * tm // RET_PRE_PARTS, (part + 1) * tm // RET_PRE_PARTS)
        h = _sublayer_in(res_ref[0, rows, :], mod, ng, 1).astype(BF16)
        for hd in range(RET_H):
            lo = 2 * qk + hd * RET_DV
            v_ref[0, rows, hd * RET_DV:(hd + 1) * RET_DV] = jnp.dot(
                h, w_ref[:, lo:lo + RET_DV], preferred_element_type=F32).astype(BF16)
            g_ref[0, rows, hd * RET_DV:(hd + 1) * RET_DV] = jnp.dot(
                h, w_ref[:, lo + nv:lo + nv + RET_DV], preferred_element_type=F32).astype(BF16)
        ang = pos_ref[0, rows, :] * inv_ref[...]
        cos = jnp.cos(ang)
        sin = jnp.sin(ang)
        for which, out_ref, scale in ((0, q_ref, 1.0), (1, k_ref, RET_DK ** -0.5)):
            for hd in range(RET_H):
                lo = which * qk + hd * RET_DK
                x = jnp.dot(h, w_ref[:, lo:lo + RET_DK], preferred_element_type=F32)
                x1, x2 = x[:, :half], x[:, half:]
                out_ref[0, rows, hd * RET_DK:hd * RET_DK + half] = ((x1 * cos - x2 * sin) * scale).astype(BF16)
                out_ref[0, rows, hd * RET_DK + half:(hd + 1) * RET_DK] = ((x1 * sin + x2 * cos) * scale).astype(BF16)


def _ret_pre(res, mod, ng, pos_f, w_in):
    bsz, t_len, d = res.shape
    tm = _tile(t_len, RET_TILE)
    qk = RET_H * RET_DK
    nv = RET_H * RET_DV
    half = RET_DK // 2
    inv = (1.0 / (ROPE_BASE ** jnp.linspace(0.0, 1.0, half, dtype=F32))).reshape(1, half)
    tok = lambda n: pl.BlockSpec((1, tm, n), lambda b, t: (b, t, 0))
    return pl.pallas_call(
        _ret_pre_kernel,
        grid=(bsz, t_len // tm),
        in_specs=[tok(d), pl.BlockSpec((1, 3 * N_SUB, d), lambda b, t: (b, 0, 0)), _resident(ng.shape),
                  tok(1), _resident((1, half)), _resident(w_in.shape)],
        out_specs=[tok(qk), tok(qk), tok(nv), tok(nv)],
        out_shape=[jax.ShapeDtypeStruct((bsz, t_len, qk), BF16), jax.ShapeDtypeStruct((bsz, t_len, qk), BF16),
                   jax.ShapeDtypeStruct((bsz, t_len, nv), BF16), jax.ShapeDtypeStruct((bsz, t_len, nv), BF16)],
        compiler_params=_params(2),
        name="ret_pre",
    )(res, mod, ng, pos_f, inv, w_in)


def _ret_mix_kernel(n_chunks, q_ref, k_ref, v_ref, g_ref, res_ref, mod_ref, ng_ref, wo_ref, out_ref, state, z_buf):
    @pl.when(pl.program_id(1) == 0)
    def _():
        state[...] = jnp.zeros(state.shape, F32)

    c = RET_CHUNK
    row = lax.broadcasted_iota(jnp.int32, (c, c), 0)
    col = lax.broadcasted_iota(jnp.int32, (c, c), 1)
    diff = (row - col).astype(F32)
    idx = lax.broadcasted_iota(jnp.int32, (c, 1), 0).astype(F32)
    heads = range(RET_H)
    log_gamma = [math.log(1.0 - 2.0 ** (-5.0 - hd)) for hd in heads]
    inner = [jnp.where(diff >= 0, jnp.exp(lg * jnp.maximum(diff, 0.0)), 0.0) for lg in log_gamma]
    q_decay = [jnp.exp(lg * (idx + 1.0)) for lg in log_gamma]
    k_decay = [jnp.exp(lg * (c - 1.0 - idx)) for lg in log_gamma]
    chunk_decay = [math.exp(lg * c) for lg in log_gamma]
    dk = lambda hd: slice(hd * RET_DK, (hd + 1) * RET_DK)
    dv = lambda hd: slice(hd * RET_DV, (hd + 1) * RET_DV)
    r_state = [state[hd] for hd in heads]
    outs = []
    for i in range(n_chunks):
        rows = slice(i * c, (i + 1) * c)
        q_c = [q_ref[0, rows, dk(hd)] for hd in heads]
        k_c = [k_ref[0, rows, dk(hd)] for hd in heads]
        v_c = [v_ref[0, rows, dv(hd)] for hd in heads]
        s = [_mm_nt(q_c[hd], k_c[hd]) * inner[hd] for hd in heads]
        cross = [_mm(q_c[hd], r_state[hd]) * q_decay[hd] for hd in heads]
        o = [_mm(s[hd], v_c[hd]) + cross[hd] for hd in heads]
        kv = [_mm_tn(k_c[hd].astype(F32) * k_decay[hd], v_c[hd]) for hd in heads]
        r_state = [r_state[hd] * chunk_decay[hd] + kv[hd] for hd in heads]
        outs.append(o)
    for hd in heads:
        state[hd] = r_state[hd]
    mod = mod_ref[0]
    ng = ng_ref[...]
    for i, o in enumerate(outs):
        rows = slice(i * c, (i + 1) * c)
        for hd in heads:
            oc = o[hd] - jnp.mean(o[hd], axis=-1, keepdims=True)
            on = oc * lax.rsqrt(jnp.mean(oc * oc, axis=-1, keepdims=True) + HEAD_NORM_EPS)
            gt = g_ref[0, rows, dv(hd)].astype(F32)
            z_buf[rows, dv(hd)] = (gt * _sigmoid(gt) * on).astype(BF16)
        y = jnp.dot(z_buf[rows, :], wo_ref[...], preferred_element_type=F32)
        out_ref[0, rows, :] = _sublayer_out(res_ref[0, rows, :], y, mod, ng, 1, 1.0)


def _ret_mix(q, k, v, g, res, mod, ng, w_o):
    bsz, t_len, d = res.shape
    tm = _tile(t_len, RET_TILE)
    qk = RET_H * RET_DK
    nv = RET_H * RET_DV
    tok = lambda n: pl.BlockSpec((1, tm, n), lambda b, t: (b, t, 0))
    return pl.pallas_call(
        functools.partial(_ret_mix_kernel, tm // RET_CHUNK),
        grid=(bsz, t_len // tm),
        in_specs=[tok(qk), tok(qk), tok(nv), tok(nv), tok(d),
                  pl.BlockSpec((1, 3 * N_SUB, d), lambda b, t: (b, 0, 0)), _resident(ng.shape), _resident(w_o.shape)],
        out_specs=tok(d),
        out_shape=jax.ShapeDtypeStruct(res.shape, F32),
        scratch_shapes=[pltpu.VMEM((RET_H, RET_DK, RET_DV), F32), pltpu.VMEM((tm, nv), BF16)],
        compiler_params=_params(2),
        name="ret_mix",
    )(q, k, v, g, res, mod, ng, w_o)


def _lru_kernel(res_ref, mod_ref, ng_ref, win_ref, cw_ref, vec_ref, gw_ref, wo_ref, out_ref, xext, h_carry):
    tm = res_ref.shape[1]
    width = xext.shape[1]

    @pl.when(pl.program_id(1) == 0)
    def _():
        xext[0:SUBLANES, :] = jnp.zeros((SUBLANES, width), F32)
        h_carry[...] = jnp.zeros(h_carry.shape, F32)

    mod = mod_ref[0]
    ng = ng_ref[...]
    cw = cw_ref[...]
    vec = vec_ref[...]
    conv_b, gate_bi, gate_br, lam = vec[0:1], vec[1:2], vec[2:3], vec[3:4]
    neg_lam = -lam
    softplus = jnp.maximum(neg_lam, 0.0) + jnp.log1p(jnp.exp(-jnp.abs(neg_lam)))
    parts = [slice(i * tm // LRU_ROW_PARTS, (i + 1) * tm // LRU_ROW_PARTS) for i in range(LRU_ROW_PARTS)]
    res = [res_ref[0, rows, :] for rows in parts]
    h = [_sublayer_in(x, mod, ng, 1).astype(BF16) for x in res]
    gate_branch = [jnp.dot(x, win_ref[:, :width], preferred_element_type=F32) for x in h]
    xb = [jnp.dot(x, win_ref[:, width:], preferred_element_type=F32) for x in h]
    for rows, x in zip(parts, xb):
        xext[SUBLANES + rows.start:SUBLANES + rows.stop, :] = x
    xc = []
    for rows, x in zip(parts, xb):
        acc = conv_b + cw[CONV_W - 1:CONV_W] * x
        for j in range(CONV_W - 1):
            lo = SUBLANES - (CONV_W - 1) + j
            acc = acc + cw[j:j + 1] * xext[lo + rows.start:lo + rows.stop, :]
        xc.append(acc)
    xext[0:SUBLANES, :] = xext[tm:tm + SUBLANES, :]

    a, u = [], []
    for x in xc:
        xcb = x.astype(BF16)
        gates = [jnp.concatenate([jnp.dot(xcb[:, hd * LRU_BW:(hd + 1) * LRU_BW], gw_ref[gi * LRU_H + hd],
                                          preferred_element_type=F32) for hd in range(LRU_H)], axis=1)
                 for gi in range(2)]
        i_gate = _sigmoid(gates[0] + gate_bi)
        r_gate = _sigmoid(gates[1] + gate_br)
        log_a = -LRU_C * r_gate * softplus
        a_p = jnp.exp(log_a)
        a.append(a_p)
        u.append(jnp.sqrt(-jnp.tanh(log_a) * (a_p * a_p + 1.0)) * (i_gate * x))

    row = lax.broadcasted_iota(jnp.int32, (SUBLANES, width), 0)
    carry = h_carry[...]
    hs = []
    for a_p, u_p in zip(a, u):
        groups = []
        for i in range(a_p.shape[0] // SUBLANES):
            ag = a_p[i * SUBLANES:(i + 1) * SUBLANES]
            ug = u_p[i * SUBLANES:(i + 1) * SUBLANES]
            for d in (1, 2, 4):
                keep = row >= d
                u_prev = jnp.where(keep, pltpu.roll(ug, d, 0), 0.0)
                a_prev = jnp.where(keep, pltpu.roll(ag, d, 0), 1.0)
                ug = ug + ag * u_prev
                ag = ag * a_prev
            hg = ug + ag * carry
            groups.append(hg)
            carry = jnp.broadcast_to(hg[SUBLANES - 1:SUBLANES, :], (SUBLANES, width))
        hs.append(jnp.concatenate(groups, axis=0))
    h_carry[...] = carry

    for rows, x, gb, hp in zip(parts, res, gate_branch, hs):
        gelu = 0.5 * gb * (1.0 + jnp.tanh(math.sqrt(2.0 / math.pi) * (gb + 0.044715 * (gb * gb * gb))))
        y = jnp.dot((gelu * hp).astype(BF16), wo_ref[...], preferred_element_type=F32)
        out_ref[0, rows, :] = _sublayer_out(x, y, mod, ng, 1, 1.0)


def _lru_sublayer(res, mod, ng, w_in, conv_w, vec, gate_w, w_o):
    bsz, t_len, d = res.shape
    width = w_o.shape[0]
    tm = _tile(t_len, LRU_TILE)
    tok = pl.BlockSpec((1, tm, d), lambda b, t: (b, t, 0))
    return pl.pallas_call(
        _lru_kernel,
        grid=(bsz, t_len // tm),
        in_specs=[tok, pl.BlockSpec((1, 3 * N_SUB, d), lambda b, t: (b, 0, 0)), _resident(ng.shape),
                  _resident(w_in.shape), _resident(conv_w.shape), _resident(vec.shape),
                  _resident(gate_w.shape), _resident(w_o.shape)],
        out_specs=tok,
        out_shape=jax.ShapeDtypeStruct(res.shape, F32),
        scratch_shapes=[pltpu.VMEM((tm + SUBLANES, width), F32), pltpu.VMEM((SUBLANES, width), F32)],
        compiler_params=_params(2),
        name="lru_sublayer",
    )(res, mod, ng, w_in, conv_w, vec, gate_w, w_o)


def kernel(x, c, positions, ada_w, ada_b, norm_g, ffn_w_in, ffn_w_out, rwkv_mu, rwkv_w_rkv, rwkv_w0, rwkv_w1, rwkv_w2, rwkv_a0, rwkv_a1, rwkv_a2, rwkv_g1, rwkv_g2, rwkv_k_k, rwkv_k_a, rwkv_r_k, rwkv_ln_w, rwkv_ln_b, rwkv_w_o, rwkv_v0, rwkv_v1, rwkv_v2, ret_w_in, ret_w_o, lru_w_in, lru_conv_w, lru_conv_b, lru_gate_w, lru_gate_b, lru_lambda, lru_w_o):
    depth = ada_w.shape[0]
    d_ff = ffn_w_out.shape[2]
    res = x.astype(F32)
    mod = _ada_mod(c.astype(F32), ada_w, ada_b)
    pos_f = positions.astype(F32)[..., None]
    v_first = None
    for i in range(depth):
        ng = norm_g[i]
        ffn = lambda res, m, s: _ffn_sublayer(
            res, mod[i], ng, ffn_w_in[i, m, :, :d_ff].astype(BF16), ffn_w_in[i, m, :, d_ff:].astype(BF16),
            ffn_w_out[i, m].astype(BF16), s)
        res = ffn(res, 0, 0)
        kind, j = i % 3, i // 3
        if kind == 0:
            p = dict(mu=rwkv_mu[j], w_rkv=rwkv_w_rkv[j], w0=rwkv_w0[j], w1=rwkv_w1[j], w2=rwkv_w2[j],
                     a0=rwkv_a0[j], a1=rwkv_a1[j], a2=rwkv_a2[j], g1=rwkv_g1[j], g2=rwkv_g2[j],
                     k_k=rwkv_k_k[j], k_a=rwkv_k_a[j], r_k=rwkv_r_k[j], ln_w=rwkv_ln_w[j], ln_b=rwkv_ln_b[j])
            if j > 0:
                p.update(v0=rwkv_v0[j - 1], v1=rwkv_v1[j - 1], v2=rwkv_v2[j - 1])
            p["w_o"] = rwkv_w_o[j]
            res, v_out = _rwkv_sublayer(res, mod[i], ng, p, v_first if j > 0 else None, j == 0)
            if j == 0:
                v_first = v_out
        elif kind == 1:
            q, k, v, g = _ret_pre(res, mod[i], ng, pos_f, ret_w_in[j].astype(BF16))
            res = _ret_mix(q, k, v, g, res, mod[i], ng, ret_w_o[j].astype(BF16))
        else:
            width = lru_w_o.shape[1]
            vec = jnp.stack([lru_conv_b[j], lru_gate_b[j, 0].reshape(width), lru_gate_b[j, 1].reshape(width),
                             lru_lambda[j]])
            gate_w = lru_gate_w[j].reshape(2 * LRU_H, LRU_BW, LRU_BW).astype(BF16)
            res = _lru_sublayer(res, mod[i], ng, lru_w_in[j].astype(BF16), lru_conv_w[j], vec, gate_w,
                                lru_w_o[j].astype(BF16))
        res = ffn(res, 1, 2)
    return res.astype(x.dtype)
```

```python
import functools
import math

import jax
import jax.numpy as jnp
from jax import lax
from jax.experimental import pallas as pl
from jax.experimental.pallas import tpu as pltpu

F32 = jnp.float32
BF16 = jnp.bfloat16

NORM_EPS = 1e-6
N_SUB = 3

MXU_WIDTH = 256
SUBLANES = 8

RWKV_N = 64
RWKV_GROUP = MXU_WIDTH // RWKV_N
RWKV_GW = RWKV_GROUP * RWKV_N
RWKV_CHUNK = 64
RWKV_GN_EPS = 64e-5
LORA_PAD = 128

RET_H = 4
RET_DK = 256
RET_DV = 512
RET_CHUNK = 256
ROPE_BASE = 10000.0
HEAD_NORM_EPS = 1e-5

LRU_H = 5
LRU_BW = 256
CONV_W = 4
LRU_C = 8.0

VMEM_LIMIT = 56 * 1024 * 1024
ADA_TILE = 1536
FFN_TILE = 1024
FFN_ROW_PARTS = 4
RWKV_TILE = 256
RET_TILE = 512
RET_PRE_PARTS = 2
LRU_TILE = 512
LRU_ROW_PARTS = 4

NT_DIMS = (((1,), (1,)), ((), ()))
TN_DIMS = (((0,), (0,)), ((), ()))


def _mm(a, b):
    return jnp.dot(a.astype(BF16), b.astype(BF16), preferred_element_type=F32)


def _mm_nt(a, b):
    return lax.dot_general(a.astype(BF16), b.astype(BF16), NT_DIMS, preferred_element_type=F32)


def _mm_tn(a, b):
    return lax.dot_general(a.astype(BF16), b.astype(BF16), TN_DIMS, preferred_element_type=F32)


def _mm_split_lhs_exact(w, x):
    hi = x.astype(BF16)
    lo = (x - hi.astype(F32)).astype(BF16)
    return (jnp.dot(w, hi, preferred_element_type=F32) + jnp.dot(w, lo, preferred_element_type=F32))


def _rms(x, g):
    return x * lax.rsqrt(jnp.mean(x * x, axis=-1, keepdims=True) + NORM_EPS) * g


def _sublayer_in(res, mod, ng, s):
    return _rms(res, ng[2 * s:2 * s + 1]) * (1.0 + mod[3 * s + 1:3 * s + 2]) + mod[3 * s:3 * s + 1]


def _sublayer_out(res, y, mod, ng, s, weight):
    return res + weight * mod[3 * s + 2:3 * s + 3] * _rms(y, ng[2 * s + 1:2 * s + 2])


def _sigmoid(x):
    return jax.nn.sigmoid(x)


def _resident(shape):
    nd = len(shape)
    return pl.BlockSpec(shape, lambda *_: (0,) * nd, pipeline_mode=pl.Buffered(1))


def _params(n_axes):
    return pltpu.CompilerParams(dimension_semantics=("arbitrary",) * n_axes, vmem_limit_bytes=VMEM_LIMIT)


def _tile(t_len, want):
    tm = min(want, t_len)
    assert t_len % tm == 0, (t_len, tm)
    return tm


def _ada_kernel(c_ref, w_ref, b_ref, o_ref):
    c = c_ref[...]
    cond = c * _sigmoid(c)
    o_ref[0] = _mm(cond, w_ref[0]) + b_ref[0]


def _ada_mod(c, ada_w, ada_b):
    depth, d, n = ada_w.shape
    bsz = c.shape[0]
    tn = ADA_TILE
    assert n % tn == 0
    out = pl.pallas_call(
        _ada_kernel,
        grid=(depth, n // tn),
        in_specs=[pl.BlockSpec((bsz, d), lambda l, j: (0, 0)),
                  pl.BlockSpec((1, d, tn), lambda l, j: (l, 0, j)),
                  pl.BlockSpec((1, 1, tn), lambda l, j: (l, 0, j))],
        out_specs=pl.BlockSpec((1, bsz, tn), lambda l, j: (l, 0, j)),
        out_shape=jax.ShapeDtypeStruct((depth, bsz, n), F32),
        compiler_params=_params(2),
        name="ada_mod",
    )(c, ada_w, ada_b.reshape(depth, 1, n))
    return out.reshape(depth, bsz, 3 * N_SUB, d)


def _ffn_kernel(s, res_ref, mod_ref, ng_ref, wa_ref, wb_ref, wo_ref, out_ref):
    mod = mod_ref[0]
    ng = ng_ref[...]
    f = wa_ref.shape[1]
    cut = -(-f // (2 * MXU_WIDTH)) * MXU_WIDTH
    chunks = [slice(0, cut), slice(cut, f)]
    tm = res_ref.shape[1]
    parts = [slice(i * tm // FFN_ROW_PARTS, (i + 1) * tm // FFN_ROW_PARTS) for i in range(FFN_ROW_PARTS)]
    res = [res_ref[0, rows, :] for rows in parts]
    h = [_sublayer_in(x, mod, ng, s).astype(BF16) for x in res]
    y = [jnp.zeros(x.shape, F32) for x in res]
    for cols in chunks:
        for i in range(len(parts)):
            a = jnp.dot(h[i], wa_ref[:, cols], preferred_element_type=F32)
            b = jnp.dot(h[i], wb_ref[:, cols], preferred_element_type=F32)
            z = (a * _sigmoid(a) * b).astype(BF16)
            y[i] = y[i] + jnp.dot(z, wo_ref[cols, :], preferred_element_type=F32)
    for i, rows in enumerate(parts):
        out_ref[0, rows, :] = _sublayer_out(res[i], y[i], mod, ng, s, 0.5)


def _ffn_sublayer(res, mod, ng, w_a, w_b, w_o, s):
    bsz, t_len, d = res.shape
    f = w_a.shape[1]
    tm = _tile(t_len, FFN_TILE)
    return pl.pallas_call(
        functools.partial(_ffn_kernel, s),
        grid=(bsz, t_len // tm),
        in_specs=[pl.BlockSpec((1, tm, d), lambda b, t: (b, t, 0)),
                  pl.BlockSpec((1, 3 * N_SUB, d), lambda b, t: (b, 0, 0)),
                  _resident(ng.shape), _resident((d, f)), _resident((d, f)), _resident((f, d))],
        out_specs=pl.BlockSpec((1, tm, d), lambda b, t: (b, t, 0)),
        out_shape=jax.ShapeDtypeStruct(res.shape, F32),
        compiler_params=_params(2),
        name="ffn_sublayer",
    )(res, mod, ng, w_a, w_b, w_o)


RWKV_STREAMS = ("r", "lw", "k", "v", "kk", "a", "g")


def _rwkv_pre_stages(has_vres, res, mod, ng_ref, mu_ref, vec_ref, w, vfirst, hext, store):
    tm, d = res.shape
    h = _sublayer_in(res, mod, ng_ref[...], 1)
    hext[SUBLANES:SUBLANES + tm, :] = h
    dx = hext[SUBLANES - 1:SUBLANES - 1 + tm, :] - h
    hext[0:SUBLANES, :] = hext[tm:tm + SUBLANES, :]
    mu = mu_ref[...]
    xr, xw, xk, xv, xa, xg = ((h + dx * mu[j:j + 1]).astype(BF16) for j in range(6))
    vec = vec_ref[...]
    w0, a0, k_k, k_a = vec[0:1], vec[1:2], vec[2:3], vec[3:4]
    full = slice(0, d)
    yield
    tw = jnp.tanh(jnp.dot(xw, w["w1"][...], preferred_element_type=F32))
    ta = jnp.dot(xa, w["a1"][...], preferred_element_type=F32)
    yield
    tg = _sigmoid(jnp.dot(xg, w["g1"][...], preferred_element_type=F32))
    if has_vres:
        tv = jnp.dot(xv, w["v1"][...], preferred_element_type=F32)
    yield
    store("lw", full, -math.exp(-0.5) * _sigmoid(w0 + _mm(tw, w["w2"][...])))
    a = _sigmoid(a0 + _mm(ta, w["a2"][...]))
    store("a", full, a)
    yield
    store("g", full, _mm(tg, w["g2"][...]))
    if has_vres:
        mix = _sigmoid(w["v0"][...] + _mm(tv, w["v2"][...]))
    yield
    for lo in range(0, d, RWKV_GW):
        cols = slice(lo, lo + RWKV_GW)
        store("r", cols, jnp.dot(xr, w["wr"][:, cols], preferred_element_type=F32))
        yield
    for lo in range(0, d, RWKV_GW):
        cols = slice(lo, lo + RWKV_GW)
        k = jnp.dot(xk, w["wk"][:, cols], preferred_element_type=F32)
        store("kk", cols, k * k_k[:, cols])
        store("k", cols, k * (1.0 + (a[:, cols] - 1.0) * k_a[:, cols]))
        yield
    for lo in range(0, d, RWKV_GW):
        cols = slice(lo, lo + RWKV_GW)
        v = jnp.dot(xv, w["wv"][:, cols], preferred_element_type=F32)
        if has_vres:
            v = v + (vfirst[:, cols].astype(F32) - v) * mix[:, cols]
        store("v", cols, v)
        yield


def _pad_cols(w, n):
    return jnp.pad(w, ((0, 0), (0, n - w.shape[1])))


def _pad_rows(w, n):
    return jnp.pad(w, ((0, n - w.shape[0]), (0, 0)))


def _block_diag(x, bd_mask):
    xb = x.astype(BF16)
    return jnp.where(bd_mask, jnp.concatenate([xb] * RWKV_GROUP, axis=0), jnp.zeros((), BF16))


class _Staged:
    def __init__(self, gen):
        self.gen, self.done, self.value = gen, False, None

    def step(self):
        if not self.done:
            try:
                next(self.gen)
            except StopIteration as stop:
                self.done, self.value = True, stop.value

    def finish(self):
        while not self.done:
            self.step()
        return self.value


def _wkv_prepare(raw, masks, consts):
    bd_mask, strict, incl, eye = masks
    tril = consts
    c = RWKV_CHUNK
    bd = lambda x: _block_diag(x, bd_mask)
    cum = [_mm_split_lhs_exact(tril, lw) for r, lw, k, v, kkn, a in raw]
    yield
    units = []
    for (r, lw, k, v, kkn, a), cum_i in zip(raw, cum):
        g_t = jnp.exp(cum_i)
        g_inv = jnp.exp(-cum_i)
        units.append((r * g_t, -kkn * jnp.exp(cum_i - lw), kkn * a * g_inv, k * g_inv, v, g_t[c - 1:c]))
    ar = [jnp.concatenate([at, rt], axis=0).astype(BF16) for rt, at, bt, kt, v, gl in units]
    sv = [bd(u[4]) for u in units]
    lb = [_mm_nt(x, bd(u[2])) for x, u in zip(ar, units)]
    lk = [_mm_nt(x, bd(u[3])) for x, u in zip(ar, units)]
    yield
    l_ab = [jnp.where(strict, x[:c], 0.0) for x in lb]
    l_rb = [jnp.where(incl, x[c:], 0.0) for x in lb]
    l_ak = [jnp.where(strict, x[:c], 0.0) for x in lk]
    l_rk = [jnp.where(incl, x[c:], 0.0) for x in lk]
    pw = [_mm(x, bd(x)) for x in l_ab]
    inv = [eye + x for x in l_ab]
    yield
    for _ in range(4):
        both = [_mm(jnp.concatenate([p, t], axis=0), bd(p)) for p, t in zip(pw, inv)]
        pw = [x[:c] for x in both]
        inv = [t + x[c:] for t, x in zip(inv, both)]
        yield
    inv = [t + _mm(t, bd(p)) for t, p in zip(inv, pw)]
    kv = [_mm(jnp.concatenate([x, y], axis=0), s) for x, y, s in zip(l_ak, l_rk, sv)]
    ak_v = [x[:c] for x in kv]
    rk_v = [x[c:] for x in kv]
    bk = [jnp.concatenate([bt, kt], axis=0).astype(BF16) for rt, at, bt, kt, v, gl in units]
    return [dict(ar=ar[i], inv=inv[i], l_rb=l_rb[i], ak_v=ak_v[i], rk_v=rk_v[i], bk=bk[i], v=units[i][4],
                 g_last=units[i][5]) for i in range(len(units))]


def _wkv_advance(get_prepared, n_chunks, z_states, masks):
    bd_mask = masks[0]
    c = RWKV_CHUNK
    bd = lambda x: _block_diag(x, bd_mask)
    ys = []
    for i in range(n_chunks):
        prepared = get_prepared(i)
        ars = [_mm_nt(p["ar"], z) for p, z in zip(prepared, z_states)]
        yield
        u = [_mm(p["inv"], bd(x[:c] + p["ak_v"])) for p, x in zip(prepared, ars)]
        yield
        y = [x[c:] + _mm(p["l_rb"], bd(uu)) + p["rk_v"] for p, x, uu in zip(prepared, ars, u)]
        dz = [_mm_tn(jnp.concatenate([uu, p["v"]], axis=0), p["bk"]) for p, uu in zip(prepared, u)]
        z_states = [(z + jnp.where(bd_mask, d, 0.0)) * p["g_last"] for z, d, p in zip(z_states, dz, prepared)]
        ys.append(y)
        yield
    return ys, z_states


def _wkv_masks():
    gw, c = RWKV_GW, RWKV_CHUNK
    row = lax.broadcasted_iota(jnp.int32, (gw, gw), 0)
    col = lax.broadcasted_iota(jnp.int32, (gw, gw), 1)
    bd_mask = (row // RWKV_N) == (col // RWKV_N)
    t = lax.broadcasted_iota(jnp.int32, (c, gw), 0)
    j = lax.broadcasted_iota(jnp.int32, (c, gw), 1) % c
    eye = jnp.where(t == j, 1.0, 0.0).astype(F32)
    rc = lax.broadcasted_iota(jnp.int32, (c, c), 0)
    cc = lax.broadcasted_iota(jnp.int32, (c, c), 1)
    tril = jnp.where(rc >= cc, 1.0, 0.0).astype(BF16)
    bd_ones = jnp.where(bd_mask, 1.0, 0.0).astype(BF16)
    return (bd_mask, t > j, t >= j, eye), tril, bd_ones


def _rwkv_kernel(has_vres, emit_vfirst, n_chunks, tiles_per_row, n_tiles, *refs):
    names = ["res_pre", "mod_pre", "res_out", "mod_out", "ng", "mu", "vec4", "wr", "wk", "wv", "w1", "w2", "a1", "a2",
             "g1", "g2"] + (["v1", "v2", "v0", "vfirst"] if has_vres else []) + ["vec3", "wo", "out"]
    names += (["vfirst_out"] if emit_vfirst else []) + ["hext", "state", "z_buf"] + ["p_" + n for n in RWKV_STREAMS]
    ref = dict(zip(names, refs))
    assert len(names) == len(refs)
    step = pl.program_id(0)
    slot_w = step % 2
    slot_r = 1 - slot_w
    pre_tile = jnp.minimum(step, n_tiles - 1)
    wkv_tile = jnp.clip(step - 1, 0, n_tiles - 1)
    hext, state, z_buf = ref["hext"], ref["state"], ref["z_buf"]

    @pl.when(step == 0)
    def _():
        for buf in [z_buf] + [ref["p_" + n] for n in RWKV_STREAMS]:
            buf[...] = jnp.zeros(buf.shape, buf.dtype)

    @pl.when(pre_tile % tiles_per_row == 0)
    def _():
        hext[0:SUBLANES, :] = jnp.zeros((SUBLANES, hext.shape[1]), F32)

    @pl.when(wkv_tile % tiles_per_row == 0)
    def _():
        state[...] = jnp.zeros(state.shape, F32)

    def store(name, cols, value):
        dst = ref["p_" + name]
        dst[slot_w, :, cols] = value.astype(dst.dtype)

    pre = _Staged(_rwkv_pre_stages(has_vres, ref["res_pre"][0], ref["mod_pre"][0], ref["ng"], ref["mu"], ref["vec4"],
                                   ref, ref["vfirst"][0] if has_vres else None, hext, store))

    c = RWKV_CHUNK
    n_groups = state.shape[0]
    masks, tril, bd_ones = _wkv_masks()
    lanes = [slice(gi * RWKV_GW, (gi + 1) * RWKV_GW) for gi in range(n_groups)]
    tb = n_chunks * c
    vec3 = ref["vec3"]

    def head_sums(per_group):
        s = _mm(jnp.concatenate(per_group, axis=0), bd_ones)
        return [s[gi * tb:(gi + 1) * tb] for gi in range(n_groups)]

    load = lambda n: [ref["p_" + n][slot_r, :, ln].astype(F32) for ln in lanes]
    r, lw, k, v, kk, a = (load(n) for n in RWKV_STREAMS[:6])
    kkn = [x / jnp.maximum(jnp.sqrt(n2), 1e-12) for x, n2 in zip(kk, head_sums([x * x for x in kk]))]
    bonus = head_sums([r[gi] * k[gi] * vec3[0:1, lanes[gi]] for gi in range(n_groups)])
    y_out = jnp.dot(z_buf[slot_r], ref["wo"][...], preferred_element_type=F32)
    ref["out"][0] = _sublayer_out(ref["res_out"][0], y_out, ref["mod_out"][0], ref["ng"][...], 1, 1.0)
    if emit_vfirst:
        ref["vfirst_out"][0] = ref["p_v"][slot_w]
    pre.step()

    def raw_units(i):
        rows = slice(i * c, (i + 1) * c)
        return [tuple(x[gi][rows] for x in (r, lw, k, v, kkn, a)) for gi in range(n_groups)]

    lead = min(2, n_chunks)
    first_task = _Staged(_wkv_prepare([u for i in range(lead) for u in raw_units(i)], masks, tril))
    while not first_task.done:
        first_task.step()
        pre.step()
    first = first_task.value
    later = [_Staged(_wkv_prepare(raw_units(i), masks, tril)) for i in range(lead, n_chunks)]

    def get_prepared(i):
        if i < lead:
            return first[i * n_groups:(i + 1) * n_groups]
        return later[i - lead].finish()

    advance = _Staged(_wkv_advance(get_prepared, n_chunks, [state[gi] for gi in range(n_groups)], masks))
    while not advance.done:
        advance.step()
        for task in later:
            if not task.done:
                task.step()
                break
        pre.step()
    ys, z_states = advance.value
    for gi in range(n_groups):
        state[gi] = z_states[gi]
    y = [jnp.concatenate([ys[i][gi] for i in range(n_chunks)], axis=0) for gi in range(n_groups)]
    yc = [x - m * (1.0 / RWKV_N) for x, m in zip(y, head_sums(y))]
    pre.step()
    var = head_sums([x * x for x in yc])
    pre.step()
    for gi, ln in enumerate(lanes):
        yn = yc[gi] * lax.rsqrt(var[gi] * (1.0 / RWKV_N) + RWKV_GN_EPS) * vec3[1:2, ln] + vec3[2:3, ln]
        z_buf[slot_w, :, ln] = ((yn + bonus[gi] * v[gi]) * ref["p_g"][slot_r, :, ln].astype(F32)).astype(BF16)
    pre.finish()


def _rwkv_sublayer(res, mod, ng, p, v_first, emit_vfirst):
    bsz, t_len, d = res.shape
    n_groups = d // RWKV_GW
    tb = _tile(t_len, RWKV_TILE)
    per_row = t_len // tb
    n_tiles = bsz * per_row
    has_vres = v_first is not None
    pre = lambda s: jnp.minimum(s, n_tiles - 1)
    done = lambda s: jnp.maximum(s - 2, 0)
    tok_pre = pl.BlockSpec((1, tb, d), lambda s: (pre(s) // per_row, pre(s) % per_row, 0))
    tok_out = pl.BlockSpec((1, tb, d), lambda s: (done(s) // per_row, done(s) % per_row, 0))
    mod_pre = pl.BlockSpec((1, 3 * N_SUB, d), lambda s: (pre(s) // per_row, 0, 0))
    mod_out = pl.BlockSpec((1, 3 * N_SUB, d), lambda s: (done(s) // per_row, 0, 0))
    lora_in = lambda w: _pad_cols(w, LORA_PAD).astype(BF16)
    lora_out = lambda w: _pad_rows(w, LORA_PAD).astype(BF16)
    vec4 = jnp.stack([p["w0"], p["a0"], p["k_k"], p["k_a"]])
    vec3 = jnp.stack([p["r_k"].reshape(d), p["ln_w"], p["ln_b"]])
    weights = [ng, p["mu"], vec4, p["w_rkv"][0].astype(BF16), p["w_rkv"][1].astype(BF16), p["w_rkv"][2].astype(BF16),
               lora_in(p["w1"]), lora_out(p["w2"]), lora_in(p["a1"]), lora_out(p["a2"]),
               lora_in(p["g1"]), lora_out(p["g2"])]
    args = [res, mod, res, mod] + weights
    in_specs = [tok_pre, mod_pre, tok_out, mod_out] + [_resident(x.shape) for x in weights]
    if has_vres:
        extra = [lora_in(p["v1"]), lora_out(p["v2"]), p["v0"].reshape(1, d)]
        args += extra + [v_first]
        in_specs += [_resident(x.shape) for x in extra] + [tok_pre]
    tail = [vec3, p["w_o"].astype(BF16)]
    args += tail
    in_specs += [_resident(x.shape) for x in tail]
    out_shape = [jax.ShapeDtypeStruct(res.shape, F32)]
    out_specs = [tok_out]
    if emit_vfirst:
        out_shape.append(jax.ShapeDtypeStruct(res.shape, BF16))
        out_specs.append(tok_out)
    streams = [pltpu.VMEM((2, tb, d), F32 if n == "lw" else BF16) for n in RWKV_STREAMS]
    outs = pl.pallas_call(
        functools.partial(_rwkv_kernel, has_vres, emit_vfirst, tb // RWKV_CHUNK, per_row, n_tiles),
        grid=(n_tiles + 2,),
        in_specs=in_specs,
        out_specs=out_specs,
        out_shape=out_shape,
        scratch_shapes=[pltpu.VMEM((tb + SUBLANES, d), F32), pltpu.VMEM((n_groups, RWKV_GW, RWKV_GW), F32),
                        pltpu.VMEM((2, tb, d), BF16)] + streams,
        compiler_params=_params(1),
        name="rwkv_sublayer",
    )(*args)
    return outs[0], (outs[1] if emit_vfirst else None)


def _ret_pre_kernel(res_ref, mod_ref, ng_ref, pos_ref, inv_ref, w_ref, q_ref, k_ref, v_ref, g_ref):
    tm = res_ref.shape[1]
    half = RET_DK // 2
    qk = RET_H * RET_DK
    nv = RET_H * RET_DV
    mod = mod_ref[0]
    ng = ng_ref[...]
    for part in range(RET_PRE_PARTS):
        rows = slice(part * tm // RET_PRE_PARTS, (part + 1) * tm // RET_PRE_PARTS)
        h = _sublayer_in(res_ref[0, rows, :], mod, ng, 1).astype(BF16)
        for hd in range(RET_H):
            lo = 2 * qk + hd * RET_DV
            v_ref[0, rows, hd * RET_DV:(hd + 1) * RET_DV] = jnp.dot(
                h, w_ref[:, lo:lo + RET_DV], preferred_element_type=F32).astype(BF16)
            g_ref[0, rows, hd * RET_DV:(hd + 1) * RET_DV] = jnp.dot(
                h, w_ref[:, lo + nv:lo + nv + RET_DV], preferred_element_type=F32).astype(BF16)
        ang = pos_ref[0, rows, :] * inv_ref[...]
        cos = jnp.cos(ang)
        sin = jnp.sin(ang)
        for which, out_ref, scale in ((0, q_ref, 1.0), (1, k_ref, RET_DK ** -0.5)):
            for hd in range(RET_H):
                lo = which * qk + hd * RET_DK
                x = jnp.dot(h, w_ref[:, lo:lo + RET_DK], preferred_element_type=F32)
                x1, x2 = x[:, :half], x[:, half:]
                out_ref[0, rows, hd * RET_DK:hd * RET_DK + half] = ((x1 * cos - x2 * sin) * scale).astype(BF16)
                out_ref[0, rows, hd * RET_DK + half:(hd + 1) * RET_DK] = ((x1 * sin + x2 * cos) * scale).astype(BF16)


def _ret_pre(res, mod, ng, pos_f, w_in):
    bsz, t_len, d = res.shape
    tm = _tile(t_len, RET_TILE)
    qk = RET_H * RET_DK
    nv = RET_H * RET_DV
    half = RET_DK // 2
    inv = (1.0 / (ROPE_BASE ** jnp.linspace(0.0, 1.0, half, dtype=F32))).reshape(1, half)
    tok = lambda n: pl.BlockSpec((1, tm, n), lambda b, t: (b, t, 0))
    return pl.pallas_call(
        _ret_pre_kernel,
        grid=(bsz, t_len // tm),
        in_specs=[tok(d), pl.BlockSpec((1, 3 * N_SUB, d), lambda b, t: (b, 0, 0)), _resident(ng.shape),
                  tok(1), _resident((1, half)), _resident(w_in.shape)],
        out_specs=[tok(qk), tok(qk), tok(nv), tok(nv)],
        out_shape=[jax.ShapeDtypeStruct((bsz, t_len, qk), BF16), jax.ShapeDtypeStruct((bsz, t_len, qk), BF16),
                   jax.ShapeDtypeStruct((bsz, t_len, nv), BF16), jax.ShapeDtypeStruct((bsz, t_len, nv), BF16)],
        compiler_params=_params(2),
        name="ret_pre",
    )(res, mod, ng, pos_f, inv, w_in)


def _ret_mix_kernel(n_chunks, q_ref, k_ref, v_ref, g_ref, res_ref, mod_ref, ng_ref, wo_ref, out_ref, state, z_buf):
    @pl.when(pl.program_id(1) == 0)
    def _():
        state[...] = jnp.zeros(state.shape, F32)

    c = RET_CHUNK
    row = lax.broadcasted_iota(jnp.int32, (c, c), 0)
    col = lax.broadcasted_iota(jnp.int32, (c, c), 1)
    diff = (row - col).astype(F32)
    idx = lax.broadcasted_iota(jnp.int32, (c, 1), 0).astype(F32)
    heads = range(RET_H)
    log_gamma = [math.log(1.0 - 2.0 ** (-5.0 - hd)) for hd in heads]
    inner = [jnp.where(diff >= 0, jnp.exp(lg * jnp.maximum(diff, 0.0)), 0.0) for lg in log_gamma]
    q_decay = [jnp.exp(lg * (idx + 1.0)) for lg in log_gamma]
    k_decay = [jnp.exp(lg * (c - 1.0 - idx)) for lg in log_gamma]
    chunk_decay = [math.exp(lg * c) for lg in log_gamma]
    dk = lambda hd: slice(hd * RET_DK, (hd + 1) * RET_DK)
    dv = lambda hd: slice(hd * RET_DV, (hd + 1) * RET_DV)
    r_state = [state[hd] for hd in heads]
    outs = []
    for i in range(n_chunks):
        rows = slice(i * c, (i + 1) * c)
        q_c = [q_ref[0, rows, dk(hd)] for hd in heads]
        k_c = [k_ref[0, rows, dk(hd)] for hd in heads]
        v_c = [v_ref[0, rows, dv(hd)] for hd in heads]
        s = [_mm_nt(q_c[hd], k_c[hd]) * inner[hd] for hd in heads]
        cross = [_mm(q_c[hd], r_state[hd]) * q_decay[hd] for hd in heads]
        o = [_mm(s[hd], v_c[hd]) + cross[hd] for hd in heads]
        kv = [_mm_tn(k_c[hd].astype(F32) * k_decay[hd], v_c[hd]) for hd in heads]
        r_state = [r_state[hd] * chunk_decay[hd] + kv[hd] for hd in heads]
        outs.append(o)
    for hd in heads:
        state[hd] = r_state[hd]
    mod = mod_ref[0]
    ng = ng_ref[...]
    for i, o in enumerate(outs):
        rows = slice(i * c, (i + 1) * c)
        for hd in heads:
            oc = o[hd] - jnp.mean(o[hd], axis=-1, keepdims=True)
            on = oc * lax.rsqrt(jnp.mean(oc * oc, axis=-1, keepdims=True) + HEAD_NORM_EPS)
            gt = g_ref[0, rows, dv(hd)].astype(F32)
            z_buf[rows, dv(hd)] = (gt * _sigmoid(gt) * on).astype(BF16)
        y = jnp.dot(z_buf[rows, :], wo_ref[...], preferred_element_type=F32)
        out_ref[0, rows, :] = _sublayer_out(res_ref[0, rows, :], y, mod, ng, 1, 1.0)


def _ret_mix(q, k, v, g, res, mod, ng, w_o):
    bsz, t_len, d = res.shape
    tm = _tile(t_len, RET_TILE)
    qk = RET_H * RET_DK
    nv = RET_H * RET_DV
    tok = lambda n: pl.BlockSpec((1, tm, n), lambda b, t: (b, t, 0))
    return pl.pallas_call(
        functools.partial(_ret_mix_kernel, tm // RET_CHUNK),
        grid=(bsz, t_len // tm),
        in_specs=[tok(qk), tok(qk), tok(nv), tok(nv), tok(d),
                  pl.BlockSpec((1, 3 * N_SUB, d), lambda b, t: (b, 0, 0)), _resident(ng.shape), _resident(w_o.shape)],
        out_specs=tok(d),
        out_shape=jax.ShapeDtypeStruct(res.shape, F32),
        scratch_shapes=[pltpu.VMEM((RET_H, RET_DK, RET_DV), F32), pltpu.VMEM((tm, nv), BF16)],
        compiler_params=_params(2),
        name="ret_mix",
    )(q, k, v, g, res, mod, ng, w_o)


def _lru_kernel(res_ref, mod_ref, ng_ref, win_ref, cw_ref, vec_ref, gw_ref, wo_ref, out_ref, xext, h_carry):
    tm = res_ref.shape[1]
    width = xext.shape[1]

    @pl.when(pl.program_id(1) == 0)
    def _():
        xext[0:SUBLANES, :] = jnp.zeros((SUBLANES, width), F32)
        h_carry[...] = jnp.zeros(h_carry.shape, F32)

    mod = mod_ref[0]
    ng = ng_ref[...]
    cw = cw_ref[...]
    vec = vec_ref[...]
    conv_b, gate_bi, gate_br, lam = vec[0:1], vec[1:2], vec[2:3], vec[3:4]
    neg_lam = -lam
    softplus = jnp.maximum(neg_lam, 0.0) + jnp.log1p(jnp.exp(-jnp.abs(neg_lam)))
    parts = [slice(i * tm // LRU_ROW_PARTS, (i + 1) * tm // LRU_ROW_PARTS) for i in range(LRU_ROW_PARTS)]
    res = [res_ref[0, rows, :] for rows in parts]
    h = [_sublayer_in(x, mod, ng, 1).astype(BF16) for x in res]
    gate_branch = [jnp.dot(x, win_ref[:, :width], preferred_element_type=F32) for x in h]
    xb = [jnp.dot(x, win_ref[:, width:], preferred_element_type=F32) for x in h]
    for rows, x in zip(parts, xb):
        xext[SUBLANES + rows.start:SUBLANES + rows.stop, :] = x
    xc = []
    for rows, x in zip(parts, xb):
        acc = conv_b + cw[CONV_W - 1:CONV_W] * x
        for j in range(CONV_W - 1):
            lo = SUBLANES - (CONV_W - 1) + j
            acc = acc + cw[j:j + 1] * xext[lo + rows.start:lo + rows.stop, :]
        xc.append(acc)
    xext[0:SUBLANES, :] = xext[tm:tm + SUBLANES, :]

    a, u = [], []
    for x in xc:
        xcb = x.astype(BF16)
        gates = [jnp.concatenate([jnp.dot(xcb[:, hd * LRU_BW:(hd + 1) * LRU_BW], gw_ref[gi * LRU_H + hd],
                                          preferred_element_type=F32) for hd in range(LRU_H)], axis=1)
                 for gi in range(2)]
        i_gate = _sigmoid(gates[0] + gate_bi)
        r_gate = _sigmoid(gates[1] + gate_br)
        log_a = -LRU_C * r_gate * softplus
        a_p = jnp.exp(log_a)
        a.append(a_p)
        u.append(jnp.sqrt(-jnp.tanh(log_a) * (a_p * a_p + 1.0)) * (i_gate * x))

    row = lax.broadcasted_iota(jnp.int32, (SUBLANES, width), 0)
    carry = h_carry[...]
    hs = []
    for a_p, u_p in zip(a, u):
        groups = []
        for i in range(a_p.shape[0] // SUBLANES):
            ag = a_p[i * SUBLANES:(i + 1) * SUBLANES]
            ug = u_p[i * SUBLANES:(i + 1) * SUBLANES]
            for d in (1, 2, 4):
                keep = row >= d
                u_prev = jnp.where(keep, pltpu.roll(ug, d, 0), 0.0)
                a_prev = jnp.where(keep, pltpu.roll(ag, d, 0), 1.0)
                ug = ug + ag * u_prev
                ag = ag * a_prev
            hg = ug + ag * carry
            groups.append(hg)
            carry = jnp.broadcast_to(hg[SUBLANES - 1:SUBLANES, :], (SUBLANES, width))
        hs.append(jnp.concatenate(groups, axis=0))
    h_carry[...] = carry

    for rows, x, gb, hp in zip(parts, res, gate_branch, hs):
        gelu = 0.5 * gb * (1.0 + jnp.tanh(math.sqrt(2.0 / math.pi) * (gb + 0.044715 * (gb * gb * gb))))
        y = jnp.dot((gelu * hp).astype(BF16), wo_ref[...], preferred_element_type=F32)
        out_ref[0, rows, :] = _sublayer_out(x, y, mod, ng, 1, 1.0)


def _lru_sublayer(res, mod, ng, w_in, conv_w, vec, gate_w, w_o):
    bsz, t_len, d = res.shape
    width = w_o.shape[0]
    tm = _tile(t_len, LRU_TILE)
    tok = pl.BlockSpec((1, tm, d), lambda b, t: (b, t, 0))
    return pl.pallas_call(
        _lru_kernel,
        grid=(bsz, t_len // tm),
        in_specs=[tok, pl.BlockSpec((1, 3 * N_SUB, d), lambda b, t: (b, 0, 0)), _resident(ng.shape),
                  _resident(w_in.shape), _resident(conv_w.shape), _resident(vec.shape),
                  _resident(gate_w.shape), _resident(w_o.shape)],
        out_specs=tok,
        out_shape=jax.ShapeDtypeStruct(res.shape, F32),
        scratch_shapes=[pltpu.VMEM((tm + SUBLANES, width), F32), pltpu.VMEM((SUBLANES, width), F32)],
        compiler_params=_params(2),
        name="lru_sublayer",
    )(res, mod, ng, w_in, conv_w, vec, gate_w, w_o)


def kernel(x, c, positions, ada_w, ada_b, norm_g, ffn_w_in, ffn_w_out, rwkv_mu, rwkv_w_rkv, rwkv_w0, rwkv_w1, rwkv_w2, rwkv_a0, rwkv_a1, rwkv_a2, rwkv_g1, rwkv_g2, rwkv_k_k, rwkv_k_a, rwkv_r_k, rwkv_ln_w, rwkv_ln_b, rwkv_w_o, rwkv_v0, rwkv_v1, rwkv_v2, ret_w_in, ret_w_o, lru_w_in, lru_conv_w, lru_conv_b, lru_gate_w, lru_gate_b, lru_lambda, lru_w_o):
    depth = ada_w.shape[0]
    d_ff = ffn_w_out.shape[2]
    res = x.astype(F32)
    mod = _ada_mod(c.astype(F32), ada_w, ada_b)
    pos_f = positions.astype(F32)[..., None]
    v_first = None
    for i in range(depth):
        ng = norm_g[i]
        ffn = lambda res, m, s: _ffn_sublayer(
            res, mod[i], ng, ffn_w_in[i, m, :, :d_ff].astype(BF16), ffn_w_in[i, m, :, d_ff:].astype(BF16),
            ffn_w_out[i, m].astype(BF16), s)
        res = ffn(res, 0, 0)
        kind, j = i % 3, i // 3
        if kind == 0:
            p = dict(mu=rwkv_mu[j], w_rkv=rwkv_w_rkv[j], w0=rwkv_w0[j], w1=rwkv_w1[j], w2=rwkv_w2[j],
                     a0=rwkv_a0[j], a1=rwkv_a1[j], a2=rwkv_a2[j], g1=rwkv_g1[j], g2=rwkv_g2[j],
                     k_k=rwkv_k_k[j], k_a=rwkv_k_a[j], r_k=rwkv_r_k[j], ln_w=rwkv_ln_w[j], ln_b=rwkv_ln_b[j])
            if j > 0:
                p.update(v0=rwkv_v0[j - 1], v1=rwkv_v1[j - 1], v2=rwkv_v2[j - 1])
            p["w_o"] = rwkv_w_o[j]
            res, v_out = _rwkv_sublayer(res, mod[i], ng, p, v_first if j > 0 else None, j == 0)
            if j == 0:
                v_first = v_out
        elif kind == 1:
            q, k, v, g = _ret_pre(res, mod[i], ng, pos_f, ret_w_in[j].astype(BF16))
            res = _ret_mix(q, k, v, g, res, mod[i], ng, ret_w_o[j].astype(BF16))
        else:
            width = lru_w_o.shape[1]
            vec = jnp.stack([lru_conv_b[j], lru_gate_b[j, 0].reshape(width), lru_gate_b[j, 1].reshape(width),
                             lru_lambda[j]])
            gate_w = lru_gate_w[j].reshape(2 * LRU_H, LRU_BW, LRU_BW).astype(BF16)
            res = _lru_sublayer(res, mod[i], ng, lru_w_in[j].astype(BF16), lru_conv_w[j], vec, gate_w,
                                lru_w_o[j].astype(BF16))
        res = ffn(res, 1, 2)
    return res.astype(x.dtype)
```

```python
import functools
import math

import jax
import jax.numpy as jnp
from jax import lax
from jax.experimental import pallas as pl
from jax.experimental.pallas import tpu as pltpu

F32 = jnp.float32
BF16 = jnp.bfloat16

NORM_EPS = 1e-6
N_SUB = 3

MXU_WIDTH = 256
SUBLANES = 8

RWKV_N = 64
RWKV_GROUP = MXU_WIDTH // RWKV_N
RWKV_GW = RWKV_GROUP * RWKV_N
RWKV_CHUNK = 64
RWKV_GN_EPS = 64e-5
LORA_PAD = 128

RET_H = 4
RET_DK = 256
RET_DV = 512
RET_CHUNK = 256
ROPE_BASE = 10000.0
HEAD_NORM_EPS = 1e-5

LRU_H = 5
LRU_BW = 256
CONV_W = 4
LRU_C = 8.0

VMEM_LIMIT = 56 * 1024 * 1024
ADA_TILE = 1536
FFN_TILE = 1024
FFN_ROW_PARTS = 4
RWKV_TILE = 256
RWKV_LEAD_CHUNKS = 4
RET_TILE = 512
RET_PRE_PARTS = 2
LRU_TILE = 512
LRU_ROW_PARTS = 4

NT_DIMS = (((1,), (1,)), ((), ()))
TN_DIMS = (((0,), (0,)), ((), ()))


def _mm(a, b):
    return jnp.dot(a.astype(BF16), b.astype(BF16), preferred_element_type=F32)


def _mm_nt(a, b):
    return lax.dot_general(a.astype(BF16), b.astype(BF16), NT_DIMS, preferred_element_type=F32)


def _mm_tn(a, b):
    return lax.dot_general(a.astype(BF16), b.astype(BF16), TN_DIMS, preferred_element_type=F32)


def _mm_split_lhs_exact(w, x):
    hi = x.astype(BF16)
    lo = (x - hi.astype(F32)).astype(BF16)
    return (jnp.dot(w, hi, preferred_element_type=F32) + jnp.dot(w, lo, preferred_element_type=F32))


def _rms(x, g):
    return x * lax.rsqrt(jnp.mean(x * x, axis=-1, keepdims=True) + NORM_EPS) * g


def _sublayer_in(res, mod, ng, s):
    return _rms(res, ng[2 * s:2 * s + 1]) * (1.0 + mod[3 * s + 1:3 * s + 2]) + mod[3 * s:3 * s + 1]


def _sublayer_out(res, y, mod, ng, s, weight):
    return res + weight * mod[3 * s + 2:3 * s + 3] * _rms(y, ng[2 * s + 1:2 * s + 2])


def _sigmoid(x):
    return jax.nn.sigmoid(x)


def _resident(shape):
    nd = len(shape)
    return pl.BlockSpec(shape, lambda *_: (0,) * nd, pipeline_mode=pl.Buffered(1))


def _params(n_axes):
    return pltpu.CompilerParams(dimension_semantics=("arbitrary",) * n_axes, vmem_limit_bytes=VMEM_LIMIT)


def _tile(t_len, want):
    tm = min(want, t_len)
    assert t_len % tm == 0, (t_len, tm)
    return tm


def _ada_kernel(c_ref, w_ref, b_ref, o_ref):
    c = c_ref[...]
    cond = c * _sigmoid(c)
    o_ref[0] = _mm(cond, w_ref[0]) + b_ref[0]


def _ada_mod(c, ada_w, ada_b):
    depth, d, n = ada_w.shape
    bsz = c.shape[0]
    tn = ADA_TILE
    assert n % tn == 0
    out = pl.pallas_call(
        _ada_kernel,
        grid=(depth, n // tn),
        in_specs=[pl.BlockSpec((bsz, d), lambda l, j: (0, 0)),
                  pl.BlockSpec((1, d, tn), lambda l, j: (l, 0, j)),
                  pl.BlockSpec((1, 1, tn), lambda l, j: (l, 0, j))],
        out_specs=pl.BlockSpec((1, bsz, tn), lambda l, j: (l, 0, j)),
        out_shape=jax.ShapeDtypeStruct((depth, bsz, n), F32),
        compiler_params=_params(2),
        name="ada_mod",
    )(c, ada_w, ada_b.reshape(depth, 1, n))
    return out.reshape(depth, bsz, 3 * N_SUB, d)


def _ffn_kernel(s, res_ref, mod_ref, ng_ref, wa_ref, wb_ref, wo_ref, out_ref):
    mod = mod_ref[0]
    ng = ng_ref[...]
    f = wa_ref.shape[1]
    cut = -(-f // (2 * MXU_WIDTH)) * MXU_WIDTH
    chunks = [slice(0, cut), slice(cut, f)]
    tm = res_ref.shape[1]
    parts = [slice(i * tm // FFN_ROW_PARTS, (i + 1) * tm // FFN_ROW_PARTS) for i in range(FFN_ROW_PARTS)]
    res = [res_ref[0, rows, :] for rows in parts]
    h = [_sublayer_in(x, mod, ng, s).astype(BF16) for x in res]
    y = [jnp.zeros(x.shape, F32) for x in res]
    for cols in chunks:
        for i in range(len(parts)):
            a = jnp.dot(h[i], wa_ref[:, cols], preferred_element_type=F32)
            b = jnp.dot(h[i], wb_ref[:, cols], preferred_element_type=F32)
            z = (a * _sigmoid(a) * b).astype(BF16)
            y[i] = y[i] + jnp.dot(z, wo_ref[cols, :], preferred_element_type=F32)
    for i, rows in enumerate(parts):
        out_ref[0, rows, :] = _sublayer_out(res[i], y[i], mod, ng, s, 0.5)


def _ffn_sublayer(res, mod, ng, w_a, w_b, w_o, s):
    bsz, t_len, d = res.shape
    f = w_a.shape[1]
    tm = _tile(t_len, FFN_TILE)
    return pl.pallas_call(
        functools.partial(_ffn_kernel, s),
        grid=(bsz, t_len // tm),
        in_specs=[pl.BlockSpec((1, tm, d), lambda b, t: (b, t, 0)),
                  pl.BlockSpec((1, 3 * N_SUB, d), lambda b, t: (b, 0, 0)),
                  _resident(ng.shape), _resident((d, f)), _resident((d, f)), _resident((f, d))],
        out_specs=pl.BlockSpec((1, tm, d), lambda b, t: (b, t, 0)),
        out_shape=jax.ShapeDtypeStruct(res.shape, F32),
        compiler_params=_params(2),
        name="ffn_sublayer",
    )(res, mod, ng, w_a, w_b, w_o)


RWKV_STREAMS = ("r", "lw", "k", "v", "kk", "a", "g")


def _rwkv_pre_stages(has_vres, res, mod, ng_ref, mu_ref, vec_ref, w, vfirst, hext, store):
    tm, d = res.shape
    h = _sublayer_in(res, mod, ng_ref[...], 1)
    hext[SUBLANES:SUBLANES + tm, :] = h
    dx = hext[SUBLANES - 1:SUBLANES - 1 + tm, :] - h
    hext[0:SUBLANES, :] = hext[tm:tm + SUBLANES, :]
    mu = mu_ref[...]
    xr, xw, xk, xv, xa, xg = ((h + dx * mu[j:j + 1]).astype(BF16) for j in range(6))
    vec = vec_ref[...]
    w0, a0, k_k, k_a = vec[0:1], vec[1:2], vec[2:3], vec[3:4]
    full = slice(0, d)
    yield
    tw = jnp.tanh(jnp.dot(xw, w["w1"][...], preferred_element_type=F32))
    ta = jnp.dot(xa, w["a1"][...], preferred_element_type=F32)
    yield
    tg = _sigmoid(jnp.dot(xg, w["g1"][...], preferred_element_type=F32))
    if has_vres:
        tv = jnp.dot(xv, w["v1"][...], preferred_element_type=F32)
    yield
    store("lw", full, -math.exp(-0.5) * _sigmoid(w0 + _mm(tw, w["w2"][...])))
    a = _sigmoid(a0 + _mm(ta, w["a2"][...]))
    store("a", full, a)
    yield
    store("g", full, _mm(tg, w["g2"][...]))
    if has_vres:
        mix = _sigmoid(w["v0"][...] + _mm(tv, w["v2"][...]))
    yield
    for lo in range(0, d, RWKV_GW):
        cols = slice(lo, lo + RWKV_GW)
        store("r", cols, jnp.dot(xr, w["wr"][:, cols], preferred_element_type=F32))
        yield
    for lo in range(0, d, RWKV_GW):
        cols = slice(lo, lo + RWKV_GW)
        k = jnp.dot(xk, w["wk"][:, cols], preferred_element_type=F32)
        store("kk", cols, k * k_k[:, cols])
        store("k", cols, k * (1.0 + (a[:, cols] - 1.0) * k_a[:, cols]))
        yield
    for lo in range(0, d, RWKV_GW):
        cols = slice(lo, lo + RWKV_GW)
        v = jnp.dot(xv, w["wv"][:, cols], preferred_element_type=F32)
        if has_vres:
            v = v + (vfirst[:, cols].astype(F32) - v) * mix[:, cols]
        store("v", cols, v)
        yield


def _pad_cols(w, n):
    return jnp.pad(w, ((0, 0), (0, n - w.shape[1])))


def _pad_rows(w, n):
    return jnp.pad(w, ((0, n - w.shape[0]), (0, 0)))


def _block_diag(x, bd_mask):
    xb = x.astype(BF16)
    return jnp.where(bd_mask, jnp.concatenate([xb] * RWKV_GROUP, axis=0), jnp.zeros((), BF16))


class _Staged:
    def __init__(self, gen):
        self.gen, self.done, self.value = gen, False, None

    def step(self):
        if not self.done:
            try:
                next(self.gen)
            except StopIteration as stop:
                self.done, self.value = True, stop.value

    def finish(self):
        while not self.done:
            self.step()
        return self.value


def _wkv_prepare(raw, masks, consts):
    bd_mask, strict, incl, eye = masks
    tril = consts
    c = RWKV_CHUNK
    bd = lambda x: _block_diag(x, bd_mask)
    cum = [_mm_split_lhs_exact(tril, lw) for r, lw, k, v, kkn, a in raw]
    yield
    units = []
    for (r, lw, k, v, kkn, a), cum_i in zip(raw, cum):
        g_t = jnp.exp(cum_i)
        g_inv = jnp.exp(-cum_i)
        units.append((r * g_t, -kkn * jnp.exp(cum_i - lw), kkn * a * g_inv, k * g_inv, v, g_t[c - 1:c]))
    ar = [jnp.concatenate([at, rt], axis=0).astype(BF16) for rt, at, bt, kt, v, gl in units]
    sv = [bd(u[4]) for u in units]
    lb = [_mm_nt(x, bd(u[2])) for x, u in zip(ar, units)]
    lk = [_mm_nt(x, bd(u[3])) for x, u in zip(ar, units)]
    yield
    l_ab = [jnp.where(strict, x[:c], 0.0) for x in lb]
    l_rb = [jnp.where(incl, x[c:], 0.0) for x in lb]
    l_ak = [jnp.where(strict, x[:c], 0.0) for x in lk]
    l_rk = [jnp.where(incl, x[c:], 0.0) for x in lk]
    pw = [_mm(x, bd(x)) for x in l_ab]
    inv = [eye + x for x in l_ab]
    yield
    for _ in range(4):
        both = [_mm(jnp.concatenate([p, t], axis=0), bd(p)) for p, t in zip(pw, inv)]
        pw = [x[:c] for x in both]
        inv = [t + x[c:] for t, x in zip(inv, both)]
        yield
    inv = [t + _mm(t, bd(p)) for t, p in zip(inv, pw)]
    kv = [_mm(jnp.concatenate([x, y], axis=0), s) for x, y, s in zip(l_ak, l_rk, sv)]
    ak_v = [x[:c] for x in kv]
    rk_v = [x[c:] for x in kv]
    bk = [jnp.concatenate([bt, kt], axis=0).astype(BF16) for rt, at, bt, kt, v, gl in units]
    return [dict(ar=ar[i], inv=inv[i], l_rb=l_rb[i], ak_v=ak_v[i], rk_v=rk_v[i], bk=bk[i], v=units[i][4],
                 g_last=units[i][5]) for i in range(len(units))]


def _wkv_advance(get_prepared, n_chunks, z_states, masks):
    bd_mask = masks[0]
    c = RWKV_CHUNK
    bd = lambda x: _block_diag(x, bd_mask)
    ys = []
    for i in range(n_chunks):
        prepared = get_prepared(i)
        ars = [_mm_nt(p["ar"], z) for p, z in zip(prepared, z_states)]
        yield
        u = [_mm(p["inv"], bd(x[:c] + p["ak_v"])) for p, x in zip(prepared, ars)]
        yield
        y = [x[c:] + _mm(p["l_rb"], bd(uu)) + p["rk_v"] for p, x, uu in zip(prepared, ars, u)]
        dz = [_mm_tn(jnp.concatenate([uu, p["v"]], axis=0), p["bk"]) for p, uu in zip(prepared, u)]
        z_states = [(z + jnp.where(bd_mask, d, 0.0)) * p["g_last"] for z, d, p in zip(z_states, dz, prepared)]
        ys.append(y)
        yield
    return ys, z_states


def _wkv_masks():
    gw, c = RWKV_GW, RWKV_CHUNK
    row = lax.broadcasted_iota(jnp.int32, (gw, gw), 0)
    col = lax.broadcasted_iota(jnp.int32, (gw, gw), 1)
    bd_mask = (row // RWKV_N) == (col // RWKV_N)
    t = lax.broadcasted_iota(jnp.int32, (c, gw), 0)
    j = lax.broadcasted_iota(jnp.int32, (c, gw), 1) % c
    eye = jnp.where(t == j, 1.0, 0.0).astype(F32)
    rc = lax.broadcasted_iota(jnp.int32, (c, c), 0)
    cc = lax.broadcasted_iota(jnp.int32, (c, c), 1)
    tril = jnp.where(rc >= cc, 1.0, 0.0).astype(BF16)
    bd_ones = jnp.where(bd_mask, 1.0, 0.0).astype(BF16)
    return (bd_mask, t > j, t >= j, eye), tril, bd_ones


def _rwkv_kernel(has_vres, emit_vfirst, n_chunks, tiles_per_row, n_tiles, *refs):
    names = ["res_pre", "mod_pre", "res_out", "mod_out", "ng", "mu", "vec4", "wr", "wk", "wv", "w1", "w2", "a1", "a2",
             "g1", "g2"] + (["v1", "v2", "v0", "vfirst"] if has_vres else []) + ["vec3", "wo", "out"]
    names += (["vfirst_out"] if emit_vfirst else []) + ["hext", "state", "z_buf"] + ["p_" + n for n in RWKV_STREAMS]
    ref = dict(zip(names, refs))
    assert len(names) == len(refs)
    step = pl.program_id(0)
    slot_w = step % 2
    slot_r = 1 - slot_w
    pre_tile = jnp.minimum(step, n_tiles - 1)
    wkv_tile = jnp.clip(step - 1, 0, n_tiles - 1)
    hext, state, z_buf = ref["hext"], ref["state"], ref["z_buf"]

    @pl.when(step == 0)
    def _():
        for buf in [z_buf] + [ref["p_" + n] for n in RWKV_STREAMS]:
            buf[...] = jnp.zeros(buf.shape, buf.dtype)

    @pl.when(pre_tile % tiles_per_row == 0)
    def _():
        hext[0:SUBLANES, :] = jnp.zeros((SUBLANES, hext.shape[1]), F32)

    @pl.when(wkv_tile % tiles_per_row == 0)
    def _():
        state[...] = jnp.zeros(state.shape, F32)

    def store(name, cols, value):
        dst = ref["p_" + name]
        dst[slot_w, :, cols] = value.astype(dst.dtype)

    pre = _Staged(_rwkv_pre_stages(has_vres, ref["res_pre"][0], ref["mod_pre"][0], ref["ng"], ref["mu"], ref["vec4"],
                                   ref, ref["vfirst"][0] if has_vres else None, hext, store))

    c = RWKV_CHUNK
    n_groups = state.shape[0]
    masks, tril, bd_ones = _wkv_masks()
    lanes = [slice(gi * RWKV_GW, (gi + 1) * RWKV_GW) for gi in range(n_groups)]
    tb = n_chunks * c
    vec3 = ref["vec3"]

    def head_sums(per_group):
        s = _mm(jnp.concatenate(per_group, axis=0), bd_ones)
        return [s[gi * tb:(gi + 1) * tb] for gi in range(n_groups)]

    load = lambda n: [ref["p_" + n][slot_r, :, ln].astype(F32) for ln in lanes]
    r, lw, k, v, kk, a = (load(n) for n in RWKV_STREAMS[:6])
    kkn = [x / jnp.maximum(jnp.sqrt(n2), 1e-12) for x, n2 in zip(kk, head_sums([x * x for x in kk]))]
    bonus = head_sums([r[gi] * k[gi] * vec3[0:1, lanes[gi]] for gi in range(n_groups)])
    y_out = jnp.dot(z_buf[slot_r], ref["wo"][...], preferred_element_type=F32)
    ref["out"][0] = _sublayer_out(ref["res_out"][0], y_out, ref["mod_out"][0], ref["ng"][...], 1, 1.0)
    if emit_vfirst:
        ref["vfirst_out"][0] = ref["p_v"][slot_w]
    pre.step()

    def raw_units(i):
        rows = slice(i * c, (i + 1) * c)
        return [tuple(x[gi][rows] for x in (r, lw, k, v, kkn, a)) for gi in range(n_groups)]

    lead = min(RWKV_LEAD_CHUNKS, n_chunks)
    first_task = _Staged(_wkv_prepare([u for i in range(lead) for u in raw_units(i)], masks, tril))
    while not first_task.done:
        first_task.step()
        pre.step()
    first = first_task.value
    later = [_Staged(_wkv_prepare(raw_units(i), masks, tril)) for i in range(lead, n_chunks)]

    def get_prepared(i):
        if i < lead:
            return first[i * n_groups:(i + 1) * n_groups]
        return later[i - lead].finish()

    advance = _Staged(_wkv_advance(get_prepared, n_chunks, [state[gi] for gi in range(n_groups)], masks))
    while not advance.done:
        advance.step()
        for task in later:
            if not task.done:
                task.step()
                break
        pre.step()
    ys, z_states = advance.value
    for gi in range(n_groups):
        state[gi] = z_states[gi]
    y = [jnp.concatenate([ys[i][gi] for i in range(n_chunks)], axis=0) for gi in range(n_groups)]
    yc = [x - m * (1.0 / RWKV_N) for x, m in zip(y, head_sums(y))]
    pre.step()
    var = head_sums([x * x for x in yc])
    pre.step()
    for gi, ln in enumerate(lanes):
        yn = yc[gi] * lax.rsqrt(var[gi] * (1.0 / RWKV_N) + RWKV_GN_EPS) * vec3[1:2, ln] + vec3[2:3, ln]
        z_buf[slot_w, :, ln] = ((yn + bonus[gi] * v[gi]) * ref["p_g"][slot_r, :, ln].astype(F32)).astype(BF16)
    pre.finish()


def _rwkv_sublayer(res, mod, ng, p, v_first, emit_vfirst):
    bsz, t_len, d = res.shape
    n_groups = d // RWKV_GW
    tb = _tile(t_len, RWKV_TILE)
    per_row = t_len // tb
    n_tiles = bsz * per_row
    has_vres = v_first is not None
    pre = lambda s: jnp.minimum(s, n_tiles - 1)
    done = lambda s: jnp.maximum(s - 2, 0)
    tok_pre = pl.BlockSpec((1, tb, d), lambda s: (pre(s) // per_row, pre(s) % per_row, 0))
    tok_out = pl.BlockSpec((1, tb, d), lambda s: (done(s) // per_row, done(s) % per_row, 0))
    mod_pre = pl.BlockSpec((1, 3 * N_SUB, d), lambda s: (pre(s) // per_row, 0, 0))
    mod_out = pl.BlockSpec((1, 3 * N_SUB, d), lambda s: (done(s) // per_row, 0, 0))
    lora_in = lambda w: _pad_cols(w, LORA_PAD).astype(BF16)
    lora_out = lambda w: _pad_rows(w, LORA_PAD).astype(BF16)
    vec4 = jnp.stack([p["w0"], p["a0"], p["k_k"], p["k_a"]])
    vec3 = jnp.stack([p["r_k"].reshape(d), p["ln_w"], p["ln_b"]])
    weights = [ng, p["mu"], vec4, p["w_rkv"][0].astype(BF16), p["w_rkv"][1].astype(BF16), p["w_rkv"][2].astype(BF16),
               lora_in(p["w1"]), lora_out(p["w2"]), lora_in(p["a1"]), lora_out(p["a2"]),
               lora_in(p["g1"]), lora_out(p["g2"])]
    args = [res, mod, res, mod] + weights
    in_specs = [tok_pre, mod_pre, tok_out, mod_out] + [_resident(x.shape) for x in weights]
    if has_vres:
        extra = [lora_in(p["v1"]), lora_out(p["v2"]), p["v0"].reshape(1, d)]
        args += extra + [v_first]
        in_specs += [_resident(x.shape) for x in extra] + [tok_pre]
    tail = [vec3, p["w_o"].astype(BF16)]
    args += tail
    in_specs += [_resident(x.shape) for x in tail]
    out_shape = [jax.ShapeDtypeStruct(res.shape, F32)]
    out_specs = [tok_out]
    if emit_vfirst:
        out_shape.append(jax.ShapeDtypeStruct(res.shape, BF16))
        out_specs.append(tok_out)
    streams = [pltpu.VMEM((2, tb, d), F32 if n == "lw" else BF16) for n in RWKV_STREAMS]
    outs = pl.pallas_call(
        functools.partial(_rwkv_kernel, has_vres, emit_vfirst, tb // RWKV_CHUNK, per_row, n_tiles),
        grid=(n_tiles + 2,),
        in_specs=in_specs,
        out_specs=out_specs,
        out_shape=out_shape,
        scratch_shapes=[pltpu.VMEM((tb + SUBLANES, d), F32), pltpu.VMEM((n_groups, RWKV_GW, RWKV_GW), F32),
                        pltpu.VMEM((2, tb, d), BF16)] + streams,
        compiler_params=_params(1),
        name="rwkv_sublayer",
    )(*args)
    return outs[0], (outs[1] if emit_vfirst else None)


def _ret_pre_kernel(res_ref, mod_ref, ng_ref, pos_ref, inv_ref, w_ref, q_ref, k_ref, v_ref, g_ref):
    tm = res_ref.shape[1]
    half = RET_DK // 2
    qk = RET_H * RET_DK
    nv = RET_H * RET_DV
    mod = mod_ref[0]
    ng = ng_ref[...]
    for part in range(RET_PRE_PARTS):
        rows = slice(part * tm // RET_PRE_PARTS, (part + 1) * tm // RET_PRE_PARTS)
        h = _sublayer_in(res_ref[0, rows, :], mod, ng, 1).astype(BF16)
        for hd in range(RET_H):
            lo = 2 * qk + hd * RET_DV
            v_ref[0, rows, hd * RET_DV:(hd + 1) * RET_DV] = jnp.dot(
                h, w_ref[:, lo:lo + RET_DV], preferred_element_type=F32).astype(BF16)
            g_ref[0, rows, hd * RET_DV:(hd + 1) * RET_DV] = jnp.dot(
                h, w_ref[:, lo + nv:lo + nv + RET_DV], preferred_element_type=F32).astype(BF16)
        ang = pos_ref[0, rows, :] * inv_ref[...]
        cos = jnp.cos(ang)
        sin = jnp.sin(ang)
        for which, out_ref, scale in ((0, q_ref, 1.0), (1, k_ref, RET_DK ** -0.5)):
            for hd in range(RET_H):
                lo = which * qk + hd * RET_DK
                x = jnp.dot(h, w_ref[:, lo:lo + RET_DK], preferred_element_type=F32)
                x1, x2 = x[:, :half], x[:, half:]
                out_ref[0, rows, hd * RET_DK:hd * RET_DK + half] = ((x1 * cos - x2 * sin) * scale).astype(BF16)
                out_ref[0, rows, hd * RET_DK + half:(hd + 1) * RET_DK] = ((x1 * sin + x2 * cos) * scale).astype(BF16)


def _ret_pre(res, mod, ng, pos_f, w_in):
    bsz, t_len, d = res.shape
    tm = _tile(t_len, RET_TILE)
    qk = RET_H * RET_DK
    nv = RET_H * RET_DV
    half = RET_DK // 2
    inv = (1.0 / (ROPE_BASE ** jnp.linspace(0.0, 1.0, half, dtype=F32))).reshape(1, half)
    tok = lambda n: pl.BlockSpec((1, tm, n), lambda b, t: (b, t, 0))
    return pl.pallas_call(
        _ret_pre_kernel,
        grid=(bsz, t_len // tm),
        in_specs=[tok(d), pl.BlockSpec((1, 3 * N_SUB, d), lambda b, t: (b, 0, 0)), _resident(ng.shape),
                  tok(1), _resident((1, half)), _resident(w_in.shape)],
        out_specs=[tok(qk), tok(qk), tok(nv), tok(nv)],
        out_shape=[jax.ShapeDtypeStruct((bsz, t_len, qk), BF16), jax.ShapeDtypeStruct((bsz, t_len, qk), BF16),
                   jax.ShapeDtypeStruct((bsz, t_len, nv), BF16), jax.ShapeDtypeStruct((bsz, t_len, nv), BF16)],
        compiler_params=_params(2),
        name="ret_pre",
    )(res, mod, ng, pos_f, inv, w_in)


def _ret_mix_kernel(n_chunks, q_ref, k_ref, v_ref, g_ref, res_ref, mod_ref, ng_ref, wo_ref, out_ref, state, z_buf):
    @pl.when(pl.program_id(1) == 0)
    def _():
        state[...] = jnp.zeros(state.shape, F32)

    c = RET_CHUNK
    row = lax.broadcasted_iota(jnp.int32, (c, c), 0)
    col = lax.broadcasted_iota(jnp.int32, (c, c), 1)
    diff = (row - col).astype(F32)
    idx = lax.broadcasted_iota(jnp.int32, (c, 1), 0).astype(F32)
    heads = range(RET_H)
    log_gamma = [math.log(1.0 - 2.0 ** (-5.0 - hd)) for hd in heads]
    inner = [jnp.where(diff >= 0, jnp.exp(lg * jnp.maximum(diff, 0.0)), 0.0) for lg in log_gamma]
    q_decay = [jnp.exp(lg * (idx + 1.0)) for lg in log_gamma]
    k_decay = [jnp.exp(lg * (c - 1.0 - idx)) for lg in log_gamma]
    chunk_decay = [math.exp(lg * c) for lg in log_gamma]
    dk = lambda hd: slice(hd * RET_DK, (hd + 1) * RET_DK)
    dv = lambda hd: slice(hd * RET_DV, (hd + 1) * RET_DV)
    r_state = [state[hd] for hd in heads]
    outs = []
    for i in range(n_chunks):
        rows = slice(i * c, (i + 1) * c)
        q_c = [q_ref[0, rows, dk(hd)] for hd in heads]
        k_c = [k_ref[0, rows, dk(hd)] for hd in heads]
        v_c = [v_ref[0, rows, dv(hd)] for hd in heads]
        s = [_mm_nt(q_c[hd], k_c[hd]) * inner[hd] for hd in heads]
        cross = [_mm(q_c[hd], r_state[hd]) * q_decay[hd] for hd in heads]
        o = [_mm(s[hd], v_c[hd]) + cross[hd] for hd in heads]
        kv = [_mm_tn(k_c[hd].astype(F32) * k_decay[hd], v_c[hd]) for hd in heads]
        r_state = [r_state[hd] * chunk_decay[hd] + kv[hd] for hd in heads]
        outs.append(o)
    for hd in heads:
        state[hd] = r_state[hd]
    mod = mod_ref[0]
    ng = ng_ref[...]
    for i, o in enumerate(outs):
        rows = slice(i * c, (i + 1) * c)
        for hd in heads:
            oc = o[hd] - jnp.mean(o[hd], axis=-1, keepdims=True)
            on = oc * lax.rsqrt(jnp.mean(oc * oc, axis=-1, keepdims=True) + HEAD_NORM_EPS)
            gt = g_ref[0, rows, dv(hd)].astype(F32)
            z_buf[rows, dv(hd)] = (gt * _sigmoid(gt) * on).astype(BF16)
        y = jnp.dot(z_buf[rows, :], wo_ref[...], preferred_element_type=F32)
        out_ref[0, rows, :] = _sublayer_out(res_ref[0, rows, :], y, mod, ng, 1, 1.0)


def _ret_mix(q, k, v, g, res, mod, ng, w_o):
    bsz, t_len, d = res.shape
    tm = _tile(t_len, RET_TILE)
    qk = RET_H * RET_DK
    nv = RET_H * RET_DV
    tok = lambda n: pl.BlockSpec((1, tm, n), lambda b, t: (b, t, 0))
    return pl.pallas_call(
        functools.partial(_ret_mix_kernel, tm // RET_CHUNK),
        grid=(bsz, t_len // tm),
        in_specs=[tok(qk), tok(qk), tok(nv), tok(nv), tok(d),
                  pl.BlockSpec((1, 3 * N_SUB, d), lambda b, t: (b, 0, 0)), _resident(ng.shape), _resident(w_o.shape)],
        out_specs=tok(d),
        out_shape=jax.ShapeDtypeStruct(res.shape, F32),
        scratch_shapes=[pltpu.VMEM((RET_H, RET_DK, RET_DV), F32), pltpu.VMEM((tm, nv), BF16)],
        compiler_params=_params(2),
        name="ret_mix",
    )(q, k, v, g, res, mod, ng, w_o)


def _lru_kernel(res_ref, mod_ref, ng_ref, win_ref, cw_ref, vec_ref, gw_ref, wo_ref, out_ref, xext, h_carry):
    tm = res_ref.shape[1]
    width = xext.shape[1]

    @pl.when(pl.program_id(1) == 0)
    def _():
        xext[0:SUBLANES, :] = jnp.zeros((SUBLANES, width), F32)
        h_carry[...] = jnp.zeros(h_carry.shape, F32)

    mod = mod_ref[0]
    ng = ng_ref[...]
    cw = cw_ref[...]
    vec = vec_ref[...]
    conv_b, gate_bi, gate_br, lam = vec[0:1], vec[1:2], vec[2:3], vec[3:4]
    neg_lam = -lam
    softplus = jnp.maximum(neg_lam, 0.0) + jnp.log1p(jnp.exp(-jnp.abs(neg_lam)))
    parts = [slice(i * tm // LRU_ROW_PARTS, (i + 1) * tm // LRU_ROW_PARTS) for i in range(LRU_ROW_PARTS)]
    res = [res_ref[0, rows, :] for rows in parts]
    h = [_sublayer_in(x, mod, ng, 1).astype(BF16) for x in res]
    gate_branch = [jnp.dot(x, win_ref[:, :width], preferred_element_type=F32) for x in h]
    xb = [jnp.dot(x, win_ref[:, width:], preferred_element_type=F32) for x in h]
    for rows, x in zip(parts, xb):
        xext[SUBLANES + rows.start:SUBLANES + rows.stop, :] = x
    xc = []
    for rows, x in zip(parts, xb):
        acc = conv_b + cw[CONV_W - 1:CONV_W] * x
        for j in range(CONV_W - 1):
            lo = SUBLANES - (CONV_W - 1) + j
            acc = acc + cw[j:j + 1] * xext[lo + rows.start:lo + rows.stop, :]
        xc.append(acc)
    xext[0:SUBLANES, :] = xext[tm:tm + SUBLANES, :]

    a, u = [], []
    for x in xc:
        xcb = x.astype(BF16)
        gates = [jnp.concatenate([jnp.dot(xcb[:, hd * LRU_BW:(hd + 1) * LRU_BW], gw_ref[gi * LRU_H + hd],
                                          preferred_element_type=F32) for hd in range(LRU_H)], axis=1)
                 for gi in range(2)]
        i_gate = _sigmoid(gates[0] + gate_bi)
        r_gate = _sigmoid(gates[1] + gate_br)
        log_a = -LRU_C * r_gate * softplus
        a_p = jnp.exp(log_a)
        a.append(a_p)
        u.append(jnp.sqrt(-jnp.tanh(log_a) * (a_p * a_p + 1.0)) * (i_gate * x))

    row = lax.broadcasted_iota(jnp.int32, (SUBLANES, width), 0)
    carry = h_carry[...]
    hs = []
    for a_p, u_p in zip(a, u):
        groups = []
        for i in range(a_p.shape[0] // SUBLANES):
            ag = a_p[i * SUBLANES:(i + 1) * SUBLANES]
            ug = u_p[i * SUBLANES:(i + 1) * SUBLANES]
            for d in (1, 2, 4):
                keep = row >= d
                u_prev = jnp.where(keep, pltpu.roll(ug, d, 0), 0.0)
                a_prev = jnp.where(keep, pltpu.roll(ag, d, 0), 1.0)
                ug = ug + ag * u_prev
                ag = ag * a_prev
            hg = ug + ag * carry
            groups.append(hg)
            carry = jnp.broadcast_to(hg[SUBLANES - 1:SUBLANES, :], (SUBLANES, width))
        hs.append(jnp.concatenate(groups, axis=0))
    h_carry[...] = carry

    for rows, x, gb, hp in zip(parts, res, gate_branch, hs):
        gelu = 0.5 * gb * (1.0 + jnp.tanh(math.sqrt(2.0 / math.pi) * (gb + 0.044715 * (gb * gb * gb))))
        y = jnp.dot((gelu * hp).astype(BF16), wo_ref[...], preferred_element_type=F32)
        out_ref[0, rows, :] = _sublayer_out(x, y, mod, ng, 1, 1.0)


def _lru_sublayer(res, mod, ng, w_in, conv_w, vec, gate_w, w_o):
    bsz, t_len, d = res.shape
    width = w_o.shape[0]
    tm = _tile(t_len, LRU_TILE)
    tok = pl.BlockSpec((1, tm, d), lambda b, t: (b, t, 0))
    return pl.pallas_call(
        _lru_kernel,
        grid=(bsz, t_len // tm),
        in_specs=[tok, pl.BlockSpec((1, 3 * N_SUB, d), lambda b, t: (b, 0, 0)), _resident(ng.shape),
                  _resident(w_in.shape), _resident(conv_w.shape), _resident(vec.shape),
                  _resident(gate_w.shape), _resident(w_o.shape)],
        out_specs=tok,
        out_shape=jax.ShapeDtypeStruct(res.shape, F32),
        scratch_shapes=[pltpu.VMEM((tm + SUBLANES, width), F32), pltpu.VMEM((SUBLANES, width), F32)],
        compiler_params=_params(2),
        name="lru_sublayer",
    )(res, mod, ng, w_in, conv_w, vec, gate_w, w_o)


def kernel(x, c, positions, ada_w, ada_b, norm_g, ffn_w_in, ffn_w_out, rwkv_mu, rwkv_w_rkv, rwkv_w0, rwkv_w1, rwkv_w2, rwkv_a0, rwkv_a1, rwkv_a2, rwkv_g1, rwkv_g2, rwkv_k_k, rwkv_k_a, rwkv_r_k, rwkv_ln_w, rwkv_ln_b, rwkv_w_o, rwkv_v0, rwkv_v1, rwkv_v2, ret_w_in, ret_w_o, lru_w_in, lru_conv_w, lru_conv_b, lru_gate_w, lru_gate_b, lru_lambda, lru_w_o):
    depth = ada_w.shape[0]
    d_ff = ffn_w_out.shape[2]
    res = x.astype(F32)
    mod = _ada_mod(c.astype(F32), ada_w, ada_b)
    pos_f = positions.astype(F32)[..., None]
    v_first = None
    for i in range(depth):
        ng = norm_g[i]
        ffn = lambda res, m, s: _ffn_sublayer(
            res, mod[i], ng, ffn_w_in[i, m, :, :d_ff].astype(BF16), ffn_w_in[i, m, :, d_ff:].astype(BF16),
            ffn_w_out[i, m].astype(BF16), s)
        res = ffn(res, 0, 0)
        kind, j = i % 3, i // 3
        if kind == 0:
            p = dict(mu=rwkv_mu[j], w_rkv=rwkv_w_rkv[j], w0=rwkv_w0[j], w1=rwkv_w1[j], w2=rwkv_w2[j],
                     a0=rwkv_a0[j], a1=rwkv_a1[j], a2=rwkv_a2[j], g1=rwkv_g1[j], g2=rwkv_g2[j],
                     k_k=rwkv_k_k[j], k_a=rwkv_k_a[j], r_k=rwkv_r_k[j], ln_w=rwkv_ln_w[j], ln_b=rwkv_ln_b[j])
            if j > 0:
                p.update(v0=rwkv_v0[j - 1], v1=rwkv_v1[j - 1], v2=rwkv_v2[j - 1])
            p["w_o"] = rwkv_w_o[j]
            res, v_out = _rwkv_sublayer(res, mod[i], ng, p, v_first if j > 0 else None, j == 0)
            if j == 0:
                v_first = v_out
        elif kind == 1:
            q, k, v, g = _ret_pre(res, mod[i], ng, pos_f, ret_w_in[j].astype(BF16))
            res = _ret_mix(q, k, v, g, res, mod[i], ng, ret_w_o[j].astype(BF16))
        else:
            width = lru_w_o.shape[1]
            vec = jnp.stack([lru_conv_b[j], lru_gate_b[j, 0].reshape(width), lru_gate_b[j, 1].reshape(width),
                             lru_lambda[j]])
            gate_w = lru_gate_w[j].reshape(2 * LRU_H, LRU_BW, LRU_BW).astype(BF16)
            res = _lru_sublayer(res, mod[i], ng, lru_w_in[j].astype(BF16), lru_conv_w[j], vec, gate_w,
                                lru_w_o[j].astype(BF16))
        res = ffn(res, 1, 2)
    return res.astype(x.dtype)
```

```python
import functools
import math

import jax
import jax.numpy as jnp
from jax import lax
from jax.experimental import pallas as pl
from jax.experimental.pallas import tpu as pltpu

F32 = jnp.float32
BF16 = jnp.bfloat16

NORM_EPS = 1e-6
N_SUB = 3

MXU_WIDTH = 256
SUBLANES = 8

RWKV_N = 64
RWKV_GROUP = MXU_WIDTH // RWKV_N
RWKV_GW = RWKV_GROUP * RWKV_N
RWKV_CHUNK = 64
RWKV_GN_EPS = 64e-5
LORA_PAD = 128

RET_H = 4
RET_DK = 256
RET_DV = 512
RET_CHUNK = 256
ROPE_BASE = 10000.0
HEAD_NORM_EPS = 1e-5

LRU_H = 5
LRU_BW = 256
CONV_W = 4
LRU_C = 8.0

VMEM_LIMIT = 56 * 1024 * 1024
ADA_TILE = 1536
FFN_TILE = 1024
FFN_ROW_PARTS = 4
RWKV_TILE = 256
RWKV_LEAD_CHUNKS = 4
RET_TILE = 512
RET_PRE_PARTS = 2
LRU_TILE = 512
LRU_ROW_PARTS = 4

NT_DIMS = (((1,), (1,)), ((), ()))
TN_DIMS = (((0,), (0,)), ((), ()))


def _mm(a, b):
    return jnp.dot(a.astype(BF16), b.astype(BF16), preferred_element_type=F32)


def _mm_nt(a, b):
    return lax.dot_general(a.astype(BF16), b.astype(BF16), NT_DIMS, preferred_element_type=F32)


def _mm_tn(a, b):
    return lax.dot_general(a.astype(BF16), b.astype(BF16), TN_DIMS, preferred_element_type=F32)


def _mm_split_lhs_exact(w, x):
    hi = x.astype(BF16)
    lo = (x - hi.astype(F32)).astype(BF16)
    return (jnp.dot(w, hi, preferred_element_type=F32) + jnp.dot(w, lo, preferred_element_type=F32))


def _rms(x, g):
    return x * lax.rsqrt(jnp.mean(x * x, axis=-1, keepdims=True) + NORM_EPS) * g


def _sublayer_in(res, mod, ng, s):
    return _rms(res, ng[2 * s:2 * s + 1]) * (1.0 + mod[3 * s + 1:3 * s + 2]) + mod[3 * s:3 * s + 1]


def _sublayer_out(res, y, mod, ng, s, weight):
    return res + weight * mod[3 * s + 2:3 * s + 3] * _rms(y, ng[2 * s + 1:2 * s + 2])


def _sigmoid(x):
    return jax.nn.sigmoid(x)


def _resident(shape):
    nd = len(shape)
    return pl.BlockSpec(shape, lambda *_: (0,) * nd, pipeline_mode=pl.Buffered(1))


def _params(n_axes):
    return pltpu.CompilerParams(dimension_semantics=("arbitrary",) * n_axes, vmem_limit_bytes=VMEM_LIMIT)


def _tile(t_len, want):
    tm = min(want, t_len)
    assert t_len % tm == 0, (t_len, tm)
    return tm


def _ada_kernel(c_ref, w_ref, b_ref, o_ref):
    c = c_ref[...]
    cond = c * _sigmoid(c)
    o_ref[0] = _mm(cond, w_ref[0]) + b_ref[0]


def _ada_mod(c, ada_w, ada_b):
    depth, d, n = ada_w.shape
    bsz = c.shape[0]
    tn = ADA_TILE
    assert n % tn == 0
    out = pl.pallas_call(
        _ada_kernel,
        grid=(depth, n // tn),
        in_specs=[pl.BlockSpec((bsz, d), lambda l, j: (0, 0)),
                  pl.BlockSpec((1, d, tn), lambda l, j: (l, 0, j)),
                  pl.BlockSpec((1, 1, tn), lambda l, j: (l, 0, j))],
        out_specs=pl.BlockSpec((1, bsz, tn), lambda l, j: (l, 0, j)),
        out_shape=jax.ShapeDtypeStruct((depth, bsz, n), F32),
        compiler_params=_params(2),
        name="ada_mod",
    )(c, ada_w, ada_b.reshape(depth, 1, n))
    return out.reshape(depth, bsz, 3 * N_SUB, d)


def _ffn_kernel(s, res_ref, mod_ref, ng_ref, wa_ref, wb_ref, wo_ref, out_ref):
    mod = mod_ref[0]
    ng = ng_ref[...]
    f = wa_ref.shape[1]
    cut = -(-f // (2 * MXU_WIDTH)) * MXU_WIDTH
    chunks = [slice(0, cut), slice(cut, f)]
    tm = res_ref.shape[1]
    parts = [slice(i * tm // FFN_ROW_PARTS, (i + 1) * tm // FFN_ROW_PARTS) for i in range(FFN_ROW_PARTS)]
    res = [res_ref[0, rows, :] for rows in parts]
    h = [_sublayer_in(x, mod, ng, s).astype(BF16) for x in res]
    y = [jnp.zeros(x.shape, F32) for x in res]
    for cols in chunks:
        a = [jnp.dot(x, wa_ref[:, cols], preferred_element_type=F32) for x in h]
        b = [jnp.dot(x, wb_ref[:, cols], preferred_element_type=F32) for x in h]
        z = [(u * _sigmoid(u) * v).astype(BF16) for u, v in zip(a, b)]
        y = [acc + jnp.dot(x, wo_ref[cols, :], preferred_element_type=F32) for acc, x in zip(y, z)]
    for i, rows in enumerate(parts):
        out_ref[0, rows, :] = _sublayer_out(res[i], y[i], mod, ng, s, 0.5)


def _ffn_sublayer(res, mod, ng, w_a, w_b, w_o, s):
    bsz, t_len, d = res.shape
    f = w_a.shape[1]
    tm = _tile(t_len, FFN_TILE)
    return pl.pallas_call(
        functools.partial(_ffn_kernel, s),
        grid=(bsz, t_len // tm),
        in_specs=[pl.BlockSpec((1, tm, d), lambda b, t: (b, t, 0)),
                  pl.BlockSpec((1, 3 * N_SUB, d), lambda b, t: (b, 0, 0)),
                  _resident(ng.shape), _resident((d, f)), _resident((d, f)), _resident((f, d))],
        out_specs=pl.BlockSpec((1, tm, d), lambda b, t: (b, t, 0)),
        out_shape=jax.ShapeDtypeStruct(res.shape, F32),
        compiler_params=_params(2),
        name="ffn_sublayer",
    )(res, mod, ng, w_a, w_b, w_o)


RWKV_STREAMS = ("r", "lw", "k", "v", "kk", "a", "g")


def _rwkv_pre_stages(has_vres, res, mod, ng_ref, mu_ref, vec_ref, w, vfirst, hext, store):
    tm, d = res.shape
    h = _sublayer_in(res, mod, ng_ref[...], 1)
    hext[SUBLANES:SUBLANES + tm, :] = h
    dx = hext[SUBLANES - 1:SUBLANES - 1 + tm, :] - h
    hext[0:SUBLANES, :] = hext[tm:tm + SUBLANES, :]
    mu = mu_ref[...]
    xr, xw, xk, xv, xa, xg = ((h + dx * mu[j:j + 1]).astype(BF16) for j in range(6))
    vec = vec_ref[...]
    w0, a0, k_k, k_a = vec[0:1], vec[1:2], vec[2:3], vec[3:4]
    full = slice(0, d)
    yield
    tw = jnp.tanh(jnp.dot(xw, w["w1"][...], preferred_element_type=F32))
    ta = jnp.dot(xa, w["a1"][...], preferred_element_type=F32)
    yield
    tg = _sigmoid(jnp.dot(xg, w["g1"][...], preferred_element_type=F32))
    if has_vres:
        tv = jnp.dot(xv, w["v1"][...], preferred_element_type=F32)
    yield
    store("lw", full, -math.exp(-0.5) * _sigmoid(w0 + _mm(tw, w["w2"][...])))
    a = _sigmoid(a0 + _mm(ta, w["a2"][...]))
    store("a", full, a)
    yield
    store("g", full, _mm(tg, w["g2"][...]))
    if has_vres:
        mix = _sigmoid(w["v0"][...] + _mm(tv, w["v2"][...]))
    yield
    for lo in range(0, d, RWKV_GW):
        cols = slice(lo, lo + RWKV_GW)
        store("r", cols, jnp.dot(xr, w["wr"][:, cols], preferred_element_type=F32))
        yield
    for lo in range(0, d, RWKV_GW):
        cols = slice(lo, lo + RWKV_GW)
        k = jnp.dot(xk, w["wk"][:, cols], preferred_element_type=F32)
        store("kk", cols, k * k_k[:, cols])
        store("k", cols, k * (1.0 + (a[:, cols] - 1.0) * k_a[:, cols]))
        yield
    for lo in range(0, d, RWKV_GW):
        cols = slice(lo, lo + RWKV_GW)
        v = jnp.dot(xv, w["wv"][:, cols], preferred_element_type=F32)
        if has_vres:
            v = v + (vfirst[:, cols].astype(F32) - v) * mix[:, cols]
        store("v", cols, v)
        yield


def _pad_cols(w, n):
    return jnp.pad(w, ((0, 0), (0, n - w.shape[1])))


def _pad_rows(w, n):
    return jnp.pad(w, ((0, n - w.shape[0]), (0, 0)))


def _block_diag(x, bd_mask):
    xb = x.astype(BF16)
    return jnp.where(bd_mask, jnp.concatenate([xb] * RWKV_GROUP, axis=0), jnp.zeros((), BF16))


class _Staged:
    def __init__(self, gen):
        self.gen, self.done, self.value = gen, False, None

    def step(self):
        if not self.done:
            try:
                next(self.gen)
            except StopIteration as stop:
                self.done, self.value = True, stop.value

    def finish(self):
        while not self.done:
            self.step()
        return self.value


def _wkv_prepare(raw, masks, consts):
    bd_mask, strict, incl, eye = masks
    tril = consts
    c = RWKV_CHUNK
    bd = lambda x: _block_diag(x, bd_mask)
    cum = [_mm_split_lhs_exact(tril, lw) for r, lw, k, v, kkn, a in raw]
    yield
    units = []
    for (r, lw, k, v, kkn, a), cum_i in zip(raw, cum):
        g_t = jnp.exp(cum_i)
        g_inv = jnp.exp(-cum_i)
        units.append((r * g_t, -kkn * jnp.exp(cum_i - lw), kkn * a * g_inv, k * g_inv, v, g_t[c - 1:c]))
    ar = [jnp.concatenate([at, rt], axis=0).astype(BF16) for rt, at, bt, kt, v, gl in units]
    sv = [bd(u[4]) for u in units]
    lb = [_mm_nt(x, bd(u[2])) for x, u in zip(ar, units)]
    lk = [_mm_nt(x, bd(u[3])) for x, u in zip(ar, units)]
    yield
    l_ab = [jnp.where(strict, x[:c], 0.0) for x in lb]
    l_rb = [jnp.where(incl, x[c:], 0.0) for x in lb]
    l_ak = [jnp.where(strict, x[:c], 0.0) for x in lk]
    l_rk = [jnp.where(incl, x[c:], 0.0) for x in lk]
    pw = [_mm(x, bd(x)) for x in l_ab]
    inv = [eye + x for x in l_ab]
    yield
    for _ in range(4):
        both = [_mm(jnp.concatenate([p, t], axis=0), bd(p)) for p, t in zip(pw, inv)]
        pw = [x[:c] for x in both]
        inv = [t + x[c:] for t, x in zip(inv, both)]
        yield
    inv = [t + _mm(t, bd(p)) for t, p in zip(inv, pw)]
    kv = [_mm(jnp.concatenate([x, y], axis=0), s) for x, y, s in zip(l_ak, l_rk, sv)]
    ak_v = [x[:c] for x in kv]
    rk_v = [x[c:] for x in kv]
    bk = [jnp.concatenate([bt, kt], axis=0).astype(BF16) for rt, at, bt, kt, v, gl in units]
    return [dict(ar=ar[i], inv=inv[i], l_rb=l_rb[i], ak_v=ak_v[i], rk_v=rk_v[i], bk=bk[i], v=units[i][4],
                 g_last=units[i][5]) for i in range(len(units))]


def _wkv_advance(get_prepared, n_chunks, z_states, masks):
    bd_mask = masks[0]
    c = RWKV_CHUNK
    bd = lambda x: _block_diag(x, bd_mask)
    ys = []
    for i in range(n_chunks):
        prepared = get_prepared(i)
        ars = [_mm_nt(p["ar"], z) for p, z in zip(prepared, z_states)]
        yield
        u = [_mm(p["inv"], bd(x[:c] + p["ak_v"])) for p, x in zip(prepared, ars)]
        yield
        y = [x[c:] + _mm(p["l_rb"], bd(uu)) + p["rk_v"] for p, x, uu in zip(prepared, ars, u)]
        dz = [_mm_tn(jnp.concatenate([uu, p["v"]], axis=0), p["bk"]) for p, uu in zip(prepared, u)]
        z_states = [(z + jnp.where(bd_mask, d, 0.0)) * p["g_last"] for z, d, p in zip(z_states, dz, prepared)]
        ys.append(y)
        yield
    return ys, z_states


def _wkv_masks():
    gw, c = RWKV_GW, RWKV_CHUNK
    row = lax.broadcasted_iota(jnp.int32, (gw, gw), 0)
    col = lax.broadcasted_iota(jnp.int32, (gw, gw), 1)
    bd_mask = (row // RWKV_N) == (col // RWKV_N)
    t = lax.broadcasted_iota(jnp.int32, (c, gw), 0)
    j = lax.broadcasted_iota(jnp.int32, (c, gw), 1) % c
    eye = jnp.where(t == j, 1.0, 0.0).astype(F32)
    rc = lax.broadcasted_iota(jnp.int32, (c, c), 0)
    cc = lax.broadcasted_iota(jnp.int32, (c, c), 1)
    tril = jnp.where(rc >= cc, 1.0, 0.0).astype(BF16)
    bd_ones = jnp.where(bd_mask, 1.0, 0.0).astype(BF16)
    return (bd_mask, t > j, t >= j, eye), tril, bd_ones


def _rwkv_kernel(has_vres, emit_vfirst, n_chunks, tiles_per_row, n_tiles, *refs):
    names = ["res_pre", "mod_pre", "res_out", "mod_out", "ng", "mu", "vec4", "wr", "wk", "wv", "w1", "w2", "a1", "a2",
             "g1", "g2"] + (["v1", "v2", "v0", "vfirst"] if has_vres else []) + ["vec3", "wo", "out"]
    names += (["vfirst_out"] if emit_vfirst else []) + ["hext", "state", "z_buf"] + ["p_" + n for n in RWKV_STREAMS]
    ref = dict(zip(names, refs))
    assert len(names) == len(refs)
    step = pl.program_id(0)
    slot_w = step % 2
    slot_r = 1 - slot_w
    pre_tile = jnp.minimum(step, n_tiles - 1)
    wkv_tile = jnp.clip(step - 1, 0, n_tiles - 1)
    hext, state, z_buf = ref["hext"], ref["state"], ref["z_buf"]

    @pl.when(step == 0)
    def _():
        for buf in [z_buf] + [ref["p_" + n] for n in RWKV_STREAMS]:
            buf[...] = jnp.zeros(buf.shape, buf.dtype)

    @pl.when(pre_tile % tiles_per_row == 0)
    def _():
        hext[0:SUBLANES, :] = jnp.zeros((SUBLANES, hext.shape[1]), F32)

    @pl.when(wkv_tile % tiles_per_row == 0)
    def _():
        state[...] = jnp.zeros(state.shape, F32)

    def store(name, cols, value):
        dst = ref["p_" + name]
        dst[slot_w, :, cols] = value.astype(dst.dtype)

    pre = _Staged(_rwkv_pre_stages(has_vres, ref["res_pre"][0], ref["mod_pre"][0], ref["ng"], ref["mu"], ref["vec4"],
                                   ref, ref["vfirst"][0] if has_vres else None, hext, store))

    c = RWKV_CHUNK
    n_groups = state.shape[0]
    masks, tril, bd_ones = _wkv_masks()
    lanes = [slice(gi * RWKV_GW, (gi + 1) * RWKV_GW) for gi in range(n_groups)]
    tb = n_chunks * c
    vec3 = ref["vec3"]

    def head_sums(per_group):
        s = _mm(jnp.concatenate(per_group, axis=0), bd_ones)
        return [s[gi * tb:(gi + 1) * tb] for gi in range(n_groups)]

    load = lambda n: [ref["p_" + n][slot_r, :, ln].astype(F32) for ln in lanes]
    r, lw, k, v, kk, a = (load(n) for n in RWKV_STREAMS[:6])
    kkn = [x / jnp.maximum(jnp.sqrt(n2), 1e-12) for x, n2 in zip(kk, head_sums([x * x for x in kk]))]
    bonus = head_sums([r[gi] * k[gi] * vec3[0:1, lanes[gi]] for gi in range(n_groups)])
    y_out = jnp.dot(z_buf[slot_r], ref["wo"][...], preferred_element_type=F32)
    ref["out"][0] = _sublayer_out(ref["res_out"][0], y_out, ref["mod_out"][0], ref["ng"][...], 1, 1.0)
    if emit_vfirst:
        ref["vfirst_out"][0] = ref["p_v"][slot_w]
    pre.step()

    def raw_units(i):
        rows = slice(i * c, (i + 1) * c)
        return [tuple(x[gi][rows] for x in (r, lw, k, v, kkn, a)) for gi in range(n_groups)]

    lead = min(RWKV_LEAD_CHUNKS, n_chunks)
    first_task = _Staged(_wkv_prepare([u for i in range(lead) for u in raw_units(i)], masks, tril))
    while not first_task.done:
        first_task.step()
        pre.step()
    first = first_task.value
    later = [_Staged(_wkv_prepare(raw_units(i), masks, tril)) for i in range(lead, n_chunks)]

    def get_prepared(i):
        if i < lead:
            return first[i * n_groups:(i + 1) * n_groups]
        return later[i - lead].finish()

    advance = _Staged(_wkv_advance(get_prepared, n_chunks, [state[gi] for gi in range(n_groups)], masks))
    while not advance.done:
        advance.step()
        for task in later:
            if not task.done:
                task.step()
                break
        pre.step()
    ys, z_states = advance.value
    for gi in range(n_groups):
        state[gi] = z_states[gi]
    y = [jnp.concatenate([ys[i][gi] for i in range(n_chunks)], axis=0) for gi in range(n_groups)]
    yc = [x - m * (1.0 / RWKV_N) for x, m in zip(y, head_sums(y))]
    pre.step()
    var = head_sums([x * x for x in yc])
    pre.step()
    for gi, ln in enumerate(lanes):
        yn = yc[gi] * lax.rsqrt(var[gi] * (1.0 / RWKV_N) + RWKV_GN_EPS) * vec3[1:2, ln] + vec3[2:3, ln]
        z_buf[slot_w, :, ln] = ((yn + bonus[gi] * v[gi]) * ref["p_g"][slot_r, :, ln].astype(F32)).astype(BF16)
    pre.finish()


def _rwkv_sublayer(res, mod, ng, p, v_first, emit_vfirst):
    bsz, t_len, d = res.shape
    n_groups = d // RWKV_GW
    tb = _tile(t_len, RWKV_TILE)
    per_row = t_len // tb
    n_tiles = bsz * per_row
    has_vres = v_first is not None
    pre = lambda s: jnp.minimum(s, n_tiles - 1)
    done = lambda s: jnp.maximum(s - 2, 0)
    tok_pre = pl.BlockSpec((1, tb, d), lambda s: (pre(s) // per_row, pre(s) % per_row, 0))
    tok_out = pl.BlockSpec((1, tb, d), lambda s: (done(s) // per_row, done(s) % per_row, 0))
    mod_pre = pl.BlockSpec((1, 3 * N_SUB, d), lambda s: (pre(s) // per_row, 0, 0))
    mod_out = pl.BlockSpec((1, 3 * N_SUB, d), lambda s: (done(s) // per_row, 0, 0))
    lora_in = lambda w: _pad_cols(w, LORA_PAD).astype(BF16)
    lora_out = lambda w: _pad_rows(w, LORA_PAD).astype(BF16)
    vec4 = jnp.stack([p["w0"], p["a0"], p["k_k"], p["k_a"]])
    vec3 = jnp.stack([p["r_k"].reshape(d), p["ln_w"], p["ln_b"]])
    weights = [ng, p["mu"], vec4, p["w_rkv"][0].astype(BF16), p["w_rkv"][1].astype(BF16), p["w_rkv"][2].astype(BF16),
               lora_in(p["w1"]), lora_out(p["w2"]), lora_in(p["a1"]), lora_out(p["a2"]),
               lora_in(p["g1"]), lora_out(p["g2"])]
    args = [res, mod, res, mod] + weights
    in_specs = [tok_pre, mod_pre, tok_out, mod_out] + [_resident(x.shape) for x in weights]
    if has_vres:
        extra = [lora_in(p["v1"]), lora_out(p["v2"]), p["v0"].reshape(1, d)]
        args += extra + [v_first]
        in_specs += [_resident(x.shape) for x in extra] + [tok_pre]
    tail = [vec3, p["w_o"].astype(BF16)]
    args += tail
    in_specs += [_resident(x.shape) for x in tail]
    out_shape = [jax.ShapeDtypeStruct(res.shape, F32)]
    out_specs = [tok_out]
    if emit_vfirst:
        out_shape.append(jax.ShapeDtypeStruct(res.shape, BF16))
        out_specs.append(tok_out)
    streams = [pltpu.VMEM((2, tb, d), F32 if n == "lw" else BF16) for n in RWKV_STREAMS]
    outs = pl.pallas_call(
        functools.partial(_rwkv_kernel, has_vres, emit_vfirst, tb // RWKV_CHUNK, per_row, n_tiles),
        grid=(n_tiles + 2,),
        in_specs=in_specs,
        out_specs=out_specs,
        out_shape=out_shape,
        scratch_shapes=[pltpu.VMEM((tb + SUBLANES, d), F32), pltpu.VMEM((n_groups, RWKV_GW, RWKV_GW), F32),
                        pltpu.VMEM((2, tb, d), BF16)] + streams,
        compiler_params=_params(1),
        name="rwkv_sublayer",
    )(*args)
    return outs[0], (outs[1] if emit_vfirst else None)


def _ret_pre_kernel(res_ref, mod_ref, ng_ref, pos_ref, inv_ref, w_ref, q_ref, k_ref, v_ref, g_ref):
    tm = res_ref.shape[1]
    half = RET_DK // 2
    qk = RET_H * RET_DK
    nv = RET_H * RET_DV
    mod = mod_ref[0]
    ng = ng_ref[...]
    for part in range(RET_PRE_PARTS):
        rows = slice(part * tm // RET_PRE_PARTS, (part + 1) * tm // RET_PRE_PARTS)
        h = _sublayer_in(res_ref[0, rows, :], mod, ng, 1).astype(BF16)
        for hd in range(RET_H):
            lo = 2 * qk + hd * RET_DV
            v_ref[0, rows, hd * RET_DV:(hd + 1) * RET_DV] = jnp.dot(
                h, w_ref[:, lo:lo + RET_DV], preferred_element_type=F32).astype(BF16)
            g_ref[0, rows, hd * RET_DV:(hd + 1) * RET_DV] = jnp.dot(
                h, w_ref[:, lo + nv:lo + nv + RET_DV], preferred_element_type=F32).astype(BF16)
        ang = pos_ref[0, rows, :] * inv_ref[...]
        cos = jnp.cos(ang)
        sin = jnp.sin(ang)
        for which, out_ref, scale in ((0, q_ref, 1.0), (1, k_ref, RET_DK ** -0.5)):
            for hd in range(RET_H):
                lo = which * qk + hd * RET_DK
                x = jnp.dot(h, w_ref[:, lo:lo + RET_DK], preferred_element_type=F32)
                x1, x2 = x[:, :half], x[:, half:]
                out_ref[0, rows, hd * RET_DK:hd * RET_DK + half] = ((x1 * cos - x2 * sin) * scale).astype(BF16)
                out_ref[0, rows, hd * RET_DK + half:(hd + 1) * RET_DK] = ((x1 * sin + x2 * cos) * scale).astype(BF16)


def _ret_pre(res, mod, ng, pos_f, w_in):
    bsz, t_len, d = res.shape
    tm = _tile(t_len, RET_TILE)
    qk = RET_H * RET_DK
    nv = RET_H * RET_DV
    half = RET_DK // 2
    inv = (1.0 / (ROPE_BASE ** jnp.linspace(0.0, 1.0, half, dtype=F32))).reshape(1, half)
    tok = lambda n: pl.BlockSpec((1, tm, n), lambda b, t: (b, t, 0))
    return pl.pallas_call(
        _ret_pre_kernel,
        grid=(bsz, t_len // tm),
        in_specs=[tok(d), pl.BlockSpec((1, 3 * N_SUB, d), lambda b, t: (b, 0, 0)), _resident(ng.shape),
                  tok(1), _resident((1, half)), _resident(w_in.shape)],
        out_specs=[tok(qk), tok(qk), tok(nv), tok(nv)],
        out_shape=[jax.ShapeDtypeStruct((bsz, t_len, qk), BF16), jax.ShapeDtypeStruct((bsz, t_len, qk), BF16),
                   jax.ShapeDtypeStruct((bsz, t_len, nv), BF16), jax.ShapeDtypeStruct((bsz, t_len, nv), BF16)],
        compiler_params=_params(2),
        name="ret_pre",
    )(res, mod, ng, pos_f, inv, w_in)


def _ret_mix_kernel(n_chunks, q_ref, k_ref, v_ref, g_ref, res_ref, mod_ref, ng_ref, wo_ref, out_ref, state, z_buf):
    @pl.when(pl.program_id(1) == 0)
    def _():
        state[...] = jnp.zeros(state.shape, F32)

    c = RET_CHUNK
    row = lax.broadcasted_iota(jnp.int32, (c, c), 0)
    col = lax.broadcasted_iota(jnp.int32, (c, c), 1)
    diff = (row - col).astype(F32)
    idx = lax.broadcasted_iota(jnp.int32, (c, 1), 0).astype(F32)
    heads = range(RET_H)
    log_gamma = [math.log(1.0 - 2.0 ** (-5.0 - hd)) for hd in heads]
    inner = [jnp.where(diff >= 0, jnp.exp(lg * jnp.maximum(diff, 0.0)), 0.0) for lg in log_gamma]
    q_decay = [jnp.exp(lg * (idx + 1.0)) for lg in log_gamma]
    k_decay = [jnp.exp(lg * (c - 1.0 - idx)) for lg in log_gamma]
    chunk_decay = [math.exp(lg * c) for lg in log_gamma]
    dk = lambda hd: slice(hd * RET_DK, (hd + 1) * RET_DK)
    dv = lambda hd: slice(hd * RET_DV, (hd + 1) * RET_DV)
    r_state = [state[hd] for hd in heads]
    outs = []
    for i in range(n_chunks):
        rows = slice(i * c, (i + 1) * c)
        q_c = [q_ref[0, rows, dk(hd)] for hd in heads]
        k_c = [k_ref[0, rows, dk(hd)] for hd in heads]
        v_c = [v_ref[0, rows, dv(hd)] for hd in heads]
        s = [_mm_nt(q_c[hd], k_c[hd]) * inner[hd] for hd in heads]
        cross = [_mm(q_c[hd], r_state[hd]) * q_decay[hd] for hd in heads]
        o = [_mm(s[hd], v_c[hd]) + cross[hd] for hd in heads]
        kv = [_mm_tn(k_c[hd].astype(F32) * k_decay[hd], v_c[hd]) for hd in heads]
        r_state = [r_state[hd] * chunk_decay[hd] + kv[hd] for hd in heads]
        outs.append(o)
    for hd in heads:
        state[hd] = r_state[hd]
    mod = mod_ref[0]
    ng = ng_ref[...]
    for i, o in enumerate(outs):
        rows = slice(i * c, (i + 1) * c)
        for hd in heads:
            oc = o[hd] - jnp.mean(o[hd], axis=-1, keepdims=True)
            on = oc * lax.rsqrt(jnp.mean(oc * oc, axis=-1, keepdims=True) + HEAD_NORM_EPS)
            gt = g_ref[0, rows, dv(hd)].astype(F32)
            z_buf[rows, dv(hd)] = (gt * _sigmoid(gt) * on).astype(BF16)
        y = jnp.dot(z_buf[rows, :], wo_ref[...], preferred_element_type=F32)
        out_ref[0, rows, :] = _sublayer_out(res_ref[0, rows, :], y, mod, ng, 1, 1.0)


def _ret_mix(q, k, v, g, res, mod, ng, w_o):
    bsz, t_len, d = res.shape
    tm = _tile(t_len, RET_TILE)
    qk = RET_H * RET_DK
    nv = RET_H * RET_DV
    tok = lambda n: pl.BlockSpec((1, tm, n), lambda b, t: (b, t, 0))
    return pl.pallas_call(
        functools.partial(_ret_mix_kernel, tm // RET_CHUNK),
        grid=(bsz, t_len // tm),
        in_specs=[tok(qk), tok(qk), tok(nv), tok(nv), tok(d),
                  pl.BlockSpec((1, 3 * N_SUB, d), lambda b, t: (b, 0, 0)), _resident(ng.shape), _resident(w_o.shape)],
        out_specs=tok(d),
        out_shape=jax.ShapeDtypeStruct(res.shape, F32),
        scratch_shapes=[pltpu.VMEM((RET_H, RET_DK, RET_DV), F32), pltpu.VMEM((tm, nv), BF16)],
        compiler_params=_params(2),
        name="ret_mix",
    )(q, k, v, g, res, mod, ng, w_o)


def _lru_kernel(res_ref, mod_ref, ng_ref, win_ref, cw_ref, vec_ref, gw_ref, wo_ref, out_ref, xext, h_carry):
    tm = res_ref.shape[1]
    width = xext.shape[1]

    @pl.when(pl.program_id(1) == 0)
    def _():
        xext[0:SUBLANES, :] = jnp.zeros((SUBLANES, width), F32)
        h_carry[...] = jnp.zeros(h_carry.shape, F32)

    mod = mod_ref[0]
    ng = ng_ref[...]
    cw = cw_ref[...]
    vec = vec_ref[...]
    conv_b, gate_bi, gate_br, lam = vec[0:1], vec[1:2], vec[2:3], vec[3:4]
    neg_lam = -lam
    softplus = jnp.maximum(neg_lam, 0.0) + jnp.log1p(jnp.exp(-jnp.abs(neg_lam)))
    parts = [slice(i * tm // LRU_ROW_PARTS, (i + 1) * tm // LRU_ROW_PARTS) for i in range(LRU_ROW_PARTS)]
    res = [res_ref[0, rows, :] for rows in parts]
    h = [_sublayer_in(x, mod, ng, 1).astype(BF16) for x in res]
    gate_branch = [jnp.dot(x, win_ref[:, :width], preferred_element_type=F32) for x in h]
    xb = [jnp.dot(x, win_ref[:, width:], preferred_element_type=F32) for x in h]
    for rows, x in zip(parts, xb):
        xext[SUBLANES + rows.start:SUBLANES + rows.stop, :] = x
    xc = []
    for rows, x in zip(parts, xb):
        acc = conv_b + cw[CONV_W - 1:CONV_W] * x
        for j in range(CONV_W - 1):
            lo = SUBLANES - (CONV_W - 1) + j
            acc = acc + cw[j:j + 1] * xext[lo + rows.start:lo + rows.stop, :]
        xc.append(acc)
    xext[0:SUBLANES, :] = xext[tm:tm + SUBLANES, :]

    a, u = [], []
    for x in xc:
        xcb = x.astype(BF16)
        gates = [jnp.concatenate([jnp.dot(xcb[:, hd * LRU_BW:(hd + 1) * LRU_BW], gw_ref[gi * LRU_H + hd],
                                          preferred_element_type=F32) for hd in range(LRU_H)], axis=1)
                 for gi in range(2)]
        i_gate = _sigmoid(gates[0] + gate_bi)
        r_gate = _sigmoid(gates[1] + gate_br)
        log_a = -LRU_C * r_gate * softplus
        a_p = jnp.exp(log_a)
        a.append(a_p)
        u.append(jnp.sqrt(-jnp.tanh(log_a) * (a_p * a_p + 1.0)) * (i_gate * x))

    row = lax.broadcasted_iota(jnp.int32, (SUBLANES, width), 0)
    carry = h_carry[...]
    hs = []
    for a_p, u_p in zip(a, u):
        groups = []
        for i in range(a_p.shape[0] // SUBLANES):
            ag = a_p[i * SUBLANES:(i + 1) * SUBLANES]
            ug = u_p[i * SUBLANES:(i + 1) * SUBLANES]
            for d in (1, 2, 4):
                keep = row >= d
                u_prev = jnp.where(keep, pltpu.roll(ug, d, 0), 0.0)
                a_prev = jnp.where(keep, pltpu.roll(ag, d, 0), 1.0)
                ug = ug + ag * u_prev
                ag = ag * a_prev
            hg = ug + ag * carry
            groups.append(hg)
            carry = jnp.broadcast_to(hg[SUBLANES - 1:SUBLANES, :], (SUBLANES, width))
        hs.append(jnp.concatenate(groups, axis=0))
    h_carry[...] = carry

    for rows, x, gb, hp in zip(parts, res, gate_branch, hs):
        gelu = 0.5 * gb * (1.0 + jnp.tanh(math.sqrt(2.0 / math.pi) * (gb + 0.044715 * (gb * gb * gb))))
        y = jnp.dot((gelu * hp).astype(BF16), wo_ref[...], preferred_element_type=F32)
        out_ref[0, rows, :] = _sublayer_out(x, y, mod, ng, 1, 1.0)


def _lru_sublayer(res, mod, ng, w_in, conv_w, vec, gate_w, w_o):
    bsz, t_len, d = res.shape
    width = w_o.shape[0]
    tm = _tile(t_len, LRU_TILE)
    tok = pl.BlockSpec((1, tm, d), lambda b, t: (b, t, 0))
    return pl.pallas_call(
        _lru_kernel,
        grid=(bsz, t_len // tm),
        in_specs=[tok, pl.BlockSpec((1, 3 * N_SUB, d), lambda b, t: (b, 0, 0)), _resident(ng.shape),
                  _resident(w_in.shape), _resident(conv_w.shape), _resident(vec.shape),
                  _resident(gate_w.shape), _resident(w_o.shape)],
        out_specs=tok,
        out_shape=jax.ShapeDtypeStruct(res.shape, F32),
        scratch_shapes=[pltpu.VMEM((tm + SUBLANES, width), F32), pltpu.VMEM((SUBLANES, width), F32)],
        compiler_params=_params(2),
        name="lru_sublayer",
    )(res, mod, ng, w_in, conv_w, vec, gate_w, w_o)


def kernel(x, c, positions, ada_w, ada_b, norm_g, ffn_w_in, ffn_w_out, rwkv_mu, rwkv_w_rkv, rwkv_w0, rwkv_w1, rwkv_w2, rwkv_a0, rwkv_a1, rwkv_a2, rwkv_g1, rwkv_g2, rwkv_k_k, rwkv_k_a, rwkv_r_k, rwkv_ln_w, rwkv_ln_b, rwkv_w_o, rwkv_v0, rwkv_v1, rwkv_v2, ret_w_in, ret_w_o, lru_w_in, lru_conv_w, lru_conv_b, lru_gate_w, lru_gate_b, lru_lambda, lru_w_o):
    depth = ada_w.shape[0]
    d_ff = ffn_w_out.shape[2]
    res = x.astype(F32)
    mod = _ada_mod(c.astype(F32), ada_w, ada_b)
    pos_f = positions.astype(F32)[..., None]
    v_first = None
    for i in range(depth):
        ng = norm_g[i]
        ffn = lambda res, m, s: _ffn_sublayer(
            res, mod[i], ng, ffn_w_in[i, m, :, :d_ff].astype(BF16), ffn_w_in[i, m, :, d_ff:].astype(BF16),
            ffn_w_out[i, m].astype(BF16), s)
        res = ffn(res, 0, 0)
        kind, j = i % 3, i // 3
        if kind == 0:
            p = dict(mu=rwkv_mu[j], w_rkv=rwkv_w_rkv[j], w0=rwkv_w0[j], w1=rwkv_w1[j], w2=rwkv_w2[j],
                     a0=rwkv_a0[j], a1=rwkv_a1[j], a2=rwkv_a2[j], g1=rwkv_g1[j], g2=rwkv_g2[j],
                     k_k=rwkv_k_k[j], k_a=rwkv_k_a[j], r_k=rwkv_r_k[j], ln_w=rwkv_ln_w[j], ln_b=rwkv_ln_b[j])
            if j > 0:
                p.update(v0=rwkv_v0[j - 1], v1=rwkv_v1[j - 1], v2=rwkv_v2[j - 1])
            p["w_o"] = rwkv_w_o[j]
            res, v_out = _rwkv_sublayer(res, mod[i], ng, p, v_first if j > 0 else None, j == 0)
            if j == 0:
                v_first = v_out
        elif kind == 1:
            q, k, v, g = _ret_pre(res, mod[i], ng, pos_f, ret_w_in[j].astype(BF16))
            res = _ret_mix(q, k, v, g, res, mod[i], ng, ret_w_o[j].astype(BF16))
        else:
            width = lru_w_o.shape[1]
            vec = jnp.stack([lru_conv_b[j], lru_gate_b[j, 0].reshape(width), lru_gate_b[j, 1].reshape(width),
                             lru_lambda[j]])
            gate_w = lru_gate_w[j].reshape(2 * LRU_H, LRU_BW, LRU_BW).astype(BF16)
            res = _lru_sublayer(res, mod[i], ng, lru_w_in[j].astype(BF16), lru_conv_w[j], vec, gate_w,
                                lru_w_o[j].astype(BF16))
        res = ffn(res, 1, 2)
    return res.astype(x.dtype)
```

```python
import functools
import math

import jax
import jax.numpy as jnp
from jax import lax
from jax.experimental import pallas as pl
from jax.experimental.pallas import tpu as pltpu

F32 = jnp.float32
BF16 = jnp.bfloat16

NORM_EPS = 1e-6
N_SUB = 3

MXU_WIDTH = 256
SUBLANES = 8

RWKV_N = 64
RWKV_GROUP = MXU_WIDTH // RWKV_N
RWKV_GW = RWKV_GROUP * RWKV_N
RWKV_CHUNK = 64
RWKV_GN_EPS = 64e-5
LORA_PAD = 128

RET_H = 4
RET_DK = 256
RET_DV = 512
RET_CHUNK = 256
ROPE_BASE = 10000.0
HEAD_NORM_EPS = 1e-5

LRU_H = 5
LRU_BW = 256
CONV_W = 4
LRU_C = 8.0

VMEM_LIMIT = 56 * 1024 * 1024
ADA_TILE = 1536
FFN_TILE = 1024
FFN_ROW_PARTS = 4
RWKV_TILE = 256
RWKV_LEAD_CHUNKS = 4
RET_TILE = 512
RET_PRE_PARTS = 2
LRU_TILE = 512
LRU_ROW_PARTS = 4

NT_DIMS = (((1,), (1,)), ((), ()))
TN_DIMS = (((0,), (0,)), ((), ()))


def _mm(a, b):
    return jnp.dot(a.astype(BF16), b.astype(BF16), preferred_element_type=F32)


def _mm_nt(a, b):
    return lax.dot_general(a.astype(BF16), b.astype(BF16), NT_DIMS, preferred_element_type=F32)


def _mm_tn(a, b):
    return lax.dot_general(a.astype(BF16), b.astype(BF16), TN_DIMS, preferred_element_type=F32)


def _mm_split_lhs_exact(w, x):
    hi = x.astype(BF16)
    lo = (x - hi.astype(F32)).astype(BF16)
    return (jnp.dot(w, hi, preferred_element_type=F32) + jnp.dot(w, lo, preferred_element_type=F32))


def _rms(x, g):
    return x * lax.rsqrt(jnp.mean(x * x, axis=-1, keepdims=True) + NORM_EPS) * g


def _sublayer_in(res, mod, ng, s):
    return _rms(res, ng[2 * s:2 * s + 1]) * (1.0 + mod[3 * s + 1:3 * s + 2]) + mod[3 * s:3 * s + 1]


def _sublayer_out(res, y, mod, ng, s, weight):
    return res + weight * mod[3 * s + 2:3 * s + 3] * _rms(y, ng[2 * s + 1:2 * s + 2])


def _sigmoid(x):
    return jax.nn.sigmoid(x)


def _resident(shape):
    nd = len(shape)
    return pl.BlockSpec(shape, lambda *_: (0,) * nd, pipeline_mode=pl.Buffered(1))


def _params(n_axes):
    return pltpu.CompilerParams(dimension_semantics=("arbitrary",) * n_axes, vmem_limit_bytes=VMEM_LIMIT)


def _tile(t_len, want):
    tm = min(want, t_len)
    assert t_len % tm == 0, (t_len, tm)
    return tm


def _ada_kernel(c_ref, w_ref, b_ref, o_ref):
    c = c_ref[...]
    cond = c * _sigmoid(c)
    o_ref[0] = _mm(cond, w_ref[0]) + b_ref[0]


def _ada_mod(c, ada_w, ada_b):
    depth, d, n = ada_w.shape
    bsz = c.shape[0]
    tn = ADA_TILE
    assert n % tn == 0
    out = pl.pallas_call(
        _ada_kernel,
        grid=(depth, n // tn),
        in_specs=[pl.BlockSpec((bsz, d), lambda l, j: (0, 0)),
                  pl.BlockSpec((1, d, tn), lambda l, j: (l, 0, j)),
                  pl.BlockSpec((1, 1, tn), lambda l, j: (l, 0, j))],
        out_specs=pl.BlockSpec((1, bsz, tn), lambda l, j: (l, 0, j)),
        out_shape=jax.ShapeDtypeStruct((depth, bsz, n), F32),
        compiler_params=_params(2),
        name="ada_mod",
    )(c, ada_w, ada_b.reshape(depth, 1, n))
    return out.reshape(depth, bsz, 3 * N_SUB, d)


def _ffn_kernel(s, res_ref, mod_ref, ng_ref, wa_ref, wb_ref, wo_ref, out_ref):
    mod = mod_ref[0]
    ng = ng_ref[...]
    f = wa_ref.shape[1]
    cut = -(-f // (2 * MXU_WIDTH)) * MXU_WIDTH
    chunks = [slice(0, cut), slice(cut, f)]
    tm = res_ref.shape[1]
    parts = [slice(i * tm // FFN_ROW_PARTS, (i + 1) * tm // FFN_ROW_PARTS) for i in range(FFN_ROW_PARTS)]
    res = [res_ref[0, rows, :] for rows in parts]
    h = [_sublayer_in(x, mod, ng, s).astype(BF16) for x in res]
    y = [jnp.zeros(x.shape, F32) for x in res]
    for cols in chunks:
        a = [jnp.dot(x, wa_ref[:, cols], preferred_element_type=F32) for x in h]
        b = [jnp.dot(x, wb_ref[:, cols], preferred_element_type=F32) for x in h]
        z = [(u * _sigmoid(u) * v).astype(BF16) for u, v in zip(a, b)]
        y = [acc + jnp.dot(x, wo_ref[cols, :], preferred_element_type=F32) for acc, x in zip(y, z)]
    for i, rows in enumerate(parts):
        out_ref[0, rows, :] = _sublayer_out(res[i], y[i], mod, ng, s, 0.5)


def _ffn_sublayer(res, mod, ng, w_a, w_b, w_o, s):
    bsz, t_len, d = res.shape
    f = w_a.shape[1]
    tm = _tile(t_len, FFN_TILE)
    return pl.pallas_call(
        functools.partial(_ffn_kernel, s),
        grid=(bsz, t_len // tm),
        in_specs=[pl.BlockSpec((1, tm, d), lambda b, t: (b, t, 0)),
                  pl.BlockSpec((1, 3 * N_SUB, d), lambda b, t: (b, 0, 0)),
                  _resident(ng.shape), _resident((d, f)), _resident((d, f)), _resident((f, d))],
        out_specs=pl.BlockSpec((1, tm, d), lambda b, t: (b, t, 0)),
        out_shape=jax.ShapeDtypeStruct(res.shape, F32),
        compiler_params=_params(2),
        name="ffn_sublayer",
    )(res, mod, ng, w_a, w_b, w_o)


RWKV_STREAMS = ("r", "lw", "k", "v", "kk", "a", "g")


def _rwkv_pre_stages(has_vres, res, mod, ng_ref, mu_ref, vec_ref, w, vfirst, hext, store):
    tm, d = res.shape
    h = _sublayer_in(res, mod, ng_ref[...], 1)
    hext[SUBLANES:SUBLANES + tm, :] = h
    dx = hext[SUBLANES - 1:SUBLANES - 1 + tm, :] - h
    hext[0:SUBLANES, :] = hext[tm:tm + SUBLANES, :]
    mu = mu_ref[...]
    xr, xw, xk, xv, xa, xg = ((h + dx * mu[j:j + 1]).astype(BF16) for j in range(6))
    vec = vec_ref[...]
    w0, a0, k_k, k_a = vec[0:1], vec[1:2], vec[2:3], vec[3:4]
    full = slice(0, d)
    yield
    tw = jnp.tanh(jnp.dot(xw, w["w1"][...], preferred_element_type=F32))
    ta = jnp.dot(xa, w["a1"][...], preferred_element_type=F32)
    yield
    tg = _sigmoid(jnp.dot(xg, w["g1"][...], preferred_element_type=F32))
    if has_vres:
        tv = jnp.dot(xv, w["v1"][...], preferred_element_type=F32)
    yield
    store("lw", full, -math.exp(-0.5) * _sigmoid(w0 + _mm(tw, w["w2"][...])))
    a = _sigmoid(a0 + _mm(ta, w["a2"][...]))
    store("a", full, a)
    yield
    store("g", full, _mm(tg, w["g2"][...]))
    if has_vres:
        mix = _sigmoid(w["v0"][...] + _mm(tv, w["v2"][...]))
    yield
    for lo in range(0, d, RWKV_GW):
        cols = slice(lo, lo + RWKV_GW)
        store("r", cols, jnp.dot(xr, w["wr"][:, cols], preferred_element_type=F32))
        yield
    for lo in range(0, d, RWKV_GW):
        cols = slice(lo, lo + RWKV_GW)
        k = jnp.dot(xk, w["wk"][:, cols], preferred_element_type=F32)
        store("kk", cols, k * k_k[:, cols])
        store("k", cols, k * (1.0 + (a[:, cols] - 1.0) * k_a[:, cols]))
        yield
    for lo in range(0, d, RWKV_GW):
        cols = slice(lo, lo + RWKV_GW)
        v = jnp.dot(xv, w["wv"][:, cols], preferred_element_type=F32)
        if has_vres:
            v = v + (vfirst[:, cols].astype(F32) - v) * mix[:, cols]
        store("v", cols, v)
        yield


def _pad_cols(w, n):
    return jnp.pad(w, ((0, 0), (0, n - w.shape[1])))


def _pad_rows(w, n):
    return jnp.pad(w, ((0, n - w.shape[0]), (0, 0)))


def _block_diag(x, bd_mask):
    xb = x.astype(BF16)
    return jnp.where(bd_mask, jnp.concatenate([xb] * RWKV_GROUP, axis=0), jnp.zeros((), BF16))


class _Staged:
    def __init__(self, gen):
        self.gen, self.done, self.value = gen, False, None

    def step(self):
        if not self.done:
            try:
                next(self.gen)
            except StopIteration as stop:
                self.done, self.value = True, stop.value

    def finish(self):
        while not self.done:
            self.step()
        return self.value


def _wkv_prepare(raw, masks, consts):
    bd_mask, strict, incl, eye = masks
    tril = consts
    c = RWKV_CHUNK
    bd = lambda x: _block_diag(x, bd_mask)
    cum = [_mm_split_lhs_exact(tril, lw) for r, lw, k, v, kkn, a in raw]
    yield
    units = []
    for (r, lw, k, v, kkn, a), cum_i in zip(raw, cum):
        g_t = jnp.exp(cum_i)
        g_inv = jnp.exp(-cum_i)
        units.append((r * g_t, -kkn * jnp.exp(cum_i - lw), kkn * a * g_inv, k * g_inv, v, g_t[c - 1:c]))
    ar = [jnp.concatenate([at, rt], axis=0).astype(BF16) for rt, at, bt, kt, v, gl in units]
    sv = [bd(u[4]) for u in units]
    lb = [_mm_nt(x, bd(u[2])) for x, u in zip(ar, units)]
    lk = [_mm_nt(x, bd(u[3])) for x, u in zip(ar, units)]
    yield
    l_ab = [jnp.where(strict, x[:c], 0.0) for x in lb]
    l_rb = [jnp.where(incl, x[c:], 0.0) for x in lb]
    l_ak = [jnp.where(strict, x[:c], 0.0) for x in lk]
    l_rk = [jnp.where(incl, x[c:], 0.0) for x in lk]
    pw = [_mm(x, bd(x)) for x in l_ab]
    inv = [eye + x for x in l_ab]
    yield
    for _ in range(4):
        both = [_mm(jnp.concatenate([p, t], axis=0), bd(p)) for p, t in zip(pw, inv)]
        pw = [x[:c] for x in both]
        inv = [t + x[c:] for t, x in zip(inv, both)]
        yield
    inv = [t + _mm(t, bd(p)) for t, p in zip(inv, pw)]
    kv = [_mm(jnp.concatenate([x, y], axis=0), s) for x, y, s in zip(l_ak, l_rk, sv)]
    ak_v = [x[:c] for x in kv]
    rk_v = [x[c:] for x in kv]
    bk = [jnp.concatenate([bt, kt], axis=0).astype(BF16) for rt, at, bt, kt, v, gl in units]
    return [dict(ar=ar[i], inv=inv[i], l_rb=l_rb[i], ak_v=ak_v[i], rk_v=rk_v[i], bk=bk[i], v=units[i][4],
                 g_last=units[i][5]) for i in range(len(units))]


def _wkv_advance(get_prepared, n_chunks, z_states, masks):
    bd_mask = masks[0]
    c = RWKV_CHUNK
    bd = lambda x: _block_diag(x, bd_mask)
    ys = []
    for i in range(n_chunks):
        prepared = get_prepared(i)
        ars = [_mm_nt(p["ar"], z) for p, z in zip(prepared, z_states)]
        yield
        u = [_mm(p["inv"], bd(x[:c] + p["ak_v"])) for p, x in zip(prepared, ars)]
        yield
        y = [x[c:] + _mm(p["l_rb"], bd(uu)) + p["rk_v"] for p, x, uu in zip(prepared, ars, u)]
        dz = [_mm_tn(jnp.concatenate([uu, p["v"]], axis=0), p["bk"]) for p, uu in zip(prepared, u)]
        z_states = [(z + jnp.where(bd_mask, d, 0.0)) * p["g_last"] for z, d, p in zip(z_states, dz, prepared)]
        ys.append(y)
        yield
    return ys, z_states


def _wkv_masks():
    gw, c = RWKV_GW, RWKV_CHUNK
    row = lax.broadcasted_iota(jnp.int32, (gw, gw), 0)
    col = lax.broadcasted_iota(jnp.int32, (gw, gw), 1)
    bd_mask = (row // RWKV_N) == (col // RWKV_N)
    t = lax.broadcasted_iota(jnp.int32, (c, gw), 0)
    j = lax.broadcasted_iota(jnp.int32, (c, gw), 1) % c
    eye = jnp.where(t == j, 1.0, 0.0).astype(F32)
    rc = lax.broadcasted_iota(jnp.int32, (c, c), 0)
    cc = lax.broadcasted_iota(jnp.int32, (c, c), 1)
    tril = jnp.where(rc >= cc, 1.0, 0.0).astype(BF16)
    bd_ones = jnp.where(bd_mask, 1.0, 0.0).astype(BF16)
    return (bd_mask, t > j, t >= j, eye), tril, bd_ones


def _rwkv_kernel(has_vres, emit_vfirst, n_chunks, tiles_per_row, n_tiles, *refs):
    names = ["res_pre", "mod_pre", "res_out", "mod_out", "ng", "mu", "vec4", "wr", "wk", "wv", "w1", "w2", "a1", "a2",
             "g1", "g2"] + (["v1", "v2", "v0", "vfirst"] if has_vres else []) + ["vec3", "wo", "out"]
    names += (["vfirst_out"] if emit_vfirst else []) + ["hext", "state", "z_buf"] + ["p_" + n for n in RWKV_STREAMS]
    ref = dict(zip(names, refs))
    assert len(names) == len(refs)
    step = pl.program_id(0)
    slot_w = step % 2
    slot_r = 1 - slot_w
    pre_tile = jnp.minimum(step, n_tiles - 1)
    wkv_tile = jnp.clip(step - 1, 0, n_tiles - 1)
    hext, state, z_buf = ref["hext"], ref["state"], ref["z_buf"]

    @pl.when(step == 0)
    def _():
        for buf in [z_buf] + [ref["p_" + n] for n in RWKV_STREAMS]:
            buf[...] = jnp.zeros(buf.shape, buf.dtype)

    @pl.when(pre_tile % tiles_per_row == 0)
    def _():
        hext[0:SUBLANES, :] = jnp.zeros((SUBLANES, hext.shape[1]), F32)

    @pl.when(wkv_tile % tiles_per_row == 0)
    def _():
        state[...] = jnp.zeros(state.shape, F32)

    def store(name, cols, value):
        dst = ref["p_" + name]
        dst[slot_w, :, cols] = value.astype(dst.dtype)

    pre = _Staged(_rwkv_pre_stages(has_vres, ref["res_pre"][0], ref["mod_pre"][0], ref["ng"], ref["mu"], ref["vec4"],
                                   ref, ref["vfirst"][0] if has_vres else None, hext, store))

    c = RWKV_CHUNK
    n_groups = state.shape[0]
    masks, tril, bd_ones = _wkv_masks()
    lanes = [slice(gi * RWKV_GW, (gi + 1) * RWKV_GW) for gi in range(n_groups)]
    tb = n_chunks * c
    vec3 = ref["vec3"]

    def head_sums(per_group):
        s = _mm(jnp.concatenate(per_group, axis=0), bd_ones)
        return [s[gi * tb:(gi + 1) * tb] for gi in range(n_groups)]

    load = lambda n: [ref["p_" + n][slot_r, :, ln].astype(F32) for ln in lanes]
    r, lw, k, v, kk, a = (load(n) for n in RWKV_STREAMS[:6])
    kkn = [x / jnp.maximum(jnp.sqrt(n2), 1e-12) for x, n2 in zip(kk, head_sums([x * x for x in kk]))]
    bonus = head_sums([r[gi] * k[gi] * vec3[0:1, lanes[gi]] for gi in range(n_groups)])
    y_out = jnp.dot(z_buf[slot_r], ref["wo"][...], preferred_element_type=F32)
    ref["out"][0] = _sublayer_out(ref["res_out"][0], y_out, ref["mod_out"][0], ref["ng"][...], 1, 1.0)
    if emit_vfirst:
        ref["vfirst_out"][0] = ref["p_v"][slot_w]
    pre.step()

    def raw_units(i):
        rows = slice(i * c, (i + 1) * c)
        return [tuple(x[gi][rows] for x in (r, lw, k, v, kkn, a)) for gi in range(n_groups)]

    lead = min(RWKV_LEAD_CHUNKS, n_chunks)
    first_task = _Staged(_wkv_prepare([u for i in range(lead) for u in raw_units(i)], masks, tril))
    while not first_task.done:
        first_task.step()
        pre.step()
    first = first_task.value
    later = [_Staged(_wkv_prepare(raw_units(i), masks, tril)) for i in range(lead, n_chunks)]

    def get_prepared(i):
        if i < lead:
            return first[i * n_groups:(i + 1) * n_groups]
        return later[i - lead].finish()

    advance = _Staged(_wkv_advance(get_prepared, n_chunks, [state[gi] for gi in range(n_groups)], masks))
    while not advance.done:
        advance.step()
        for task in later:
            if not task.done:
                task.step()
                break
        pre.step()
    ys, z_states = advance.value
    for gi in range(n_groups):
        state[gi] = z_states[gi]
    y = [jnp.concatenate([ys[i][gi] for i in range(n_chunks)], axis=0) for gi in range(n_groups)]
    yc = [x - m * (1.0 / RWKV_N) for x, m in zip(y, head_sums(y))]
    pre.step()
    var = head_sums([x * x for x in yc])
    pre.step()
    for gi, ln in enumerate(lanes):
        yn = yc[gi] * lax.rsqrt(var[gi] * (1.0 / RWKV_N) + RWKV_GN_EPS) * vec3[1:2, ln] + vec3[2:3, ln]
        z_buf[slot_w, :, ln] = ((yn + bonus[gi] * v[gi]) * ref["p_g"][slot_r, :, ln].astype(F32)).astype(BF16)
    pre.finish()


def _rwkv_sublayer(res, mod, ng, p, v_first, emit_vfirst):
    bsz, t_len, d = res.shape
    n_groups = d // RWKV_GW
    tb = _tile(t_len, RWKV_TILE)
    per_row = t_len // tb
    n_tiles = bsz * per_row
    has_vres = v_first is not None
    pre = lambda s: jnp.minimum(s, n_tiles - 1)
    done = lambda s: jnp.maximum(s - 2, 0)
    tok_pre = pl.BlockSpec((1, tb, d), lambda s: (pre(s) // per_row, pre(s) % per_row, 0))
    tok_out = pl.BlockSpec((1, tb, d), lambda s: (done(s) // per_row, done(s) % per_row, 0))
    mod_pre = pl.BlockSpec((1, 3 * N_SUB, d), lambda s: (pre(s) // per_row, 0, 0))
    mod_out = pl.BlockSpec((1, 3 * N_SUB, d), lambda s: (done(s) // per_row, 0, 0))
    lora_in = lambda w: _pad_cols(w, LORA_PAD).astype(BF16)
    lora_out = lambda w: _pad_rows(w, LORA_PAD).astype(BF16)
    vec4 = jnp.stack([p["w0"], p["a0"], p["k_k"], p["k_a"]])
    vec3 = jnp.stack([p["r_k"].reshape(d), p["ln_w"], p["ln_b"]])
    weights = [ng, p["mu"], vec4, p["w_rkv"][0].astype(BF16), p["w_rkv"][1].astype(BF16), p["w_rkv"][2].astype(BF16),
               lora_in(p["w1"]), lora_out(p["w2"]), lora_in(p["a1"]), lora_out(p["a2"]),
               lora_in(p["g1"]), lora_out(p["g2"])]
    args = [res, mod, res, mod] + weights
    in_specs = [tok_pre, mod_pre, tok_out, mod_out] + [_resident(x.shape) for x in weights]
    if has_vres:
        extra = [lora_in(p["v1"]), lora_out(p["v2"]), p["v0"].reshape(1, d)]
        args += extra + [v_first]
        in_specs += [_resident(x.shape) for x in extra] + [tok_pre]
    tail = [vec3, p["w_o"].astype(BF16)]
    args += tail
    in_specs += [_resident(x.shape) for x in tail]
    out_shape = [jax.ShapeDtypeStruct(res.shape, F32)]
    out_specs = [tok_out]
    if emit_vfirst:
        out_shape.append(jax.ShapeDtypeStruct(res.shape, BF16))
        out_specs.append(tok_out)
    streams = [pltpu.VMEM((2, tb, d), F32 if n == "lw" else BF16) for n in RWKV_STREAMS]
    outs = pl.pallas_call(
        functools.partial(_rwkv_kernel, has_vres, emit_vfirst, tb // RWKV_CHUNK, per_row, n_tiles),
        grid=(n_tiles + 2,),
        in_specs=in_specs,
        out_specs=out_specs,
        out_shape=out_shape,
        scratch_shapes=[pltpu.VMEM((tb + SUBLANES, d), F32), pltpu.VMEM((n_groups, RWKV_GW, RWKV_GW), F32),
                        pltpu.VMEM((2, tb, d), BF16)] + streams,
        compiler_params=_params(1),
        name="rwkv_sublayer",
    )(*args)
    return outs[0], (outs[1] if emit_vfirst else None)


def _ret_pre_kernel(res_ref, mod_ref, ng_ref, pos_ref, inv_ref, w_ref, q_ref, k_ref, v_ref, g_ref):
    tm = res_ref.shape[1]
    half = RET_DK // 2
    qk = RET_H * RET_DK
    nv = RET_H * RET_DV
    mod = mod_ref[0]
    ng = ng_ref[...]
    for part in range(RET_PRE_PARTS):
        rows = slice(part * tm // RET_PRE_PARTS, (part + 1) * tm // RET_PRE_PARTS)
        h = _sublayer_in(res_ref[0, rows, :], mod, ng, 1).astype(BF16)
        for hd in range(RET_H):
            lo = 2 * qk + hd * RET_DV
            v_ref[0, rows, hd * RET_DV:(hd + 1) * RET_DV] = jnp.dot(
                h, w_ref[:, lo:lo + RET_DV], preferred_element_type=F32).astype(BF16)
            g_ref[0, rows, hd * RET_DV:(hd + 1) * RET_DV] = jnp.dot(
                h, w_ref[:, lo + nv:lo + nv + RET_DV], preferred_element_type=F32).astype(BF16)
        ang = pos_ref[0, rows, :] * inv_ref[...]
        cos = jnp.cos(ang)
        sin = jnp.sin(ang)
        for which, out_ref, scale in ((0, q_ref, 1.0), (1, k_ref, RET_DK ** -0.5)):
            for hd in range(RET_H):
                lo = which * qk + hd * RET_DK
                x = jnp.dot(h, w_ref[:, lo:lo + RET_DK], preferred_element_type=F32)
                x1, x2 = x[:, :half], x[:, half:]
                out_ref[0, rows, hd * RET_DK:hd * RET_DK + half] = ((x1 * cos - x2 * sin) * scale).astype(BF16)
                out_ref[0, rows, hd * RET_DK + half:(hd + 1) * RET_DK] = ((x1 * sin + x2 * cos) * scale).astype(BF16)


def _ret_pre(res, mod, ng, pos_f, w_in):
    bsz, t_len, d = res.shape
    tm = _tile(t_len, RET_TILE)
    qk = RET_H * RET_DK
    nv = RET_H * RET_DV
    half = RET_DK // 2
    inv = (1.0 / (ROPE_BASE ** jnp.linspace(0.0, 1.0, half, dtype=F32))).reshape(1, half)
    tok = lambda n: pl.BlockSpec((1, tm, n), lambda b, t: (b, t, 0))
    return pl.pallas_call(
        _ret_pre_kernel,
        grid=(bsz, t_len // tm),
        in_specs=[tok(d), pl.BlockSpec((1, 3 * N_SUB, d), lambda b, t: (b, 0, 0)), _resident(ng.shape),
                  tok(1), _resident((1, half)), _resident(w_in.shape)],
        out_specs=[tok(qk), tok(qk), tok(nv), tok(nv)],
        out_shape=[jax.ShapeDtypeStruct((bsz, t_len, qk), BF16), jax.ShapeDtypeStruct((bsz, t_len, qk), BF16),
                   jax.ShapeDtypeStruct((bsz, t_len, nv), BF16), jax.ShapeDtypeStruct((bsz, t_len, nv), BF16)],
        compiler_params=_params(2),
        name="ret_pre",
    )(res, mod, ng, pos_f, inv, w_in)


def _ret_mix_kernel(n_chunks, q_ref, k_ref, v_ref, g_ref, res_ref, mod_ref, ng_ref, wo_ref, out_ref, state, z_buf):
    @pl.when(pl.program_id(1) == 0)
    def _():
        state[...] = jnp.zeros(state.shape, F32)

    c = RET_CHUNK
    row = lax.broadcasted_iota(jnp.int32, (c, c), 0)
    col = lax.broadcasted_iota(jnp.int32, (c, c), 1)
    diff = (row - col).astype(F32)
    idx = lax.broadcasted_iota(jnp.int32, (c, 1), 0).astype(F32)
    heads = range(RET_H)
    log_gamma = [math.log(1.0 - 2.0 ** (-5.0 - hd)) for hd in heads]
    inner = [jnp.where(diff >= 0, jnp.exp(lg * jnp.maximum(diff, 0.0)), 0.0) for lg in log_gamma]
    q_decay = [jnp.exp(lg * (idx + 1.0)) for lg in log_gamma]
    k_decay = [jnp.exp(lg * (c - 1.0 - idx)) for lg in log_gamma]
    chunk_decay = [math.exp(lg * c) for lg in log_gamma]
    dk = lambda hd: slice(hd * RET_DK, (hd + 1) * RET_DK)
    dv = lambda hd: slice(hd * RET_DV, (hd + 1) * RET_DV)
    r_state = [state[hd] for hd in heads]
    outs = []
    chunk_rows = [slice(i * c, (i + 1) * c) for i in range(n_chunks)]
    scores = [[_mm_nt(q_ref[0, rows, dk(hd)], k_ref[0, rows, dk(hd)]) * inner[hd] for hd in heads]
              for rows in chunk_rows]
    intra = [[_mm(scores[i][hd], v_ref[0, rows, dv(hd)]) for hd in heads] for i, rows in enumerate(chunk_rows)]
    for i, rows in enumerate(chunk_rows):
        q_c = [q_ref[0, rows, dk(hd)] for hd in heads]
        k_c = [k_ref[0, rows, dk(hd)] for hd in heads]
        v_c = [v_ref[0, rows, dv(hd)] for hd in heads]
        cross = [_mm(q_c[hd], r_state[hd]) * q_decay[hd] for hd in heads]
        o = [intra[i][hd] + cross[hd] for hd in heads]
        kv = [_mm_tn(k_c[hd].astype(F32) * k_decay[hd], v_c[hd]) for hd in heads]
        r_state = [r_state[hd] * chunk_decay[hd] + kv[hd] for hd in heads]
        outs.append(o)
    for hd in heads:
        state[hd] = r_state[hd]
    mod = mod_ref[0]
    ng = ng_ref[...]
    for i, o in enumerate(outs):
        rows = slice(i * c, (i + 1) * c)
        for hd in heads:
            oc = o[hd] - jnp.mean(o[hd], axis=-1, keepdims=True)
            on = oc * lax.rsqrt(jnp.mean(oc * oc, axis=-1, keepdims=True) + HEAD_NORM_EPS)
            gt = g_ref[0, rows, dv(hd)].astype(F32)
            z_buf[rows, dv(hd)] = (gt * _sigmoid(gt) * on).astype(BF16)
        y = jnp.dot(z_buf[rows, :], wo_ref[...], preferred_element_type=F32)
        out_ref[0, rows, :] = _sublayer_out(res_ref[0, rows, :], y, mod, ng, 1, 1.0)


def _ret_mix(q, k, v, g, res, mod, ng, w_o):
    bsz, t_len, d = res.shape
    tm = _tile(t_len, RET_TILE)
    qk = RET_H * RET_DK
    nv = RET_H * RET_DV
    tok = lambda n: pl.BlockSpec((1, tm, n), lambda b, t: (b, t, 0))
    return pl.pallas_call(
        functools.partial(_ret_mix_kernel, tm // RET_CHUNK),
        grid=(bsz, t_len // tm),
        in_specs=[tok(qk), tok(qk), tok(nv), tok(nv), tok(d),
                  pl.BlockSpec((1, 3 * N_SUB, d), lambda b, t: (b, 0, 0)), _resident(ng.shape), _resident(w_o.shape)],
        out_specs=tok(d),
        out_shape=jax.ShapeDtypeStruct(res.shape, F32),
        scratch_shapes=[pltpu.VMEM((RET_H, RET_DK, RET_DV), F32), pltpu.VMEM((tm, nv), BF16)],
        compiler_params=_params(2),
        name="ret_mix",
    )(q, k, v, g, res, mod, ng, w_o)


def _lru_kernel(res_ref, mod_ref, ng_ref, win_ref, cw_ref, vec_ref, gw_ref, wo_ref, out_ref, xext, h_carry):
    tm = res_ref.shape[1]
    width = xext.shape[1]

    @pl.when(pl.program_id(1) == 0)
    def _():
        xext[0:SUBLANES, :] = jnp.zeros((SUBLANES, width), F32)
        h_carry[...] = jnp.zeros(h_carry.shape, F32)

    mod = mod_ref[0]
    ng = ng_ref[...]
    cw = cw_ref[...]
    vec = vec_ref[...]
    conv_b, gate_bi, gate_br, lam = vec[0:1], vec[1:2], vec[2:3], vec[3:4]
    neg_lam = -lam
    softplus = jnp.maximum(neg_lam, 0.0) + jnp.log1p(jnp.exp(-jnp.abs(neg_lam)))
    parts = [slice(i * tm // LRU_ROW_PARTS, (i + 1) * tm // LRU_ROW_PARTS) for i in range(LRU_ROW_PARTS)]
    res = [res_ref[0, rows, :] for rows in parts]
    h = [_sublayer_in(x, mod, ng, 1).astype(BF16) for x in res]
    gate_branch = [jnp.dot(x, win_ref[:, :width], preferred_element_type=F32) for x in h]
    xb = [jnp.dot(x, win_ref[:, width:], preferred_element_type=F32) for x in h]
    for rows, x in zip(parts, xb):
        xext[SUBLANES + rows.start:SUBLANES + rows.stop, :] = x
    xc = []
    for rows, x in zip(parts, xb):
        acc = conv_b + cw[CONV_W - 1:CONV_W] * x
        for j in range(CONV_W - 1):
            lo = SUBLANES - (CONV_W - 1) + j
            acc = acc + cw[j:j + 1] * xext[lo + rows.start:lo + rows.stop, :]
        xc.append(acc)
    xext[0:SUBLANES, :] = xext[tm:tm + SUBLANES, :]

    a, u = [], []
    for x in xc:
        xcb = x.astype(BF16)
        gates = [jnp.concatenate([jnp.dot(xcb[:, hd * LRU_BW:(hd + 1) * LRU_BW], gw_ref[gi * LRU_H + hd],
                                          preferred_element_type=F32) for hd in range(LRU_H)], axis=1)
                 for gi in range(2)]
        i_gate = _sigmoid(gates[0] + gate_bi)
        r_gate = _sigmoid(gates[1] + gate_br)
        log_a = -LRU_C * r_gate * softplus
        a_p = jnp.exp(log_a)
        a.append(a_p)
        u.append(jnp.sqrt(-jnp.tanh(log_a) * (a_p * a_p + 1.0)) * (i_gate * x))

    row = lax.broadcasted_iota(jnp.int32, (SUBLANES, width), 0)
    carry = h_carry[...]
    hs = []
    for a_p, u_p in zip(a, u):
        groups = []
        for i in range(a_p.shape[0] // SUBLANES):
            ag = a_p[i * SUBLANES:(i + 1) * SUBLANES]
            ug = u_p[i * SUBLANES:(i + 1) * SUBLANES]
            for d in (1, 2, 4):
                keep = row >= d
                u_prev = jnp.where(keep, pltpu.roll(ug, d, 0), 0.0)
                a_prev = jnp.where(keep, pltpu.roll(ag, d, 0), 1.0)
                ug = ug + ag * u_prev
                ag = ag * a_prev
            hg = ug + ag * carry
            groups.append(hg)
            carry = jnp.broadcast_to(hg[SUBLANES - 1:SUBLANES, :], (SUBLANES, width))
        hs.append(jnp.concatenate(groups, axis=0))
    h_carry[...] = carry

    for rows, x, gb, hp in zip(parts, res, gate_branch, hs):
        gelu = 0.5 * gb * (1.0 + jnp.tanh(math.sqrt(2.0 / math.pi) * (gb + 0.044715 * (gb * gb * gb))))
        y = jnp.dot((gelu * hp).astype(BF16), wo_ref[...], preferred_element_type=F32)
        out_ref[0, rows, :] = _sublayer_out(x, y, mod, ng, 1, 1.0)


def _lru_sublayer(res, mod, ng, w_in, conv_w, vec, gate_w, w_o):
    bsz, t_len, d = res.shape
    width = w_o.shape[0]
    tm = _tile(t_len, LRU_TILE)
    tok = pl.BlockSpec((1, tm, d), lambda b, t: (b, t, 0))
    return pl.pallas_call(
        _lru_kernel,
        grid=(bsz, t_len // tm),
        in_specs=[tok, pl.BlockSpec((1, 3 * N_SUB, d), lambda b, t: (b, 0, 0)), _resident(ng.shape),
                  _resident(w_in.shape), _resident(conv_w.shape), _resident(vec.shape),
                  _resident(gate_w.shape), _resident(w_o.shape)],
        out_specs=tok,
        out_shape=jax.ShapeDtypeStruct(res.shape, F32),
        scratch_shapes=[pltpu.VMEM((tm + SUBLANES, width), F32), pltpu.VMEM((SUBLANES, width), F32)],
        compiler_params=_params(2),
        name="lru_sublayer",
    )(res, mod, ng, w_in, conv_w, vec, gate_w, w_o)


def kernel(x, c, positions, ada_w, ada_b, norm_g, ffn_w_in, ffn_w_out, rwkv_mu, rwkv_w_rkv, rwkv_w0, rwkv_w1, rwkv_w2, rwkv_a0, rwkv_a1, rwkv_a2, rwkv_g1, rwkv_g2, rwkv_k_k, rwkv_k_a, rwkv_r_k, rwkv_ln_w, rwkv_ln_b, rwkv_w_o, rwkv_v0, rwkv_v1, rwkv_v2, ret_w_in, ret_w_o, lru_w_in, lru_conv_w, lru_conv_b, lru_gate_w, lru_gate_b, lru_lambda, lru_w_o):
    depth = ada_w.shape[0]
    d_ff = ffn_w_out.shape[2]
    res = x.astype(F32)
    mod = _ada_mod(c.astype(F32), ada_w, ada_b)
    pos_f = positions.astype(F32)[..., None]
    v_first = None
    for i in range(depth):
        ng = norm_g[i]
        ffn = lambda res, m, s: _ffn_sublayer(
            res, mod[i], ng, ffn_w_in[i, m, :, :d_ff].astype(BF16), ffn_w_in[i, m, :, d_ff:].astype(BF16),
            ffn_w_out[i, m].astype(BF16), s)
        res = ffn(res, 0, 0)
        kind, j = i % 3, i // 3
        if kind == 0:
            p = dict(mu=rwkv_mu[j], w_rkv=rwkv_w_rkv[j], w0=rwkv_w0[j], w1=rwkv_w1[j], w2=rwkv_w2[j],
                     a0=rwkv_a0[j], a1=rwkv_a1[j], a2=rwkv_a2[j], g1=rwkv_g1[j], g2=rwkv_g2[j],
                     k_k=rwkv_k_k[j], k_a=rwkv_k_a[j], r_k=rwkv_r_k[j], ln_w=rwkv_ln_w[j], ln_b=rwkv_ln_b[j])
            if j > 0:
                p.update(v0=rwkv_v0[j - 1], v1=rwkv_v1[j - 1], v2=rwkv_v2[j - 1])
            p["w_o"] = rwkv_w_o[j]
            res, v_out = _rwkv_sublayer(res, mod[i], ng, p, v_first if j > 0 else None, j == 0)
            if j == 0:
                v_first = v_out
        elif kind == 1:
            q, k, v, g = _ret_pre(res, mod[i], ng, pos_f, ret_w_in[j].astype(BF16))
            res = _ret_mix(q, k, v, g, res, mod[i], ng, ret_w_o[j].astype(BF16))
        else:
            width = lru_w_o.shape[1]
            vec = jnp.stack([lru_conv_b[j], lru_gate_b[j, 0].reshape(width), lru_gate_b[j, 1].reshape(width),
                             lru_lambda[j]])
            gate_w = lru_gate_w[j].reshape(2 * LRU_H, LRU_BW, LRU_BW).astype(BF16)
            res = _lru_sublayer(res, mod[i], ng, lru_w_in[j].astype(BF16), lru_conv_w[j], vec, gate_w,
                                lru_w_o[j].astype(BF16))
        res = ffn(res, 1, 2)
    return res.astype(x.dtype)
```
